```python
import math
import jax
import jax.numpy as jnp
from jax import lax
import numpy as np

D_MODEL = 1024
BATCH = 32
SEQ = 2048
DEPTH = 4
DEC_BATCH = 8
DEC_SEQ = 2048
PAST_LEN = 128

F32 = jnp.float32
EPS = 1e-6
N_BRANCH = 4
W_BR = D_MODEL // 2
W_A = W_BR
W_B = W_BR
W_C = W_BR
W_D = W_BR
HY_ORDER = 2
HY_SHORT = 3
HY_BANDS = 16
HY_EMB = 1 + 2 * HY_BANDS
HY_FFN = 64
HY_SIN_FREQ = 1.0
HY_TARGET = 1e-2
HY_FAST_PCT = 0.3
HY_SLOW_PCT = 1.5
HY_MIN_DECAY = -math.log(HY_TARGET) / HY_SLOW_PCT
HY_MAX_DECAY = -math.log(HY_TARGET) / HY_FAST_PCT
SGU_CHUNK = 128
SGU_GROUPS = 4
SGU_GW = W_B // SGU_GROUPS
RG_HEADS = 8
RG_HD = W_C // RG_HEADS
RG_CONV = 4
RG_C = 8.0
ML_HEADS = 4
ML_HD = W_D // ML_HEADS
ML_CHUNK = 128
D_FF = 2816
N_EXPERTS = 8
TOP_K = 2
D_FF_EXPERT = 1408
N_DENSE = (DEPTH + 1) // 2
N_MOE = DEPTH // 2
N_IN_HY = 3 * W_A
N_IN_SGU = 2 * W_B
N_IN_RG = 2 * W_C
N_IN_ML = 4 * W_D + 4 * ML_HEADS
N_IN_GATE = N_BRANCH * D_MODEL
N_IN = N_IN_HY + N_IN_SGU + N_IN_RG + N_IN_ML + N_IN_GATE
SPLIT_POINTS = (N_IN_HY, N_IN_HY + N_IN_SGU, N_IN_HY + N_IN_SGU + N_IN_RG,
                N_IN_HY + N_IN_SGU + N_IN_RG + N_IN_ML)

kernel_name = 'hybrid_gated_parallel_encoder'


def _rmsnorm(x, g):
    xf = x.astype(F32)
    return (xf * lax.rsqrt(jnp.mean(xf * xf, axis=-1, keepdims=True) + EPS) * g).astype(x.dtype)


def _depthwise_conv(x, w, b, pad_left):
    K = w.shape[0]
    L = x.shape[1]
    xp = jnp.pad(x, ((0, 0), (pad_left, K - 1 - pad_left), (0, 0)))
    y = b + xp[:, 0:L] * w[0]
    for j in range(1, K):
        y = y + xp[:, j:j + L] * w[j]
    return y


def _hyena_filters(L, w1, b1, w2, b2, w3):
    t = jnp.arange(L, dtype=F32)
    t_norm = t / L
    bands = jnp.arange(1, HY_BANDS + 1, dtype=F32)
    ang = (2.0 * math.pi / L) * t[:, None] * bands[None, :]
    z = jnp.concatenate([t_norm[:, None], jnp.cos(ang), jnp.sin(ang)], axis=-1)
    h = jnp.sin(HY_SIN_FREQ * (z @ w1 + b1))
    h = jnp.sin(HY_SIN_FREQ * (h @ w2 + b2))
    h = (h @ w3).astype(F32).reshape(L, 2 * HY_ORDER, W_A)
    deltas = jnp.linspace(HY_MIN_DECAY, HY_MAX_DECAY, W_A, dtype=F32)
    return h * jnp.exp(-t_norm[:, None, None] * deltas[None, None, :])


def _fft_long_conv(u, k_fwd, k_bwd, bias):
    L = u.shape[1]
    k2 = jnp.concatenate([k_fwd, jnp.zeros_like(k_fwd[:1]), k_bwd[:0:-1]], axis=0)
    kf = jnp.fft.rfft(k2, axis=0)
    uf = jnp.fft.rfft(u, n=2 * L, axis=1)
    y = jnp.fft.irfft(uf * kf[None], n=2 * L, axis=1)[:, :L]
    return y + u * bias


def _hyena_mixer(p, conv_w, conv_b, w1, b1, w2, b2, w3, bias):
    L = p.shape[1]
    u = _depthwise_conv(p.astype(F32), conv_w, conv_b, HY_SHORT // 2)
    z, x1, x2 = jnp.split(u, 3, axis=-1)
    filt = _hyena_filters(L, w1, b1, w2, b2, w3)
    for o, gate in enumerate((x1, x2)):
        z = gate * _fft_long_conv(z, filt[:, 2 * o], filt[:, 2 * o + 1], bias[o])
    return z


def _sgu_mixer(p, ln_g, ln_b, ws, bs):
    B, L, _ = p.shape
    u, v = jnp.split(p.astype(F32), 2, axis=-1)
    mu = jnp.mean(v, axis=-1, keepdims=True)
    var = jnp.mean(jnp.square(v - mu), axis=-1, keepdims=True)
    v = (v - mu) * lax.rsqrt(var + EPS) * ln_g + ln_b
    v = v.reshape(B, L // SGU_CHUNK, SGU_CHUNK, SGU_GROUPS, SGU_GW)
    v = jnp.einsum('gpq,bnqgc->bnpgc', ws, v) + bs.T[:, :, None]
    return u * v.reshape(B, L, W_B)


def _lin_combine(left, right):
    a_l, b_l = left
    a_r, b_r = right
    return a_l * a_r, a_r * b_l + b_r


def _rglru_mixer(p, conv_w, conv_b, wa, ba, wx, bx, lam):
    B, L, _ = p.shape
    xb, gb = jnp.split(p.astype(F32), 2, axis=-1)
    xc = _depthwise_conv(xb, conv_w, conv_b, RG_CONV // 2)
    xh = xc.reshape(B, L, RG_HEADS, RG_HD)
    out = jnp.zeros_like(xc)
    for d, rev in ((0, False), (1, True)):
        r = jax.nn.sigmoid(jnp.einsum('blhi,hij->blhj', xh, wa[d]) + ba[d]).reshape(B, L, W_C)
        i = jax.nn.sigmoid(jnp.einsum('blhi,hij->blhj', xh, wx[d]) + bx[d]).reshape(B, L, W_C)
        log_a = -RG_C * r * jax.nn.softplus(-lam[d])
        a = jnp.exp(log_a)
        bt = jnp.sqrt(-jnp.expm1(2.0 * log_a)) * (i * xc)
        _, h = lax.associative_scan(_lin_combine, (a, bt), reverse=rev, axis=1)
        out = out + h
    return out * jax.nn.gelu(gb)


def _mlstm_chunkwise(q, k, v, ig, fg):
    B, L, H, Dh = q.shape
    NC = L // ML_CHUNK

    def chunks(t):
        t = t.reshape((B, NC, ML_CHUNK, H) + t.shape[3:])
        return jnp.moveaxis(t, (1, 3), (0, 2))

    logf = jax.nn.log_sigmoid(fg)
    tril = jnp.tril(jnp.ones((ML_CHUNK, ML_CHUNK), dtype=bool))

    def step(carry, inp):
        C, n, m = carry
        qc, kc, vc, lf, li = inp
        b = jnp.cumsum(lf, axis=-1)
        g = b[..., -1]
        dlog = jnp.where(tril, b[..., :, None] - b[..., None, :] + li[..., None, :], -jnp.inf)
        inter = b + m[..., None]
        m_t = jnp.maximum(inter, jnp.max(dlog, axis=-1))
        w_intra = jnp.exp(dlog - m_t[..., None])
        w_inter = jnp.exp(inter - m_t)
        s = jnp.einsum('bhtd,bhsd->bhts', qc, kc) * w_intra
        num = w_inter[..., None] * jnp.einsum('bhvk,bhtk->bhtv', C, qc) + jnp.einsum('bhts,bhsv->bhtv', s, vc)
        den = w_inter * jnp.einsum('bhk,bhtk->bht', n, qc) + jnp.sum(s, axis=-1)
        h = num / jnp.maximum(jnp.abs(den), jnp.exp(-m_t))[..., None]
        wlog = g[..., None] - b + li
        m_new = jnp.maximum(g + m, jnp.max(wlog, axis=-1))
        decay = jnp.exp(g + m - m_new)
        ws = jnp.exp(wlog - m_new[..., None])
        C_new = decay[..., None, None] * C + jnp.einsum('bhs,bhsv,bhsk->bhvk', ws, vc, kc)
        n_new = decay[..., None] * n + jnp.einsum('bhs,bhsk->bhk', ws, kc)
        return (C_new, n_new, m_new), h

    init = (jnp.zeros((B, H, Dh, Dh), F32), jnp.zeros((B, H, Dh), F32), jnp.zeros((B, H), F32))
    _, hs = lax.scan(step, init, (chunks(q), chunks(k), chunks(v), chunks(logf), chunks(ig)))
    return jnp.moveaxis(hs, (0, 2), (1, 3)).reshape(B, L, H, Dh)


def _mlstm_mixer(p, i_bias, f_bias, norm_g):
    B, L, _ = p.shape
    p = p.astype(F32)
    q, k, v, o, gates = jnp.split(p, (W_D, 2 * W_D, 3 * W_D, 4 * W_D), axis=-1)
    q = q.reshape(B, L, ML_HEADS, ML_HD) * (ML_HD ** -0.5)
    k = k.reshape(B, L, ML_HEADS, ML_HD)
    v = v.reshape(B, L, ML_HEADS, ML_HD)
    gates = gates.reshape(B, L, 2, 2, ML_HEADS)
    ig = gates[:, :, 0] + i_bias
    fg = gates[:, :, 1] + f_bias
    h_f = _mlstm_chunkwise(q, k, v, ig[:, :, 0], fg[:, :, 0])
    flip = lambda t: jnp.flip(t, axis=1)
    h_b = flip(_mlstm_chunkwise(flip(q), flip(k), flip(v), flip(ig[:, :, 1]), flip(fg[:, :, 1])))
    h = h_f + h_b
    h = h * lax.rsqrt(jnp.mean(h * h, axis=-1, keepdims=True) + EPS)
    return jax.nn.sigmoid(o) * (h.reshape(B, L, W_D) * norm_g)


def _token_mixer(h, w_in, hy_conv_w, hy_conv_b, hy_ffn_w1, hy_ffn_b1, hy_ffn_w2, hy_ffn_b2, hy_ffn_w3,
                 hy_bias, sgu_ln_g, sgu_ln_b, sgu_ws, sgu_bs, rg_conv_w, rg_conv_b, rg_wa, rg_ba, rg_wx,
                 rg_bx, rg_lambda, ml_i_bias, ml_f_bias, ml_norm_g, w_branch, w_out):
    B, L, _ = h.shape
    proj = h @ w_in
    p_hy, p_sgu, p_rg, p_ml, p_gate = jnp.split(proj, SPLIT_POINTS, axis=-1)
    y_a = _hyena_mixer(p_hy, hy_conv_w, hy_conv_b, hy_ffn_w1, hy_ffn_b1, hy_ffn_w2, hy_ffn_b2, hy_ffn_w3, hy_bias)
    y_b = _sgu_mixer(p_sgu, sgu_ln_g, sgu_ln_b, sgu_ws, sgu_bs)
    y_c = _rglru_mixer(p_rg, rg_conv_w, rg_conv_b, rg_wa, rg_ba, rg_wx, rg_bx, rg_lambda)
    y_d = _mlstm_mixer(p_ml, ml_i_bias, ml_f_bias, ml_norm_g)
    gates = jax.nn.sigmoid(p_gate.astype(F32)).reshape(B, L, N_BRANCH, D_MODEL)
    merged = gates[:, :, 0] * (y_a @ w_branch[0])
    merged = merged + gates[:, :, 1] * (y_b @ w_branch[1])
    merged = merged + gates[:, :, 2] * (y_c @ w_branch[2])
    merged = merged + gates[:, :, 3] * (y_d @ w_branch[3])
    return (merged @ w_out).astype(h.dtype)


def _swiglu(x, w1, w3, w2):
    return (jax.nn.silu(x @ w1) * (x @ w3)) @ w2


def _moe_swiglu(x, router_w, router_b, w1, w3, w2):
    logits = (x @ router_w).astype(F32) + router_b
    top_v, top_i = lax.top_k(logits, TOP_K)
    wts = jax.nn.softmax(top_v, axis=-1)
    combine = jnp.sum(jax.nn.one_hot(top_i, N_EXPERTS, dtype=F32) * wts[..., None], axis=-2)
    y = jnp.zeros(x.shape, F32)
    for e in range(N_EXPERTS):
        y = y + combine[..., e:e + 1] * _swiglu(x, w1[e], w3[e], w2[e])
    return y.astype(x.dtype)


def _trunk(x, norm_mix_g, w_in, hy_conv_w, hy_conv_b, hy_ffn_w1, hy_ffn_b1, hy_ffn_w2, hy_ffn_b2, hy_ffn_w3,
           hy_bias, sgu_ln_g, sgu_ln_b, sgu_ws, sgu_bs, rg_conv_w, rg_conv_b, rg_wa, rg_ba, rg_wx, rg_bx,
           rg_lambda, ml_i_bias, ml_f_bias, ml_norm_g, w_branch, w_out, norm_ffn_g, ffn_w1, ffn_w3, ffn_w2,
           router_w, router_b, moe_w1, moe_w3, moe_w2, norm_final_g):
    for l in range(DEPTH):
        h = _rmsnorm(x, norm_mix_g[l])
        x = x + _token_mixer(h, w_in[l], hy_conv_w[l], hy_conv_b[l], hy_ffn_w1[l], hy_ffn_b1[l], hy_ffn_w2[l],
                             hy_ffn_b2[l], hy_ffn_w3[l], hy_bias[l], sgu_ln_g[l], sgu_ln_b[l], sgu_ws[l], sgu_bs[l],
                             rg_conv_w[l], rg_conv_b[l], rg_wa[l], rg_ba[l], rg_wx[l], rg_bx[l], rg_lambda[l],
                             ml_i_bias[l], ml_f_bias[l], ml_norm_g[l], w_branch[l], w_out[l])
        h = _rmsnorm(x, norm_ffn_g[l])
        j = l // 2
        if l % 2 == 0:
            x = x + _swiglu(h, ffn_w1[j], ffn_w3[j], ffn_w2[j])
        else:
            x = x + _moe_swiglu(h, router_w[j], router_b[j], moe_w1[j], moe_w3[j], moe_w2[j])
    return _rmsnorm(x, norm_final_g)


def setup_inputs(seed: int = 0) -> dict:
    key = jax.random.key(seed)
    ks = iter(jax.random.split(key, 48))
    nrm = lambda shape, scale: scale * jax.random.normal(next(ks), shape, F32)
    gain = lambda shape: 1.0 + nrm(shape, 0.02)
    a0 = jax.random.uniform(next(ks), (DEPTH, 2, W_C), F32, minval=0.9, maxval=0.999)
    s0 = a0 ** (1.0 / RG_C)
    rg_lambda = jnp.log(s0) - jnp.log1p(-s0)
    ml_f_bias = jnp.linspace(3.0, 6.0, ML_HEADS, dtype=F32)[None, None, :] + nrm((DEPTH, 2, ML_HEADS), 0.02)
    return {
        'x_prompt': nrm((BATCH, SEQ, D_MODEL), 1.0),
        'x_sample': nrm((DEC_BATCH, DEC_SEQ, D_MODEL), 1.0),
        'norm_mix_g': gain((DEPTH, D_MODEL)),
        'w_in': nrm((DEPTH, D_MODEL, N_IN), D_MODEL ** -0.5),
        'hy_conv_w': nrm((DEPTH, HY_SHORT, N_IN_HY), HY_SHORT ** -0.5),
        'hy_conv_b': nrm((DEPTH, N_IN_HY), 0.02),
        'hy_ffn_w1': nrm((DEPTH, HY_EMB, HY_FFN), HY_EMB ** -0.5),
        'hy_ffn_b1': nrm((DEPTH, HY_FFN), 0.02),
        'hy_ffn_w2': nrm((DEPTH, HY_FFN, HY_FFN), HY_FFN ** -0.5),
        'hy_ffn_b2': nrm((DEPTH, HY_FFN), 0.02),
        'hy_ffn_w3': nrm((DEPTH, HY_FFN, 2 * HY_ORDER * W_A), 0.05 * HY_FFN ** -0.5),
        'hy_bias': nrm((DEPTH, HY_ORDER, W_A), 1.0),
        'sgu_ln_g': gain((DEPTH, W_B)),
        'sgu_ln_b': nrm((DEPTH, W_B), 0.02),
        'sgu_ws': nrm((DEPTH, SGU_GROUPS, SGU_CHUNK, SGU_CHUNK), SGU_CHUNK ** -0.5),
        'sgu_bs': gain((DEPTH, SGU_GROUPS, SGU_CHUNK)),
        'rg_conv_w': nrm((DEPTH, RG_CONV, W_C), RG_CONV ** -0.5),
        'rg_conv_b': nrm((DEPTH, W_C), 0.02),
        'rg_wa': nrm((DEPTH, 2, RG_HEADS, RG_HD, RG_HD), RG_HD ** -0.5),
        'rg_ba': nrm((DEPTH, 2, RG_HEADS, RG_HD), 0.02),
        'rg_wx': nrm((DEPTH, 2, RG_HEADS, RG_HD, RG_HD), RG_HD ** -0.5),
        'rg_bx': nrm((DEPTH, 2, RG_HEADS, RG_HD), 0.02),
        'rg_lambda': rg_lambda,
        'ml_i_bias': nrm((DEPTH, 2, ML_HEADS), 0.1),
        'ml_f_bias': ml_f_bias,
        'ml_norm_g': gain((DEPTH, W_D)),
        'w_branch': nrm((DEPTH, N_BRANCH, W_BR, D_MODEL), W_BR ** -0.5),
        'w_out': nrm((DEPTH, D_MODEL, D_MODEL), D_MODEL ** -0.5),
        'norm_ffn_g': gain((DEPTH, D_MODEL)),
        'ffn_w1': nrm((N_DENSE, D_MODEL, D_FF), D_MODEL ** -0.5),
        'ffn_w3': nrm((N_DENSE, D_MODEL, D_FF), D_MODEL ** -0.5),
        'ffn_w2': nrm((N_DENSE, D_FF, D_MODEL), D_FF ** -0.5),
        'router_w': nrm((N_MOE, D_MODEL, N_EXPERTS), D_MODEL ** -0.5),
        'router_b': nrm((N_MOE, N_EXPERTS), 0.01),
        'moe_w1': nrm((N_MOE, N_EXPERTS, D_MODEL, D_FF_EXPERT), D_MODEL ** -0.5),
        'moe_w3': nrm((N_MOE, N_EXPERTS, D_MODEL, D_FF_EXPERT), D_MODEL ** -0.5),
        'moe_w2': nrm((N_MOE, N_EXPERTS, D_FF_EXPERT, D_MODEL), D_FF_EXPERT ** -0.5),
        'norm_final_g': gain((D_MODEL,)),
    }


def reference(x_prompt, x_sample, norm_mix_g, w_in, hy_conv_w, hy_conv_b, hy_ffn_w1, hy_ffn_b1, hy_ffn_w2,
              hy_ffn_b2, hy_ffn_w3, hy_bias, sgu_ln_g, sgu_ln_b, sgu_ws, sgu_bs, rg_conv_w, rg_conv_b, rg_wa,
              rg_ba, rg_wx, rg_bx, rg_lambda, ml_i_bias, ml_f_bias, ml_norm_g, w_branch, w_out, norm_ffn_g,
              ffn_w1, ffn_w3, ffn_w2, router_w, router_b, moe_w1, moe_w3, moe_w2, norm_final_g):
    weights = (norm_mix_g, w_in, hy_conv_w, hy_conv_b, hy_ffn_w1, hy_ffn_b1, hy_ffn_w2, hy_ffn_b2, hy_ffn_w3,
               hy_bias, sgu_ln_g, sgu_ln_b, sgu_ws, sgu_bs, rg_conv_w, rg_conv_b, rg_wa, rg_ba, rg_wx, rg_bx,
               rg_lambda, ml_i_bias, ml_f_bias, ml_norm_g, w_branch, w_out, norm_ffn_g, ffn_w1, ffn_w3, ffn_w2,
               router_w, router_b, moe_w1, moe_w3, moe_w2, norm_final_g)
    y_prompt = _trunk(x_prompt, *weights)
    y_sample = _trunk(x_sample, *weights)
    return (y_prompt, y_sample)
```

```python
import functools
import math

import jax
import jax.numpy as jnp
import numpy as np
from jax import lax
from jax.experimental import pallas as pl
from jax.experimental.pallas import tpu as pltpu

F32 = jnp.float32
BF16 = jnp.bfloat16
EPS = 1e-6
HIGHEST = lax.Precision.HIGHEST

LANES = 128
SUBLANES = 8
VMEM_BYTES_V7X = 64 * 1024 * 1024

D_MODEL = 1024
W_BR = 512
N_BRANCH = 4
HY_BANDS = 16
HY_EMB = 1 + 2 * HY_BANDS
HY_FFN = 64
HY_TARGET = 1e-2
HY_MIN_DECAY = -math.log(HY_TARGET) / 1.5
HY_MAX_DECAY = -math.log(HY_TARGET) / 0.3
SGU_CHUNK = 128
SGU_GROUPS = 4
RG_HEADS = 8
RG_HD = W_BR // RG_HEADS
RG_C = 8.0
ML_HEADS = 4
ML_HD = W_BR // ML_HEADS
ML_CHUNK = 128
N_EXPERTS = 8

N_PROJ = N_BRANCH * D_MODEL + 3 * W_BR + 2 * W_BR + 2 * W_BR + 4 * W_BR
COL_GATE = 0
COL_HY = N_BRANCH * D_MODEL
COL_SGU = COL_HY + 3 * W_BR
COL_RG = COL_SGU + 2 * W_BR
COL_ML = COL_RG + 2 * W_BR


def _params(semantics, vmem_mb):
    return pltpu.CompilerParams(dimension_semantics=semantics,
                                vmem_limit_bytes=vmem_mb * 1024 * 1024)


def _sigmoid(x):
    return 1.0 / (1.0 + jnp.exp(-x))


def _rms(x, g):
    return x * lax.rsqrt(jnp.mean(x * x, axis=-1, keepdims=True) + EPS) * g


def _inproj_kernel(x_ref, g_ref, w_ref, wg_ref, proj_ref, mlg_ref, h_ref):
    @pl.when(pl.program_id(1) == 0)
    def _():
        h = _rms(x_ref[...], g_ref[...]).astype(BF16)
        h_ref[...] = h
        mlg_ref[...] = jnp.dot(h, wg_ref[...], preferred_element_type=F32)

    proj_ref[...] = jnp.dot(h_ref[...], w_ref[...], preferred_element_type=F32).astype(BF16)


def _inproj(x, g, w, wg, tm=1024, n_split=4):
    T = x.shape[0]
    tn = N_PROJ // n_split
    return pl.pallas_call(
        _inproj_kernel,
        out_shape=(jax.ShapeDtypeStruct((T, N_PROJ), BF16),
                   jax.ShapeDtypeStruct((T, LANES), F32)),
        grid=(T // tm, n_split),
        in_specs=[pl.BlockSpec((tm, D_MODEL), lambda i, j: (i, 0)),
                  pl.BlockSpec((1, D_MODEL), lambda i, j: (0, 0)),
                  pl.BlockSpec((D_MODEL, tn), lambda i, j: (0, j)),
                  pl.BlockSpec((D_MODEL, LANES), lambda i, j: (0, 0))],
        out_specs=(pl.BlockSpec((tm, tn), lambda i, j: (i, j)),
                   pl.BlockSpec((tm, LANES), lambda i, j: (i, 0))),
        scratch_shapes=[pltpu.VMEM((tm, D_MODEL), BF16)],
        compiler_params=_params(("parallel", "arbitrary"), 48),
        name="inproj",
    )(x, g, w, wg)


def _hy_filter_kernel(z_ref, w1_ref, b1_ref, w2_ref, b2_ref, w3_ref, delta_ref, bias_ref, kk_ref):
    half = pl.program_id(2)
    z = z_ref[...]
    h = jnp.sin(jnp.dot(w1_ref[...], z, precision=HIGHEST, preferred_element_type=F32) + b1_ref[...])
    h = jnp.sin(jnp.dot(w2_ref[...], h, precision=HIGHEST, preferred_element_type=F32) + b2_ref[...])
    f = jnp.dot(w3_ref[...], h, precision=HIGHEST, preferred_element_type=F32)
    t_norm = z[0:1, :]
    f = f * jnp.exp(-t_norm * delta_ref[...])
    lane = lax.broadcasted_iota(jnp.int32, f.shape, 1)
    first = lane == 0
    f = jnp.where(first, jnp.where(half == 0, 0.0, f + bias_ref[...]), f)
    kk_ref[...] = f


def _hy_filters(hy_ffn_w1, hy_ffn_b1, hy_ffn_w2, hy_ffn_b2, hy_ffn_w3, hy_bias, L):
    depth = hy_ffn_w1.shape[0]
    lag = np.stack([L - np.arange(L), np.arange(L)]).astype(np.float64)
    bands = np.arange(1, HY_BANDS + 1, dtype=np.float64)
    ang = (2.0 * math.pi / L) * lag[:, None, :] * bands[None, :, None]
    z = np.concatenate([lag[:, None, :] / L, np.cos(ang), np.sin(ang)], axis=1)
    z = np.pad(z, ((0, 0), (0, LANES - HY_EMB), (0, 0))).astype(np.float32)
    z = jnp.asarray(z)
    w1t = jnp.pad(jnp.swapaxes(hy_ffn_w1, 1, 2), ((0, 0), (0, 0), (0, LANES - HY_EMB)))
    w2t = jnp.swapaxes(hy_ffn_w2, 1, 2)
    w3t = jnp.swapaxes(hy_ffn_w3, 1, 2).reshape(depth, 4, W_BR, HY_FFN)
    b1 = hy_ffn_b1[:, :, None]
    b2 = hy_ffn_b2[:, :, None]
    delta = jnp.linspace(HY_MIN_DECAY, HY_MAX_DECAY, W_BR, dtype=F32)[:, None]
    bias = hy_bias[:, :, :, None]
    return pl.pallas_call(
        _hy_filter_kernel,
        out_shape=jax.ShapeDtypeStruct((depth, 2, W_BR, 2 * L), F32),
        grid=(depth, 2, 2),
        in_specs=[pl.BlockSpec((None, LANES, L), lambda l, o, s: (s, 0, 0)),
                  pl.BlockSpec((None, HY_FFN, LANES), lambda l, o, s: (l, 0, 0)),
                  pl.BlockSpec((None, HY_FFN, 1), lambda l, o, s: (l, 0, 0)),
                  pl.BlockSpec((None, HY_FFN, HY_FFN), lambda l, o, s: (l, 0, 0)),
                  pl.BlockSpec((None, HY_FFN, 1), lambda l, o, s: (l, 0, 0)),
                  pl.BlockSpec((None, None, W_BR, HY_FFN), lambda l, o, s: (l, 2 * o + 1 - s, 0, 0)),
                  pl.BlockSpec((W_BR, 1), lambda l, o, s: (0, 0)),
                  pl.BlockSpec((None, None, W_BR, 1), lambda l, o, s: (l, o, 0, 0))],
        out_specs=pl.BlockSpec((None, None, W_BR, L), lambda l, o, s: (l, o, 0, s)),
        compiler_params=_params(("parallel", "parallel", "parallel"), 48),
        name="hy_filters",
    )(z, w1t, b1, w2t, b2, w3t, delta, bias)


HY_PRE_NB = 8


def _hy_pre_kernel(p_ref, w_ref, b_ref, o_ref):
    L = p_ref.shape[1]
    w = w_ref[...]
    row = lax.broadcasted_iota(jnp.int32, (L, LANES), 0)

    def body(bi, carry):
        x = p_ref[bi].astype(F32)
        xm = jnp.where(row == 0, 0.0, pltpu.roll(x, 1, 0))
        xp = jnp.where(row == L - 1, 0.0, pltpu.roll(x, L - 1, 0))
        u = b_ref[...] + w[0:1] * xm + w[1:2] * x + w[2:3] * xp
        o_ref[:, pl.ds(pl.multiple_of(bi * L, LANES), L)] = u.T.astype(BF16)
        return carry

    lax.fori_loop(0, HY_PRE_NB, body, 0)


def _hy_pre(proj3, conv_w, conv_b):
    B, L, _ = proj3.shape
    nc = 3 * W_BR // LANES
    return pl.pallas_call(
        _hy_pre_kernel,
        out_shape=jax.ShapeDtypeStruct((3 * W_BR, B * L), BF16),
        grid=(B // HY_PRE_NB, nc),
        in_specs=[pl.BlockSpec((HY_PRE_NB, L, LANES), lambda b, c: (b, 0, COL_HY // LANES + c)),
                  pl.BlockSpec((3, LANES), lambda b, c: (0, c)),
                  pl.BlockSpec((1, LANES), lambda b, c: (0, c))],
        out_specs=pl.BlockSpec((LANES, HY_PRE_NB * L), lambda b, c: (c, b)),
        compiler_params=_params(("parallel", "parallel"), 48),
        name="hy_pre",
    )(proj3, conv_w, conv_b)


HY_CB = 8


def _hy_build_table(kk_ref, o, ci, g_ref, L):
    nblk = (2 * L - LANES) // LANES
    for m in range(nblk):
        lo = 2 * L - 2 * LANES - LANES * m
        seg = kk_ref[o, pl.ds(ci, 1), lo:lo + 2 * LANES]
        x = jnp.broadcast_to(seg, (LANES, 2 * LANES))
        r = pltpu.roll(x, 0, 1, stride=1, stride_axis=0)
        g_ref[LANES * m:LANES * (m + 1), :] = r[:, LANES:].astype(BF16)


def _hy_toeplitz_matmul(u, g_ref, L):
    acc = None
    kt = 2 * LANES
    for sb in range(L // kt):
        rhs = jnp.concatenate(
            [g_ref[kt * sb - LANES * tb + L - LANES:kt * sb - LANES * tb + L - LANES + kt, :]
             for tb in range(L // LANES)], axis=1)
        part = jnp.dot(u[:, kt * sb:kt * (sb + 1)], rhs, preferred_element_type=F32)
        acc = part if acc is None else acc + part
    return acc


def _hy_conv_kernel(z_ref, x1_ref, x2_ref, kk_ref, o_ref, g0_ref, g1_ref):
    L = z_ref.shape[2]

    def body(ci, carry):
        _hy_build_table(kk_ref, 0, ci, g0_ref, L)
        _hy_build_table(kk_ref, 1, ci, g1_ref, L)
        y0 = _hy_toeplitz_matmul(z_ref[ci], g0_ref, L)
        z1 = x1_ref[ci].astype(F32) * y0
        y1 = _hy_toeplitz_matmul(z1.astype(BF16), g1_ref, L)
        o_ref[ci] = (x2_ref[ci].astype(F32) * y1).astype(BF16)
        return carry

    lax.fori_loop(0, HY_CB, body, 0)


def _hy_conv(zc3, kk):
    _, B, L = zc3.shape
    nblk = W_BR // HY_CB
    act = lambda off: pl.BlockSpec((HY_CB, B, L), lambda c: (c + off * nblk, 0, 0))
    return pl.pallas_call(
        _hy_conv_kernel,
        out_shape=jax.ShapeDtypeStruct((W_BR, B, L), BF16),
        grid=(nblk,),
        in_specs=[act(0), act(1), act(2),
                  pl.BlockSpec((2, HY_CB, 2 * L), lambda c: (0, c, 0))],
        out_specs=pl.BlockSpec((HY_CB, B, L), lambda c: (c, 0, 0)),
        scratch_shapes=[pltpu.VMEM((2 * L - LANES, LANES), BF16),
                        pltpu.VMEM((2 * L - LANES, LANES), BF16)],
        compiler_params=_params(("parallel",), 48),
        name="hy_conv",
    )(zc3, zc3, zc3, kk)


HY_POST_TT = 8192


def _hy_post_kernel(y_ref, o_ref):
    o_ref[...] = y_ref[...].astype(F32).T.astype(BF16)


def _hy_post(yc):
    C, T = yc.shape
    return pl.pallas_call(
        _hy_post_kernel,
        out_shape=jax.ShapeDtypeStruct((T, C), BF16),
        grid=(T // HY_POST_TT, C // LANES),
        in_specs=[pl.BlockSpec((LANES, HY_POST_TT), lambda t, c: (c, t))],
        out_specs=pl.BlockSpec((HY_POST_TT, LANES), lambda t, c: (t, c)),
        compiler_params=_params(("parallel", "parallel"), 48),
        name="hy_post",
    )(yc)


def _sgu_kernel(u_ref, v_ref, g_ref, b_ref, ws_ref, bs_ref, o_ref):
    L = u_ref.shape[0]
    gw = W_BR // SGU_GROUPS

    def body(n, carry):
        rows = pl.ds(pl.multiple_of(n * SGU_CHUNK, SGU_CHUNK), SGU_CHUNK)
        v = v_ref[rows, :].astype(F32)
        mu = jnp.mean(v, axis=-1, keepdims=True)
        d = v - mu
        var = jnp.mean(d * d, axis=-1, keepdims=True)
        vn = (d * lax.rsqrt(var + EPS) * g_ref[...] + b_ref[...]).astype(BF16)
        mixed = jnp.concatenate(
            [jnp.dot(ws_ref[k], vn[:, gw * k:gw * (k + 1)], preferred_element_type=F32) + bs_ref[k]
             for k in range(SGU_GROUPS)], axis=1)
        o_ref[rows, :] = (u_ref[rows, :].astype(F32) * mixed).astype(BF16)
        return carry

    lax.fori_loop(0, L // SGU_CHUNK, body, 0)


def _sgu(proj3, ln_g, ln_b, ws, bs):
    B, L, _ = proj3.shape
    cb = COL_SGU // W_BR
    return pl.pallas_call(
        _sgu_kernel,
        out_shape=jax.ShapeDtypeStruct((B, L, W_BR), BF16),
        grid=(B,),
        in_specs=[pl.BlockSpec((None, L, W_BR), lambda b: (b, 0, cb)),
                  pl.BlockSpec((None, L, W_BR), lambda b: (b, 0, cb + 1)),
                  pl.BlockSpec((1, W_BR), lambda b: (0, 0)),
                  pl.BlockSpec((1, W_BR), lambda b: (0, 0)),
                  pl.BlockSpec((SGU_GROUPS, SGU_CHUNK, SGU_CHUNK), lambda b: (0, 0, 0)),
                  pl.BlockSpec((SGU_GROUPS, SGU_CHUNK, 1), lambda b: (0, 0, 0))],
        out_specs=pl.BlockSpec((None, L, W_BR), lambda b: (b, 0, 0)),
        compiler_params=_params(("parallel",), 48),
        name="sgu",
    )(proj3, proj3, ln_g, ln_b, ws, bs)


RG_TL = 256
RG_PAD = 8


def _scan_blocks(a, b, reverse):
    n = a.shape[0]
    pos = lax.broadcasted_iota(jnp.int32, a.shape, 0) & (SUBLANES - 1)
    for k in (1, 2, 4):
        if reverse:
            valid = pos < SUBLANES - k
            shift = n - k
        else:
            valid = pos >= k
            shift = k
        a_s = jnp.where(valid, pltpu.roll(a, shift, 0), 1.0)
        b_s = jnp.where(valid, pltpu.roll(b, shift, 0), 0.0)
        b = a * b_s + b
        a = a * a_s
    return a, b


def _rg_kernel(xb_ref, gb_ref, cw_ref, cb_ref, w_ref, bias_ref, lam_ref, o_ref,
               xpad_ref, af_ref, bf_ref, ab_ref, bb_ref):
    L = xb_ref.shape[0]
    zeros = jnp.zeros((RG_PAD, LANES), F32)
    xpad_ref[0:RG_PAD, :] = zeros
    xpad_ref[RG_PAD + L:RG_PAD + L + RG_PAD, :] = zeros
    xpad_ref[RG_PAD:RG_PAD + L, :] = xb_ref[...].astype(F32)
    cw = cw_ref[...]
    lam = lam_ref[...]
    sp = jnp.maximum(-lam, 0.0) + jnp.log(1.0 + jnp.exp(-jnp.abs(lam)))

    for ti in range(L // RG_TL):
        t0 = ti * RG_TL
        xc = cb_ref[...]
        for j in range(4):
            xc = xc + cw[j:j + 1] * xpad_ref[RG_PAD + t0 + j - 2:RG_PAD + t0 + j - 2 + RG_TL, :]
        gates = jnp.dot(xc.astype(BF16), w_ref[...], preferred_element_type=F32) + bias_ref[...]
        gates = _sigmoid(gates)
        for d, (a_ref, b_ref) in enumerate(((af_ref, bf_ref), (ab_ref, bb_ref))):
            r = gates[:, 2 * d * LANES:(2 * d + 1) * LANES]
            i = gates[:, (2 * d + 1) * LANES:(2 * d + 2) * LANES]
            a = jnp.exp((-RG_C) * r * sp[d:d + 1])
            bt = jnp.sqrt(1.0 - a * a) * (i * xc)
            a_c, b_c = _scan_blocks(a, bt, reverse=(d == 1))
            a_ref[t0:t0 + RG_TL, :] = a_c
            b_ref[t0:t0 + RG_TL, :] = b_c

    nblk = L // SUBLANES

    def body(k, carry):
        hf, hb = carry
        rf = pl.ds(pl.multiple_of(k * SUBLANES, SUBLANES), SUBLANES)
        rb = pl.ds(pl.multiple_of((nblk - 1 - k) * SUBLANES, SUBLANES), SUBLANES)
        f = af_ref[rf, :] * hf + bf_ref[rf, :]
        bf_ref[rf, :] = f
        g = ab_ref[rb, :] * hb + bb_ref[rb, :]
        bb_ref[rb, :] = g
        return (jnp.broadcast_to(f[SUBLANES - 1:SUBLANES, :], (SUBLANES, LANES)),
                jnp.broadcast_to(g[0:1, :], (SUBLANES, LANES)))

    h0 = jnp.zeros((SUBLANES, LANES), F32)
    lax.fori_loop(0, nblk, body, (h0, h0))
    gb = gb_ref[...].astype(F32)
    gelu = 0.5 * gb * (1.0 + jnp.tanh(math.sqrt(2.0 / math.pi) * (gb + 0.044715 * (gb * gb * gb))))
    o_ref[...] = ((bf_ref[...] + bb_ref[...]) * gelu).astype(BF16)


def _rglru(proj3, conv_w, conv_b, w_bd, bias_bd, lam):
    B, L, _ = proj3.shape
    ng = W_BR // LANES
    cx = COL_RG // LANES
    return pl.pallas_call(
        _rg_kernel,
        out_shape=jax.ShapeDtypeStruct((B, L, W_BR), BF16),
        grid=(B, ng),
        in_specs=[pl.BlockSpec((None, L, LANES), lambda b, c: (b, 0, cx + c)),
                  pl.BlockSpec((None, L, LANES), lambda b, c: (b, 0, cx + ng + c)),
                  pl.BlockSpec((4, LANES), lambda b, c: (0, c)),
                  pl.BlockSpec((1, LANES), lambda b, c: (0, c)),
                  pl.BlockSpec((None, LANES, 4 * LANES), lambda b, c: (c, 0, 0)),
                  pl.BlockSpec((None, 1, 4 * LANES), lambda b, c: (c, 0, 0)),
                  pl.BlockSpec((2, LANES), lambda b, c: (0, c))],
        out_specs=pl.BlockSpec((None, L, LANES), lambda b, c: (b, 0, c)),
        scratch_shapes=[pltpu.VMEM((L + 2 * RG_PAD, LANES), F32)] + [pltpu.VMEM((L, LANES), F32)] * 4,
        compiler_params=_params(("parallel", "parallel"), 48),
        name="rglru",
    )(proj3, proj3, conv_w, conv_b, w_bd, bias_bd, lam)


def _ml_kernel(q_ref, k_ref, v_ref, o_ref, gt_ref, gbias_ref, ng_ref, y_ref, h_ref, ct_ref, n_ref, m_ref):
    L = q_ref.shape[0]
    nc = L // ML_CHUNK
    row = lax.broadcasted_iota(jnp.int32, (ML_CHUNK, ML_CHUNK), 0)
    col = lax.broadcasted_iota(jnp.int32, (ML_CHUNK, ML_CHUNK), 1)
    lane = lax.broadcasted_iota(jnp.int32, (ML_CHUNK, LANES), 1)
    is_f = (lane >= 2 * ML_HEADS) & (lane < 4 * ML_HEADS)
    scale = ML_HD ** -0.5

    for d in range(2):
        mask = (col <= row) if d == 0 else (col >= row)
        tri = mask.astype(F32)
        ct_ref[...] = jnp.zeros(ct_ref.shape, F32)
        n_ref[...] = jnp.zeros(n_ref.shape, F32)
        m_ref[...] = jnp.zeros(m_ref.shape, F32)

        def body(step, carry, d=d, mask=mask, tri=tri):
            c = step if d == 0 else nc - 1 - step
            rows = pl.ds(pl.multiple_of(c * ML_CHUNK, ML_CHUNK), ML_CHUNK)
            pre = gt_ref[rows, :] + gbias_ref[...]
            logf = jnp.minimum(pre, 0.0) - jnp.log(1.0 + jnp.exp(-jnp.abs(pre)))
            gl = jnp.where(is_f, logf, pre)
            cum = jnp.dot(tri, jnp.where(is_f, logf, 0.0), precision=HIGHEST, preferred_element_type=F32)
            cum_t = cum.T
            gl_t = gl.T
            for hd in range(ML_HEADS):
                li_lane = d * ML_HEADS + hd
                lf_lane = 2 * ML_HEADS + d * ML_HEADS + hd
                b_col = cum[:, lf_lane:lf_lane + 1]
                b_row = cum_t[lf_lane:lf_lane + 1, :]
                li_col = gl[:, li_lane:li_lane + 1]
                li_row = gl_t[li_lane:li_lane + 1, :]
                last = ML_CHUNK - 1 if d == 0 else 0
                g_tot = b_col[last:last + 1, :]
                m_prev = m_ref[hd]
                dlog = jnp.where(mask, b_col - b_row + li_row, -jnp.inf)
                inter = b_col + m_prev
                m_t = jnp.maximum(inter, jnp.max(dlog, axis=1, keepdims=True))
                w_intra = jnp.exp(dlog - m_t)
                w_inter = jnp.exp(inter - m_t)
                hs = slice(hd * ML_HD, (hd + 1) * ML_HD)
                qh = q_ref[rows, hs]
                kh = k_ref[rows, hs]
                vh = v_ref[rows, hs]
                kt = kh.astype(F32).T.astype(BF16)
                s = jnp.dot(qh, kt, preferred_element_type=F32) * (scale * w_intra)
                ct = ct_ref[hd]
                q_c = jnp.dot(qh, ct.astype(BF16), preferred_element_type=F32) * scale
                num = w_inter * q_c + jnp.dot(s.astype(BF16), vh, preferred_element_type=F32)
                q_n = jnp.sum(qh.astype(F32) * n_ref[hd], axis=1, keepdims=True) * scale
                den = w_inter * q_n + jnp.sum(s, axis=1, keepdims=True)
                hout = num / jnp.maximum(jnp.abs(den), jnp.exp(-m_t))
                if d == 0:
                    h_ref[rows, hs] = hout
                else:
                    h_ref[rows, hs] = h_ref[rows, hs] + hout
                wlog_col = g_tot - b_col + li_col
                m_new = jnp.maximum(g_tot + m_prev, jnp.max(wlog_col, axis=0, keepdims=True))
                decay = jnp.exp(g_tot + m_prev - m_new)
                ws_col = jnp.exp(wlog_col - m_new)
                vw = (vh.astype(F32) * ws_col).astype(BF16)
                ct_ref[hd] = decay * ct + jnp.dot(kt, vw, preferred_element_type=F32)
                n_ref[hd] = decay * n_ref[hd] + jnp.sum(kh.astype(F32) * ws_col, axis=0, keepdims=True)
                m_ref[hd] = m_new
            return carry

        lax.fori_loop(0, nc, body, 0)

    for hd in range(ML_HEADS):
        hs = slice(hd * ML_HD, (hd + 1) * ML_HD)
        h = h_ref[:, hs]
        hn = h * lax.rsqrt(jnp.mean(h * h, axis=-1, keepdims=True) + EPS) * ng_ref[:, hs]
        y_ref[:, hs] = (_sigmoid(o_ref[:, hs].astype(F32)) * hn).astype(BF16)


def _mlstm(proj3, mlg3, gbias, norm_g):
    B, L, _ = proj3.shape
    cb = COL_ML // W_BR
    col = lambda j: pl.BlockSpec((None, L, W_BR), lambda b: (b, 0, cb + j))
    return pl.pallas_call(
        _ml_kernel,
        out_shape=jax.ShapeDtypeStruct((B, L, W_BR), BF16),
        grid=(B,),
        in_specs=[col(0), col(1), col(2), col(3),
                  pl.BlockSpec((None, L, LANES), lambda b: (b, 0, 0)),
                  pl.BlockSpec((1, LANES), lambda b: (0, 0)),
                  pl.BlockSpec((1, W_BR), lambda b: (0, 0))],
        out_specs=pl.BlockSpec((None, L, W_BR), lambda b: (b, 0, 0)),
        scratch_shapes=[pltpu.VMEM((L, W_BR), F32),
                        pltpu.VMEM((ML_HEADS, ML_HD, ML_HD), F32),
                        pltpu.VMEM((ML_HEADS, 1, ML_HD), F32),
                        pltpu.VMEM((ML_HEADS, 1, 1), F32)],
        compiler_params=_params(("parallel",), 48),
        name="mlstm",
    )(proj3, proj3, proj3, proj3, mlg3, gbias, norm_g)


def _merge_kernel(route, gate_ref, ya_ref, yb_ref, yc_ref, yd_ref, wb_ref, wo_ref, x_ref, g_ref, *rest):
    if route:
        rw_ref, rb_ref, xo_ref, hn_ref, comb_ref = rest
    else:
        xo_ref, hn_ref = rest
    merged = None
    for k, y_ref in enumerate((ya_ref, yb_ref, yc_ref, yd_ref)):
        t = jnp.dot(y_ref[...], wb_ref[k], preferred_element_type=F32)
        gk = _sigmoid(gate_ref[:, k * D_MODEL:(k + 1) * D_MODEL].astype(F32))
        merged = gk * t if merged is None else merged + gk * t
    xn = x_ref[...] + jnp.dot(merged.astype(BF16), wo_ref[...], preferred_element_type=F32)
    xo_ref[...] = xn
    h = _rms(xn, g_ref[...])
    hn_ref[...] = h.astype(BF16)
    if route:
        logits = jnp.dot(h, rw_ref[...], precision=HIGHEST, preferred_element_type=F32) + rb_ref[...]
        lane = lax.broadcasted_iota(jnp.int32, logits.shape, 1)
        logits = jnp.where(lane < N_EXPERTS, logits, -jnp.inf)
        v1 = jnp.max(logits, axis=1, keepdims=True)
        i1 = jnp.min(jnp.where(logits == v1, lane, LANES), axis=1, keepdims=True)
        rest_l = jnp.where(lane == i1, -jnp.inf, logits)
        v2 = jnp.max(rest_l, axis=1, keepdims=True)
        i2 = jnp.min(jnp.where(rest_l == v2, lane, LANES), axis=1, keepdims=True)
        e2 = jnp.exp(v2 - v1)
        p1 = 1.0 / (1.0 + e2)
        comb_ref[...] = jnp.where(lane == i1, p1, jnp.where(lane == i2, e2 * p1, 0.0))


def _merge(proj, ya, yb, yc, yd, wb, wo, x, g, router=None, tm=512):
    T = x.shape[0]
    row = lambda w: pl.BlockSpec((tm, w), lambda i: (i, 0))
    in_specs = [row(N_BRANCH * D_MODEL), row(W_BR), row(W_BR), row(W_BR), row(W_BR),
                pl.BlockSpec((N_BRANCH, W_BR, D_MODEL), lambda i: (0, 0, 0)),
                pl.BlockSpec((D_MODEL, D_MODEL), lambda i: (0, 0)),
                row(D_MODEL),
                pl.BlockSpec((1, D_MODEL), lambda i: (0, 0))]
    out_shape = [jax.ShapeDtypeStruct((T, D_MODEL), F32), jax.ShapeDtypeStruct((T, D_MODEL), BF16)]
    out_specs = [row(D_MODEL), row(D_MODEL)]
    args = [proj, ya, yb, yc, yd, wb, wo, x, g]
    if router is not None:
        in_specs += [pl.BlockSpec((D_MODEL, LANES), lambda i: (0, 0)),
                     pl.BlockSpec((1, LANES), lambda i: (0, 0))]
        out_shape.append(jax.ShapeDtypeStruct((T, LANES), F32))
        out_specs.append(row(LANES))
        args += list(router)
    return pl.pallas_call(
        functools.partial(_merge_kernel, router is not None),
        out_shape=tuple(out_shape),
        grid=(T // tm,),
        in_specs=in_specs,
        out_specs=tuple(out_specs),
        compiler_params=_params(("parallel",), 56),
        name="merge_route" if router is not None else "merge",
    )(*args)


def _swiglu_acc(h, w1_ref, w3_ref, w2_ref):
    a = jnp.dot(h, w1_ref[...], preferred_element_type=F32)
    b = jnp.dot(h, w3_ref[...], preferred_element_type=F32)
    act = (a * _sigmoid(a) * b).astype(BF16)
    return jnp.dot(act, w2_ref[...], preferred_element_type=F32)


def _ffn_kernel(h_ref, x_ref, w1_ref, w3_ref, w2_ref, o_ref, acc_ref):
    f = pl.program_id(1)

    @pl.when(f == 0)
    def _():
        acc_ref[...] = x_ref[...]

    acc_ref[...] += _swiglu_acc(h_ref[...], w1_ref, w3_ref, w2_ref)

    @pl.when(f == pl.num_programs(1) - 1)
    def _():
        o_ref[...] = acc_ref[...]


def _ffn(hn, x, w1, w3, w2, tm=512, n_split=2):
    T = x.shape[0]
    dff = w1.shape[1]
    tf = dff // n_split
    return pl.pallas_call(
        _ffn_kernel,
        out_shape=jax.ShapeDtypeStruct((T, D_MODEL), F32),
        grid=(T // tm, n_split),
        in_specs=[pl.BlockSpec((tm, D_MODEL), lambda i, f: (i, 0)),
                  pl.BlockSpec((tm, D_MODEL), lambda i, f: (i, 0)),
                  pl.BlockSpec((D_MODEL, tf), lambda i, f: (0, f)),
                  pl.BlockSpec((D_MODEL, tf), lambda i, f: (0, f)),
                  pl.BlockSpec((tf, D_MODEL), lambda i, f: (f, 0))],
        out_specs=pl.BlockSpec((tm, D_MODEL), lambda i, f: (i, 0)),
        scratch_shapes=[pltpu.VMEM((tm, D_MODEL), F32)],
        compiler_params=_params(("parallel", "arbitrary"), 56),
        name="ffn",
    )(hn, x, w1, w3, w2)


def _moe_kernel(h_ref, x_ref, comb_ref, w1_ref, w3_ref, w2_ref, o_ref, acc_ref):
    e = pl.program_id(1)

    @pl.when(e == 0)
    def _():
        acc_ref[...] = x_ref[...]

    comb = comb_ref[...]
    lane = lax.broadcasted_iota(jnp.int32, comb.shape, 1)
    c = jnp.sum(jnp.where(lane == e, comb, 0.0), axis=1, keepdims=True)
    acc_ref[...] += c * _swiglu_acc(h_ref[...], w1_ref, w3_ref, w2_ref)

    @pl.when(e == pl.num_programs(1) - 1)
    def _():
        o_ref[...] = acc_ref[...]


def _moe(hn, x, comb, w1, w3, w2, tm=512):
    T = x.shape[0]
    dfe = w1.shape[2]
    return pl.pallas_call(
        _moe_kernel,
        out_shape=jax.ShapeDtypeStruct((T, D_MODEL), F32),
        grid=(T // tm, N_EXPERTS),
        in_specs=[pl.BlockSpec((tm, D_MODEL), lambda i, e: (i, 0)),
                  pl.BlockSpec((tm, D_MODEL), lambda i, e: (i, 0)),
                  pl.BlockSpec((tm, LANES), lambda i, e: (i, 0)),
                  pl.BlockSpec((None, D_MODEL, dfe), lambda i, e: (e, 0, 0)),
                  pl.BlockSpec((None, D_MODEL, dfe), lambda i, e: (e, 0, 0)),
                  pl.BlockSpec((None, dfe, D_MODEL), lambda i, e: (e, 0, 0))],
        out_specs=pl.BlockSpec((tm, D_MODEL), lambda i, e: (i, 0)),
        scratch_shapes=[pltpu.VMEM((tm, D_MODEL), F32)],
        compiler_params=_params(("parallel", "arbitrary"), 56),
        name="moe",
    )(hn, x, comb, w1, w3, w2)


def _final_norm_kernel(x_ref, g_ref, o_ref):
    o_ref[...] = _rms(x_ref[...], g_ref[...])


def _final_norm(x, g, tm=1024):
    T = x.shape[0]
    return pl.pallas_call(
        _final_norm_kernel,
        out_shape=jax.ShapeDtypeStruct((T, D_MODEL), F32),
        grid=(T // tm,),
        in_specs=[pl.BlockSpec((tm, D_MODEL), lambda i: (i, 0)),
                  pl.BlockSpec((1, D_MODEL), lambda i: (0, 0))],
        out_specs=pl.BlockSpec((tm, D_MODEL), lambda i: (i, 0)),
        compiler_params=_params(("parallel",), 48),
        name="final_norm",
    )(x, g)


def _split_w_in(w_in_l):
    n_hy, n_sgu, n_rg = 3 * W_BR, 2 * W_BR, 2 * W_BR
    o1 = n_hy
    o2 = o1 + n_sgu
    o3 = o2 + n_rg
    o4 = o3 + 4 * W_BR
    o5 = o4 + 4 * ML_HEADS
    w = jnp.concatenate([w_in_l[:, o5:], w_in_l[:, :o4]], axis=1).astype(BF16)
    wg = jnp.pad(w_in_l[:, o4:o5], ((0, 0), (0, LANES - 4 * ML_HEADS))).astype(BF16)
    return w, wg


def _rg_block_diag(wa, ba, wx, bx):
    hpg = LANES // RG_HD
    ng = RG_HEADS // hpg
    eye = jnp.eye(hpg, dtype=F32)

    def bd(w):
        w = w.reshape(ng, hpg, RG_HD, RG_HD)
        return jnp.einsum('gaij,ab->gaibj', w, eye).reshape(ng, LANES, LANES)

    w = jnp.concatenate([bd(wa[0]), bd(wx[0]), bd(wa[1]), bd(wx[1])], axis=2).astype(BF16)
    fl = lambda b: b.reshape(ng, 1, LANES)
    bias = jnp.concatenate([fl(ba[0]), fl(bx[0]), fl(ba[1]), fl(bx[1])], axis=2)
    return w, bias


def _token_mixer(x, B, L, l, norm_g, w_in, kk, hy_conv_w, hy_conv_b, sgu_ln_g, sgu_ln_b, sgu_ws, sgu_bs,
                 rg_conv_w, rg_conv_b, rg_wa, rg_ba, rg_wx, rg_bx, rg_lambda, ml_i_bias, ml_f_bias,
                 ml_norm_g, w_branch, w_out, next_g, router):
    T = B * L
    w, wg = _split_w_in(w_in[l])
    proj, mlg = _inproj(x, norm_g[l][None, :], w, wg)
    proj3 = proj.reshape(B, L, N_PROJ)

    zc = _hy_pre(proj3, hy_conv_w[l], hy_conv_b[l][None, :])
    ya_c = _hy_conv(zc.reshape(3 * W_BR, B, L), kk[l])
    ya = _hy_post(ya_c.reshape(W_BR, T))

    yb = _sgu(proj3, sgu_ln_g[l][None, :], sgu_ln_b[l][None, :], sgu_ws[l].astype(BF16),
              sgu_bs[l][:, :, None]).reshape(T, W_BR)

    w_bd, bias_bd = _rg_block_diag(rg_wa[l], rg_ba[l], rg_wx[l], rg_bx[l])
    yc = _rglru(proj3, rg_conv_w[l], rg_conv_b[l][None, :], w_bd, bias_bd, rg_lambda[l]).reshape(T, W_BR)

    gbias = jnp.pad(jnp.concatenate([ml_i_bias[l].reshape(-1), ml_f_bias[l].reshape(-1)]),
                    (0, LANES - 4 * ML_HEADS))[None, :]
    yd = _mlstm(proj3, mlg.reshape(B, L, LANES), gbias, ml_norm_g[l][None, :]).reshape(T, W_BR)

    return _merge(proj, ya, yb, yc, yd, w_branch[l].astype(BF16), w_out[l].astype(BF16), x,
                  next_g[None, :], router)


def kernel(x_prompt, x_sample, norm_mix_g, w_in, hy_conv_w, hy_conv_b, hy_ffn_w1, hy_ffn_b1, hy_ffn_w2, hy_ffn_b2, hy_ffn_w3, hy_bias, sgu_ln_g, sgu_ln_b, sgu_ws, sgu_bs, rg_conv_w, rg_conv_b, rg_wa, rg_ba, rg_wx, rg_bx, rg_lambda, ml_i_bias, ml_f_bias, ml_norm_g, w_branch, w_out, norm_ffn_g, ffn_w1, ffn_w3, ffn_w2, router_w, router_b, moe_w1, moe_w3, moe_w2, norm_final_g):
    bp, L, _ = x_prompt.shape
    bs = x_sample.shape[0]
    B = bp + bs
    depth = w_in.shape[0]
    x = jnp.concatenate([x_prompt, x_sample], axis=0).reshape(B * L, D_MODEL)
    kk = _hy_filters(hy_ffn_w1, hy_ffn_b1, hy_ffn_w2, hy_ffn_b2, hy_ffn_w3, hy_bias, L)
    for l in range(depth):
        j = l // 2
        router = None
        if l % 2 == 1:
            router = (jnp.pad(router_w[j], ((0, 0), (0, LANES - N_EXPERTS))),
                      jnp.pad(router_b[j], (0, LANES - N_EXPERTS))[None, :])
        outs = _token_mixer(x, B, L, l, norm_mix_g, w_in, kk, hy_conv_w, hy_conv_b, sgu_ln_g, sgu_ln_b,
                            sgu_ws, sgu_bs, rg_conv_w, rg_conv_b, rg_wa, rg_ba, rg_wx, rg_bx, rg_lambda,
                            ml_i_bias, ml_f_bias, ml_norm_g, w_branch, w_out, norm_ffn_g[l], router)
        if router is None:
            x, hn = outs
            x = _ffn(hn, x, ffn_w1[j].astype(BF16), ffn_w3[j].astype(BF16), ffn_w2[j].astype(BF16))
        else:
            x, hn, comb = outs
            x = _moe(hn, x, comb, moe_w1[j].astype(BF16), moe_w3[j].astype(BF16), moe_w2[j].astype(BF16))
    y = _final_norm(x, norm_final_g[None, :]).reshape(B, L, D_MODEL)
    return (y[:bp], y[bp:])
```

```python
import functools
import math

import jax
import jax.numpy as jnp
import numpy as np
from jax import lax
from jax.experimental import pallas as pl
from jax.experimental.pallas import tpu as pltpu

F32 = jnp.float32
BF16 = jnp.bfloat16
EPS = 1e-6
HIGHEST = lax.Precision.HIGHEST

LANES = 128
SUBLANES = 8
VMEM_BYTES_V7X = 64 * 1024 * 1024

D_MODEL = 1024
W_BR = 512
N_BRANCH = 4
HY_BANDS = 16
HY_EMB = 1 + 2 * HY_BANDS
HY_FFN = 64
HY_TARGET = 1e-2
HY_MIN_DECAY = -math.log(HY_TARGET) / 1.5
HY_MAX_DECAY = -math.log(HY_TARGET) / 0.3
SGU_CHUNK = 128
SGU_GROUPS = 4
RG_HEADS = 8
RG_HD = W_BR // RG_HEADS
RG_C = 8.0
ML_HEADS = 4
ML_HD = W_BR // ML_HEADS
ML_CHUNK = 128
N_EXPERTS = 8

N_PROJ = N_BRANCH * D_MODEL + 3 * W_BR + 2 * W_BR + 2 * W_BR + 4 * W_BR
COL_GATE = 0
COL_HY = N_BRANCH * D_MODEL
COL_SGU = COL_HY + 3 * W_BR
COL_RG = COL_SGU + 2 * W_BR
COL_ML = COL_RG + 2 * W_BR


def _params(semantics, vmem_mb):
    return pltpu.CompilerParams(dimension_semantics=semantics,
                                vmem_limit_bytes=vmem_mb * 1024 * 1024)


def _sigmoid(x):
    return 1.0 / (1.0 + jnp.exp(-x))


def _rms(x, g):
    return x * lax.rsqrt(jnp.mean(x * x, axis=-1, keepdims=True) + EPS) * g


def _inproj_kernel(x_ref, g_ref, w_ref, wg_ref, proj_ref, mlg_ref, h_ref):
    @pl.when(pl.program_id(1) == 0)
    def _():
        h = _rms(x_ref[...], g_ref[...]).astype(BF16)
        h_ref[...] = h
        mlg_ref[...] = jnp.dot(h, wg_ref[...], preferred_element_type=F32)

    proj_ref[...] = jnp.dot(h_ref[...], w_ref[...], preferred_element_type=F32).astype(BF16)


def _inproj(x, g, w, wg, tm=1024, n_split=4):
    T = x.shape[0]
    tn = N_PROJ // n_split
    return pl.pallas_call(
        _inproj_kernel,
        out_shape=(jax.ShapeDtypeStruct((T, N_PROJ), BF16),
                   jax.ShapeDtypeStruct((T, LANES), F32)),
        grid=(T // tm, n_split),
        in_specs=[pl.BlockSpec((tm, D_MODEL), lambda i, j: (i, 0)),
                  pl.BlockSpec((1, D_MODEL), lambda i, j: (0, 0)),
                  pl.BlockSpec((D_MODEL, tn), lambda i, j: (0, j)),
                  pl.BlockSpec((D_MODEL, LANES), lambda i, j: (0, 0))],
        out_specs=(pl.BlockSpec((tm, tn), lambda i, j: (i, j)),
                   pl.BlockSpec((tm, LANES), lambda i, j: (i, 0))),
        scratch_shapes=[pltpu.VMEM((tm, D_MODEL), BF16)],
        compiler_params=_params(("parallel", "arbitrary"), 48),
        name="inproj",
    )(x, g, w, wg)


def _hy_filter_kernel(z_ref, w1_ref, b1_ref, w2_ref, b2_ref, w3_ref, delta_ref, bias_ref, kk_ref):
    half = pl.program_id(2)
    z = z_ref[...]
    h = jnp.sin(jnp.dot(w1_ref[...], z, precision=HIGHEST, preferred_element_type=F32) + b1_ref[...])
    h = jnp.sin(jnp.dot(w2_ref[...], h, precision=HIGHEST, preferred_element_type=F32) + b2_ref[...])
    f = jnp.dot(w3_ref[...], h, precision=HIGHEST, preferred_element_type=F32)
    t_norm = z[0:1, :]
    f = f * jnp.exp(-t_norm * delta_ref[...])
    lane = lax.broadcasted_iota(jnp.int32, f.shape, 1)
    first = lane == 0
    f = jnp.where(first, jnp.where(half == 0, 0.0, f + bias_ref[...]), f)
    kk_ref[...] = f


def _hy_filters(hy_ffn_w1, hy_ffn_b1, hy_ffn_w2, hy_ffn_b2, hy_ffn_w3, hy_bias, L):
    depth = hy_ffn_w1.shape[0]
    lag = np.stack([L - np.arange(L), np.arange(L)]).astype(np.float64)
    bands = np.arange(1, HY_BANDS + 1, dtype=np.float64)
    ang = (2.0 * math.pi / L) * lag[:, None, :] * bands[None, :, None]
    z = np.concatenate([lag[:, None, :] / L, np.cos(ang), np.sin(ang)], axis=1)
    z = np.pad(z, ((0, 0), (0, LANES - HY_EMB), (0, 0))).astype(np.float32)
    z = jnp.asarray(z)
    w1t = jnp.pad(jnp.swapaxes(hy_ffn_w1, 1, 2), ((0, 0), (0, 0), (0, LANES - HY_EMB)))
    w2t = jnp.swapaxes(hy_ffn_w2, 1, 2)
    w3t = jnp.swapaxes(hy_ffn_w3, 1, 2).reshape(depth, 4, W_BR, HY_FFN)
    b1 = hy_ffn_b1[:, :, None]
    b2 = hy_ffn_b2[:, :, None]
    delta = jnp.linspace(HY_MIN_DECAY, HY_MAX_DECAY, W_BR, dtype=F32)[:, None]
    bias = hy_bias[:, :, :, None]
    return pl.pallas_call(
        _hy_filter_kernel,
        out_shape=jax.ShapeDtypeStruct((depth, 2, W_BR, 2 * L), F32),
        grid=(depth, 2, 2),
        in_specs=[pl.BlockSpec((None, LANES, L), lambda l, o, s: (s, 0, 0)),
                  pl.BlockSpec((None, HY_FFN, LANES), lambda l, o, s: (l, 0, 0)),
                  pl.BlockSpec((None, HY_FFN, 1), lambda l, o, s: (l, 0, 0)),
                  pl.BlockSpec((None, HY_FFN, HY_FFN), lambda l, o, s: (l, 0, 0)),
                  pl.BlockSpec((None, HY_FFN, 1), lambda l, o, s: (l, 0, 0)),
                  pl.BlockSpec((None, None, W_BR, HY_FFN), lambda l, o, s: (l, 2 * o + 1 - s, 0, 0)),
                  pl.BlockSpec((W_BR, 1), lambda l, o, s: (0, 0)),
                  pl.BlockSpec((None, None, W_BR, 1), lambda l, o, s: (l, o, 0, 0))],
        out_specs=pl.BlockSpec((None, None, W_BR, L), lambda l, o, s: (l, o, 0, s)),
        compiler_params=_params(("parallel", "parallel", "parallel"), 48),
        name="hy_filters",
    )(z, w1t, b1, w2t, b2, w3t, delta, bias)


HY_PRE_NB = 8


def _hy_pre_kernel(p_ref, w_ref, b_ref, o_ref):
    L = p_ref.shape[1]
    w = w_ref[...]
    row = lax.broadcasted_iota(jnp.int32, (L, LANES), 0)

    def body(bi, carry):
        x = p_ref[bi].astype(F32)
        xm = jnp.where(row == 0, 0.0, pltpu.roll(x, 1, 0))
        xp = jnp.where(row == L - 1, 0.0, pltpu.roll(x, L - 1, 0))
        u = b_ref[...] + w[0:1] * xm + w[1:2] * x + w[2:3] * xp
        o_ref[:, pl.ds(pl.multiple_of(bi * L, LANES), L)] = u.T.astype(BF16)
        return carry

    lax.fori_loop(0, HY_PRE_NB, body, 0)


def _hy_pre(proj3, conv_w, conv_b):
    B, L, _ = proj3.shape
    nc = 3 * W_BR // LANES
    return pl.pallas_call(
        _hy_pre_kernel,
        out_shape=jax.ShapeDtypeStruct((3 * W_BR, B * L), BF16),
        grid=(B // HY_PRE_NB, nc),
        in_specs=[pl.BlockSpec((HY_PRE_NB, L, LANES), lambda b, c: (b, 0, COL_HY // LANES + c)),
                  pl.BlockSpec((3, LANES), lambda b, c: (0, c)),
                  pl.BlockSpec((1, LANES), lambda b, c: (0, c))],
        out_specs=pl.BlockSpec((LANES, HY_PRE_NB * L), lambda b, c: (c, b)),
        compiler_params=_params(("parallel", "parallel"), 48),
        name="hy_pre",
    )(proj3, conv_w, conv_b)


HY_CB = 8


def _hy_build_table(kk_ref, o, ci, g_ref, L):
    nblk = (2 * L - LANES) // LANES
    for m in range(nblk):
        lo = 2 * L - 2 * LANES - LANES * m
        seg = kk_ref[o, pl.ds(ci, 1), lo:lo + 2 * LANES]
        x = jnp.broadcast_to(seg, (LANES, 2 * LANES))
        r = pltpu.roll(x, 0, 1, stride=1, stride_axis=0)
        g_ref[LANES * m:LANES * (m + 1), :] = r[:, LANES:].astype(BF16)


def _hy_toeplitz_matmul(u, g_ref, L):
    acc = None
    kt = 2 * LANES
    for sb in range(L // kt):
        rhs = jnp.concatenate(
            [g_ref[kt * sb - LANES * tb + L - LANES:kt * sb - LANES * tb + L - LANES + kt, :]
             for tb in range(L // LANES)], axis=1)
        part = jnp.dot(u[:, kt * sb:kt * (sb + 1)], rhs, preferred_element_type=F32)
        acc = part if acc is None else acc + part
    return acc


def _hy_conv_kernel(z_ref, x1_ref, x2_ref, kk_ref, o_ref, g0_ref, g1_ref):
    L = z_ref.shape[2]

    def body(ci, carry):
        _hy_build_table(kk_ref, 0, ci, g0_ref, L)
        _hy_build_table(kk_ref, 1, ci, g1_ref, L)
        y0 = _hy_toeplitz_matmul(z_ref[ci], g0_ref, L)
        z1 = x1_ref[ci].astype(F32) * y0
        y1 = _hy_toeplitz_matmul(z1.astype(BF16), g1_ref, L)
        o_ref[ci] = (x2_ref[ci].astype(F32) * y1).astype(BF16)
        return carry

    lax.fori_loop(0, HY_CB, body, 0)


def _hy_conv(zc3, kk):
    _, B, L = zc3.shape
    nblk = W_BR // HY_CB
    act = lambda off: pl.BlockSpec((HY_CB, B, L), lambda c: (c + off * nblk, 0, 0))
    return pl.pallas_call(
        _hy_conv_kernel,
        out_shape=jax.ShapeDtypeStruct((W_BR, B, L), BF16),
        grid=(nblk,),
        in_specs=[act(0), act(1), act(2),
                  pl.BlockSpec((2, HY_CB, 2 * L), lambda c: (0, c, 0))],
        out_specs=pl.BlockSpec((HY_CB, B, L), lambda c: (c, 0, 0)),
        scratch_shapes=[pltpu.VMEM((2 * L - LANES, LANES), BF16),
                        pltpu.VMEM((2 * L - LANES, LANES), BF16)],
        compiler_params=_params(("parallel",), 48),
        name="hy_conv",
    )(zc3, zc3, zc3, kk)


HY_POST_TT = 8192


def _hy_post_kernel(y_ref, o_ref):
    o_ref[...] = y_ref[...].astype(F32).T.astype(BF16)


def _hy_post(yc):
    C, T = yc.shape
    return pl.pallas_call(
        _hy_post_kernel,
        out_shape=jax.ShapeDtypeStruct((T, C), BF16),
        grid=(T // HY_POST_TT, C // LANES),
        in_specs=[pl.BlockSpec((LANES, HY_POST_TT), lambda t, c: (c, t))],
        out_specs=pl.BlockSpec((HY_POST_TT, LANES), lambda t, c: (t, c)),
        compiler_params=_params(("parallel", "parallel"), 48),
        name="hy_post",
    )(yc)


def _sgu_kernel(u_ref, v_ref, g_ref, b_ref, ws_ref, bs_ref, o_ref):
    L = u_ref.shape[0]
    gw = W_BR // SGU_GROUPS

    def body(n, carry):
        rows = pl.ds(pl.multiple_of(n * SGU_CHUNK, SGU_CHUNK), SGU_CHUNK)
        v = v_ref[rows, :].astype(F32)
        mu = jnp.mean(v, axis=-1, keepdims=True)
        d = v - mu
        var = jnp.mean(d * d, axis=-1, keepdims=True)
        vn = (d * lax.rsqrt(var + EPS) * g_ref[...] + b_ref[...]).astype(BF16)
        mixed = jnp.concatenate(
            [jnp.dot(ws_ref[k], vn[:, gw * k:gw * (k + 1)], preferred_element_type=F32) + bs_ref[k]
             for k in range(SGU_GROUPS)], axis=1)
        o_ref[rows, :] = (u_ref[rows, :].astype(F32) * mixed).astype(BF16)
        return carry

    lax.fori_loop(0, L // SGU_CHUNK, body, 0)


def _sgu(proj3, ln_g, ln_b, ws, bs):
    B, L, _ = proj3.shape
    cb = COL_SGU // W_BR
    return pl.pallas_call(
        _sgu_kernel,
        out_shape=jax.ShapeDtypeStruct((B, L, W_BR), BF16),
        grid=(B,),
        in_specs=[pl.BlockSpec((None, L, W_BR), lambda b: (b, 0, cb)),
                  pl.BlockSpec((None, L, W_BR), lambda b: (b, 0, cb + 1)),
                  pl.BlockSpec((1, W_BR), lambda b: (0, 0)),
                  pl.BlockSpec((1, W_BR), lambda b: (0, 0)),
                  pl.BlockSpec((SGU_GROUPS, SGU_CHUNK, SGU_CHUNK), lambda b: (0, 0, 0)),
                  pl.BlockSpec((SGU_GROUPS, SGU_CHUNK, 1), lambda b: (0, 0, 0))],
        out_specs=pl.BlockSpec((None, L, W_BR), lambda b: (b, 0, 0)),
        compiler_params=_params(("parallel",), 48),
        name="sgu",
    )(proj3, proj3, ln_g, ln_b, ws, bs)


RG_TL = 256
RG_PAD = 8


def _scan_blocks(a, b, reverse):
    n = a.shape[0]
    pos = lax.broadcasted_iota(jnp.int32, a.shape, 0) & (SUBLANES - 1)
    for k in (1, 2, 4):
        if reverse:
            valid = pos < SUBLANES - k
            shift = n - k
        else:
            valid = pos >= k
            shift = k
        a_s = jnp.where(valid, pltpu.roll(a, shift, 0), 1.0)
        b_s = jnp.where(valid, pltpu.roll(b, shift, 0), 0.0)
        b = a * b_s + b
        a = a * a_s
    return a, b


def _rg_kernel(xb_ref, gb_ref, cw_ref, cb_ref, w_ref, bias_ref, lam_ref, o_ref,
               xpad_ref, af_ref, bf_ref, ab_ref, bb_ref):
    L = xb_ref.shape[0]
    zeros = jnp.zeros((RG_PAD, LANES), F32)
    xpad_ref[0:RG_PAD, :] = zeros
    xpad_ref[RG_PAD + L:RG_PAD + L + RG_PAD, :] = zeros
    xpad_ref[RG_PAD:RG_PAD + L, :] = xb_ref[...].astype(F32)
    cw = cw_ref[...]
    lam = lam_ref[...]
    sp = jnp.maximum(-lam, 0.0) + jnp.log(1.0 + jnp.exp(-jnp.abs(lam)))

    for ti in range(L // RG_TL):
        t0 = ti * RG_TL
        xc = cb_ref[...]
        for j in range(4):
            xc = xc + cw[j:j + 1] * xpad_ref[RG_PAD + t0 + j - 2:RG_PAD + t0 + j - 2 + RG_TL, :]
        gates = jnp.dot(xc.astype(BF16), w_ref[...], preferred_element_type=F32) + bias_ref[...]
        gates = _sigmoid(gates)
        for d, (a_ref, b_ref) in enumerate(((af_ref, bf_ref), (ab_ref, bb_ref))):
            r = gates[:, 2 * d * LANES:(2 * d + 1) * LANES]
            i = gates[:, (2 * d + 1) * LANES:(2 * d + 2) * LANES]
            a = jnp.exp((-RG_C) * r * sp[d:d + 1])
            bt = jnp.sqrt(1.0 - a * a) * (i * xc)
            a_c, b_c = _scan_blocks(a, bt, reverse=(d == 1))
            a_ref[t0:t0 + RG_TL, :] = a_c
            b_ref[t0:t0 + RG_TL, :] = b_c

    nblk = L // SUBLANES

    def body(k, carry):
        hf, hb = carry
        rf = pl.ds(pl.multiple_of(k * SUBLANES, SUBLANES), SUBLANES)
        rb = pl.ds(pl.multiple_of((nblk - 1 - k) * SUBLANES, SUBLANES), SUBLANES)
        f = af_ref[rf, :] * hf + bf_ref[rf, :]
        bf_ref[rf, :] = f
        g = ab_ref[rb, :] * hb + bb_ref[rb, :]
        bb_ref[rb, :] = g
        return (jnp.broadcast_to(f[SUBLANES - 1:SUBLANES, :], (SUBLANES, LANES)),
                jnp.broadcast_to(g[0:1, :], (SUBLANES, LANES)))

    h0 = jnp.zeros((SUBLANES, LANES), F32)
    lax.fori_loop(0, nblk, body, (h0, h0))
    gb = gb_ref[...].astype(F32)
    gelu = 0.5 * gb * (1.0 + jnp.tanh(math.sqrt(2.0 / math.pi) * (gb + 0.044715 * (gb * gb * gb))))
    o_ref[...] = ((bf_ref[...] + bb_ref[...]) * gelu).astype(BF16)


def _rglru(proj3, conv_w, conv_b, w_bd, bias_bd, lam):
    B, L, _ = proj3.shape
    ng = W_BR // LANES
    cx = COL_RG // LANES
    return pl.pallas_call(
        _rg_kernel,
        out_shape=jax.ShapeDtypeStruct((B, L, W_BR), BF16),
        grid=(B, ng),
        in_specs=[pl.BlockSpec((None, L, LANES), lambda b, c: (b, 0, cx + c)),
                  pl.BlockSpec((None, L, LANES), lambda b, c: (b, 0, cx + ng + c)),
                  pl.BlockSpec((4, LANES), lambda b, c: (0, c)),
                  pl.BlockSpec((1, LANES), lambda b, c: (0, c)),
                  pl.BlockSpec((None, LANES, 4 * LANES), lambda b, c: (c, 0, 0)),
                  pl.BlockSpec((None, 1, 4 * LANES), lambda b, c: (c, 0, 0)),
                  pl.BlockSpec((2, LANES), lambda b, c: (0, c))],
        out_specs=pl.BlockSpec((None, L, LANES), lambda b, c: (b, 0, c)),
        scratch_shapes=[pltpu.VMEM((L + 2 * RG_PAD, LANES), F32)] + [pltpu.VMEM((L, LANES), F32)] * 4,
        compiler_params=_params(("parallel", "parallel"), 48),
        name="rglru",
    )(proj3, proj3, conv_w, conv_b, w_bd, bias_bd, lam)


def _ml_kernel(q_ref, k_ref, v_ref, o_ref, gt_ref, gbias_ref, ng_ref, y_ref, h_ref, ct_ref, n_ref, m_ref):
    L = q_ref.shape[0]
    nc = L // ML_CHUNK
    row = lax.broadcasted_iota(jnp.int32, (ML_CHUNK, ML_CHUNK), 0)
    col = lax.broadcasted_iota(jnp.int32, (ML_CHUNK, ML_CHUNK), 1)
    lane = lax.broadcasted_iota(jnp.int32, (ML_CHUNK, LANES), 1)
    is_f = (lane >= 2 * ML_HEADS) & (lane < 4 * ML_HEADS)
    scale = ML_HD ** -0.5

    for d in range(2):
        mask = (col <= row) if d == 0 else (col >= row)
        tri = jnp.where(mask, 1.0, 0.0).astype(BF16)
        ct_ref[...] = jnp.zeros(ct_ref.shape, F32)
        n_ref[...] = jnp.zeros(n_ref.shape, F32)
        m_ref[...] = jnp.zeros(m_ref.shape, F32)

        def body(step, carry, d=d, mask=mask, tri=tri):
            c = step if d == 0 else nc - 1 - step
            rows = pl.ds(pl.multiple_of(c * ML_CHUNK, ML_CHUNK), ML_CHUNK)
            pre = gt_ref[rows, :] + gbias_ref[...]
            logf = jnp.minimum(pre, 0.0) - jnp.log(1.0 + jnp.exp(-jnp.abs(pre)))
            gl = jnp.where(is_f, logf, pre)
            lf = jnp.where(is_f, logf, 0.0)
            lf_a = lf.astype(BF16)
            lf_r = lf - lf_a.astype(F32)
            lf_b = lf_r.astype(BF16)
            lf_c = (lf_r - lf_b.astype(F32)).astype(BF16)
            cum3 = jnp.dot(tri, jnp.concatenate([lf_a, lf_b, lf_c], axis=1), preferred_element_type=F32)
            cum = cum3[:, 0:LANES] + cum3[:, LANES:2 * LANES] + cum3[:, 2 * LANES:3 * LANES]
            cum_t = cum.T
            gl_t = gl.T
            for hd in range(ML_HEADS):
                li_lane = d * ML_HEADS + hd
                lf_lane = 2 * ML_HEADS + d * ML_HEADS + hd
                b_col = cum[:, lf_lane:lf_lane + 1]
                b_row = cum_t[lf_lane:lf_lane + 1, :]
                li_col = gl[:, li_lane:li_lane + 1]
                li_row = gl_t[li_lane:li_lane + 1, :]
                last = ML_CHUNK - 1 if d == 0 else 0
                g_tot = b_col[last:last + 1, :]
                m_prev = m_ref[hd]
                dlog = jnp.where(mask, b_col - b_row + li_row, -jnp.inf)
                inter = b_col + m_prev
                m_t = jnp.maximum(inter, jnp.max(dlog, axis=1, keepdims=True))
                w_intra = jnp.exp(dlog - m_t)
                w_inter = jnp.exp(inter - m_t)
                hs = slice(hd * ML_HD, (hd + 1) * ML_HD)
                qh = q_ref[rows, hs]
                kh = k_ref[rows, hs]
                vh = v_ref[rows, hs]
                kt = kh.astype(F32).T.astype(BF16)
                s = jnp.dot(qh, kt, preferred_element_type=F32) * (scale * w_intra)
                ct = ct_ref[hd]
                q_c = jnp.dot(qh, ct.astype(BF16), preferred_element_type=F32) * scale
                num = w_inter * q_c + jnp.dot(s.astype(BF16), vh, preferred_element_type=F32)
                q_n = jnp.sum(qh.astype(F32) * n_ref[hd], axis=1, keepdims=True) * scale
                den = w_inter * q_n + jnp.sum(s, axis=1, keepdims=True)
                hout = num / jnp.maximum(jnp.abs(den), jnp.exp(-m_t))
                if d == 0:
                    h_ref[rows, hs] = hout
                else:
                    h_ref[rows, hs] = h_ref[rows, hs] + hout
                wlog_col = g_tot - b_col + li_col
                m_new = jnp.maximum(g_tot + m_prev, jnp.max(wlog_col, axis=0, keepdims=True))
                decay = jnp.exp(g_tot + m_prev - m_new)
                ws_col = jnp.exp(wlog_col - m_new)
                vw = (vh.astype(F32) * ws_col).astype(BF16)
                ct_ref[hd] = decay * ct + jnp.dot(kt, vw, preferred_element_type=F32)
                n_ref[hd] = decay * n_ref[hd] + jnp.sum(kh.astype(F32) * ws_col, axis=0, keepdims=True)
                m_ref[hd] = m_new
            return carry

        lax.fori_loop(0, nc, body, 0)

    for hd in range(ML_HEADS):
        hs = slice(hd * ML_HD, (hd + 1) * ML_HD)
        h = h_ref[:, hs]
        hn = h * lax.rsqrt(jnp.mean(h * h, axis=-1, keepdims=True) + EPS) * ng_ref[:, hs]
        y_ref[:, hs] = (_sigmoid(o_ref[:, hs].astype(F32)) * hn).astype(BF16)


def _mlstm(proj3, mlg3, gbias, norm_g):
    B, L, _ = proj3.shape
    cb = COL_ML // W_BR
    col = lambda j: pl.BlockSpec((None, L, W_BR), lambda b: (b, 0, cb + j))
    return pl.pallas_call(
        _ml_kernel,
        out_shape=jax.ShapeDtypeStruct((B, L, W_BR), BF16),
        grid=(B,),
        in_specs=[col(0), col(1), col(2), col(3),
                  pl.BlockSpec((None, L, LANES), lambda b: (b, 0, 0)),
                  pl.BlockSpec((1, LANES), lambda b: (0, 0)),
                  pl.BlockSpec((1, W_BR), lambda b: (0, 0))],
        out_specs=pl.BlockSpec((None, L, W_BR), lambda b: (b, 0, 0)),
        scratch_shapes=[pltpu.VMEM((L, W_BR), F32),
                        pltpu.VMEM((ML_HEADS, ML_HD, ML_HD), F32),
                        pltpu.VMEM((ML_HEADS, 1, ML_HD), F32),
                        pltpu.VMEM((ML_HEADS, 1, 1), F32)],
        compiler_params=_params(("parallel",), 48),
        name="mlstm",
    )(proj3, proj3, proj3, proj3, mlg3, gbias, norm_g)


def _merge_kernel(route, gate_ref, ya_ref, yb_ref, yc_ref, yd_ref, wb_ref, wo_ref, x_ref, g_ref, *rest):
    if route:
        rw_ref, rb_ref, xo_ref, hn_ref, comb_ref, combt_ref = rest
    else:
        xo_ref, hn_ref = rest
    merged = None
    for k, y_ref in enumerate((ya_ref, yb_ref, yc_ref, yd_ref)):
        t = jnp.dot(y_ref[...], wb_ref[k], preferred_element_type=F32)
        gk = _sigmoid(gate_ref[:, k * D_MODEL:(k + 1) * D_MODEL].astype(F32))
        merged = gk * t if merged is None else merged + gk * t
    xn = x_ref[...] + jnp.dot(merged.astype(BF16), wo_ref[...], preferred_element_type=F32)
    xo_ref[...] = xn
    h = _rms(xn, g_ref[...])
    h_hi = h.astype(BF16)
    hn_ref[...] = h_hi
    if route:
        h_lo = (h - h_hi.astype(F32)).astype(BF16)
        r_hi = jnp.dot(h_hi, rw_ref[...], preferred_element_type=F32)
        r_lo = jnp.dot(h_lo, rw_ref[:, 0:LANES], preferred_element_type=F32)
        logits = r_hi[:, 0:LANES] + r_hi[:, LANES:2 * LANES] + r_lo + rb_ref[...]
        lane = lax.broadcasted_iota(jnp.int32, logits.shape, 1)
        logits = jnp.where(lane < N_EXPERTS, logits, -jnp.inf)
        v1 = jnp.max(logits, axis=1, keepdims=True)
        i1 = jnp.min(jnp.where(logits == v1, lane, LANES), axis=1, keepdims=True)
        rest_l = jnp.where(lane == i1, -jnp.inf, logits)
        v2 = jnp.max(rest_l, axis=1, keepdims=True)
        i2 = jnp.min(jnp.where(rest_l == v2, lane, LANES), axis=1, keepdims=True)
        e2 = jnp.exp(v2 - v1)
        p1 = 1.0 / (1.0 + e2)
        comb = jnp.where(lane == i1, p1, jnp.where(lane == i2, e2 * p1, 0.0))
        comb_ref[...] = comb
        combt_ref[...] = comb.T


def _merge(proj, ya, yb, yc, yd, wb, wo, x, g, router=None, tm=512):
    T = x.shape[0]
    row = lambda w: pl.BlockSpec((tm, w), lambda i: (i, 0))
    in_specs = [row(N_BRANCH * D_MODEL), row(W_BR), row(W_BR), row(W_BR), row(W_BR),
                pl.BlockSpec((N_BRANCH, W_BR, D_MODEL), lambda i: (0, 0, 0)),
                pl.BlockSpec((D_MODEL, D_MODEL), lambda i: (0, 0)),
                row(D_MODEL),
                pl.BlockSpec((1, D_MODEL), lambda i: (0, 0))]
    out_shape = [jax.ShapeDtypeStruct((T, D_MODEL), F32), jax.ShapeDtypeStruct((T, D_MODEL), BF16)]
    out_specs = [row(D_MODEL), row(D_MODEL)]
    args = [proj, ya, yb, yc, yd, wb, wo, x, g]
    if router is not None:
        in_specs += [pl.BlockSpec((D_MODEL, 2 * LANES), lambda i: (0, 0)),
                     pl.BlockSpec((1, LANES), lambda i: (0, 0))]
        out_shape += [jax.ShapeDtypeStruct((T, LANES), F32), jax.ShapeDtypeStruct((LANES, T), F32)]
        out_specs += [row(LANES), pl.BlockSpec((LANES, tm), lambda i: (0, i))]
        args += list(router)
    return pl.pallas_call(
        functools.partial(_merge_kernel, router is not None),
        out_shape=tuple(out_shape),
        grid=(T // tm,),
        in_specs=in_specs,
        out_specs=tuple(out_specs),
        compiler_params=_params(("parallel",), 56),
        name="merge_route" if router is not None else "merge",
    )(*args)


def _swiglu_acc(h, w1_ref, w3_ref, w2_ref):
    a = jnp.dot(h, w1_ref[...], preferred_element_type=F32)
    b = jnp.dot(h, w3_ref[...], preferred_element_type=F32)
    act = (a * _sigmoid(a) * b).astype(BF16)
    return jnp.dot(act, w2_ref[...], preferred_element_type=F32)


def _ffn_kernel(h_ref, x_ref, w1_ref, w3_ref, w2_ref, o_ref, acc_ref):
    f = pl.program_id(1)

    @pl.when(f == 0)
    def _():
        acc_ref[...] = x_ref[...]

    acc_ref[...] += _swiglu_acc(h_ref[...], w1_ref, w3_ref, w2_ref)

    @pl.when(f == pl.num_programs(1) - 1)
    def _():
        o_ref[...] = acc_ref[...]


def _ffn(hn, x, w1, w3, w2, tm=512, n_split=2):
    T = x.shape[0]
    dff = w1.shape[1]
    tf = dff // n_split
    return pl.pallas_call(
        _ffn_kernel,
        out_shape=jax.ShapeDtypeStruct((T, D_MODEL), F32),
        grid=(T // tm, n_split),
        in_specs=[pl.BlockSpec((tm, D_MODEL), lambda i, f: (i, 0)),
                  pl.BlockSpec((tm, D_MODEL), lambda i, f: (i, 0)),
                  pl.BlockSpec((D_MODEL, tf), lambda i, f: (0, f)),
                  pl.BlockSpec((D_MODEL, tf), lambda i, f: (0, f)),
                  pl.BlockSpec((tf, D_MODEL), lambda i, f: (f, 0))],
        out_specs=pl.BlockSpec((tm, D_MODEL), lambda i, f: (i, 0)),
        scratch_shapes=[pltpu.VMEM((tm, D_MODEL), F32)],
        compiler_params=_params(("parallel", "arbitrary"), 56),
        name="ffn",
    )(hn, x, w1, w3, w2)


MOE_TM = 1024
MOE_CH = 256


def _moe_kernel(cnt_ref, h_ref, x_ref, comb_ref, combt_ref, w1_ref, w3_ref, w2_ref, o_ref,
                rank_ref, rankt_ref):
    i = pl.program_id(0)
    e = pl.program_id(1)
    tm = h_ref.shape[0]

    @pl.when(e == 0)
    def _():
        o_ref[...] = x_ref[...]
        r = lax.broadcasted_iota(jnp.int32, (tm, tm), 0)
        c = lax.broadcasted_iota(jnp.int32, (tm, tm), 1)
        before = jnp.where(c < r, 1.0, 0.0).astype(BF16)
        sel = jnp.where(comb_ref[...] > 0.0, 1.0, 0.0).astype(BF16)
        rank_ref[...] = jnp.dot(before, sel, preferred_element_type=F32)
        selt = jnp.where(combt_ref[...] > 0.0, 1.0, 0.0).astype(BF16)
        rankt_ref[...] = lax.dot_general(selt, before, (((1,), (1,)), ((), ())),
                                         preferred_element_type=F32)

    lane = lax.broadcasted_iota(jnp.int32, (tm, LANES), 1)
    comb = comb_ref[...]
    c_col = jnp.sum(jnp.where(lane == e, comb, 0.0), axis=1, keepdims=True)
    rank_col = jnp.sum(jnp.where(lane == e, rank_ref[...], 0.0), axis=1, keepdims=True)
    rank_col = jnp.where(c_col > 0.0, rank_col, -1.0)
    c_row = combt_ref[pl.ds(e, 1), :]
    rank_row = jnp.where(c_row > 0.0, rankt_ref[pl.ds(e, 1), :], -1.0)
    cnt = cnt_ref[i * N_EXPERTS + e]

    def run_chunk(base, size):
        pos_r = (lax.broadcasted_iota(jnp.int32, (size, tm), 0) + base).astype(F32)
        gather = jnp.where(rank_row == pos_r, 1.0, 0.0).astype(BF16)
        xs = jnp.dot(gather, h_ref[...], preferred_element_type=F32).astype(BF16)
        y = _swiglu_acc(xs, w1_ref, w3_ref, w2_ref).astype(BF16)
        pos_c = (lax.broadcasted_iota(jnp.int32, (tm, size), 1) + base).astype(F32)
        scatter = jnp.where(rank_col == pos_c, c_col, 0.0).astype(BF16)
        o_ref[...] += jnp.dot(scatter, y, preferred_element_type=F32)

    n_full = cnt // MOE_CH
    rem = cnt - n_full * MOE_CH

    def body(j, carry):
        run_chunk(j * MOE_CH, MOE_CH)
        return carry

    lax.fori_loop(0, n_full, body, 0)

    @pl.when(rem > MOE_CH // 2)
    def _():
        run_chunk(n_full * MOE_CH, MOE_CH)

    @pl.when((rem > 0) & (rem <= MOE_CH // 2))
    def _():
        run_chunk(n_full * MOE_CH, MOE_CH // 2)


def _moe(hn, x, comb, combt, w1, w3, w2, tm=MOE_TM):
    T = x.shape[0]
    dfe = w1.shape[2]
    nt = T // tm
    cnt = jnp.sum((comb[:, :N_EXPERTS] > 0.0).reshape(nt, tm, N_EXPERTS), axis=1, dtype=jnp.int32).reshape(-1)
    grid_spec = pltpu.PrefetchScalarGridSpec(
        num_scalar_prefetch=1,
        grid=(nt, N_EXPERTS),
        in_specs=[pl.BlockSpec((tm, D_MODEL), lambda i, e, c: (i, 0)),
                  pl.BlockSpec((tm, D_MODEL), lambda i, e, c: (i, 0)),
                  pl.BlockSpec((tm, LANES), lambda i, e, c: (i, 0)),
                  pl.BlockSpec((LANES, tm), lambda i, e, c: (0, i)),
                  pl.BlockSpec((None, D_MODEL, dfe), lambda i, e, c: (e, 0, 0)),
                  pl.BlockSpec((None, D_MODEL, dfe), lambda i, e, c: (e, 0, 0)),
                  pl.BlockSpec((None, dfe, D_MODEL), lambda i, e, c: (e, 0, 0))],
        out_specs=pl.BlockSpec((tm, D_MODEL), lambda i, e, c: (i, 0)),
        scratch_shapes=[pltpu.VMEM((tm, LANES), F32), pltpu.VMEM((LANES, tm), F32)])
    return pl.pallas_call(
        _moe_kernel,
        out_shape=jax.ShapeDtypeStruct((T, D_MODEL), F32),
        grid_spec=grid_spec,
        compiler_params=_params(("parallel", "arbitrary"), 56),
        name="moe",
    )(cnt, hn, x, comb, combt, w1, w3, w2)


def _final_norm_kernel(x_ref, g_ref, o_ref):
    o_ref[...] = _rms(x_ref[...], g_ref[...])


def _final_norm(x, g, tm=1024):
    T = x.shape[0]
    return pl.pallas_call(
        _final_norm_kernel,
        out_shape=jax.ShapeDtypeStruct((T, D_MODEL), F32),
        grid=(T // tm,),
        in_specs=[pl.BlockSpec((tm, D_MODEL), lambda i: (i, 0)),
                  pl.BlockSpec((1, D_MODEL), lambda i: (0, 0))],
        out_specs=pl.BlockSpec((tm, D_MODEL), lambda i: (i, 0)),
        compiler_params=_params(("parallel",), 48),
        name="final_norm",
    )(x, g)


def _split_w_in(w_in_l):
    n_hy, n_sgu, n_rg = 3 * W_BR, 2 * W_BR, 2 * W_BR
    o1 = n_hy
    o2 = o1 + n_sgu
    o3 = o2 + n_rg
    o4 = o3 + 4 * W_BR
    o5 = o4 + 4 * ML_HEADS
    w = jnp.concatenate([w_in_l[:, o5:], w_in_l[:, :o4]], axis=1).astype(BF16)
    wg = jnp.pad(w_in_l[:, o4:o5], ((0, 0), (0, LANES - 4 * ML_HEADS))).astype(BF16)
    return w, wg


def _router_operands(router_w, router_b):
    w = jnp.pad(router_w, ((0, 0), (0, LANES - N_EXPERTS)))
    w_hi = w.astype(BF16)
    w_lo = (w - w_hi.astype(F32)).astype(BF16)
    return (jnp.concatenate([w_hi, w_lo], axis=1),
            jnp.pad(router_b, (0, LANES - N_EXPERTS))[None, :])


def _rg_block_diag(wa, ba, wx, bx):
    hpg = LANES // RG_HD
    ng = RG_HEADS // hpg
    eye = jnp.eye(hpg, dtype=F32)

    def bd(w):
        w = w.reshape(ng, hpg, RG_HD, RG_HD)
        return jnp.einsum('gaij,ab->gaibj', w, eye).reshape(ng, LANES, LANES)

    w = jnp.concatenate([bd(wa[0]), bd(wx[0]), bd(wa[1]), bd(wx[1])], axis=2).astype(BF16)
    fl = lambda b: b.reshape(ng, 1, LANES)
    bias = jnp.concatenate([fl(ba[0]), fl(bx[0]), fl(ba[1]), fl(bx[1])], axis=2)
    return w, bias


def _token_mixer(x, B, L, l, norm_g, w_in, kk, hy_conv_w, hy_conv_b, sgu_ln_g, sgu_ln_b, sgu_ws, sgu_bs,
                 rg_conv_w, rg_conv_b, rg_wa, rg_ba, rg_wx, rg_bx, rg_lambda, ml_i_bias, ml_f_bias,
                 ml_norm_g, w_branch, w_out, next_g, router):
    T = B * L
    w, wg = _split_w_in(w_in[l])
    proj, mlg = _inproj(x, norm_g[l][None, :], w, wg)
    proj3 = proj.reshape(B, L, N_PROJ)

    zc = _hy_pre(proj3, hy_conv_w[l], hy_conv_b[l][None, :])
    ya_c = _hy_conv(zc.reshape(3 * W_BR, B, L), kk[l])
    ya = _hy_post(ya_c.reshape(W_BR, T))

    yb = _sgu(proj3, sgu_ln_g[l][None, :], sgu_ln_b[l][None, :], sgu_ws[l].astype(BF16),
              sgu_bs[l][:, :, None]).reshape(T, W_BR)

    w_bd, bias_bd = _rg_block_diag(rg_wa[l], rg_ba[l], rg_wx[l], rg_bx[l])
    yc = _rglru(proj3, rg_conv_w[l], rg_conv_b[l][None, :], w_bd, bias_bd, rg_lambda[l]).reshape(T, W_BR)

    gbias = jnp.pad(jnp.concatenate([ml_i_bias[l].reshape(-1), ml_f_bias[l].reshape(-1)]),
                    (0, LANES - 4 * ML_HEADS))[None, :]
    yd = _mlstm(proj3, mlg.reshape(B, L, LANES), gbias, ml_norm_g[l][None, :]).reshape(T, W_BR)

    return _merge(proj, ya, yb, yc, yd, w_branch[l].astype(BF16), w_out[l].astype(BF16), x,
                  next_g[None, :], router)


def kernel(x_prompt, x_sample, norm_mix_g, w_in, hy_conv_w, hy_conv_b, hy_ffn_w1, hy_ffn_b1, hy_ffn_w2, hy_ffn_b2, hy_ffn_w3, hy_bias, sgu_ln_g, sgu_ln_b, sgu_ws, sgu_bs, rg_conv_w, rg_conv_b, rg_wa, rg_ba, rg_wx, rg_bx, rg_lambda, ml_i_bias, ml_f_bias, ml_norm_g, w_branch, w_out, norm_ffn_g, ffn_w1, ffn_w3, ffn_w2, router_w, router_b, moe_w1, moe_w3, moe_w2, norm_final_g):
    bp, L, _ = x_prompt.shape
    bs = x_sample.shape[0]
    B = bp + bs
    depth = w_in.shape[0]
    x = jnp.concatenate([x_prompt, x_sample], axis=0).reshape(B * L, D_MODEL)
    kk = _hy_filters(hy_ffn_w1, hy_ffn_b1, hy_ffn_w2, hy_ffn_b2, hy_ffn_w3, hy_bias, L)
    for l in range(depth):
        j = l // 2
        router = None
        if l % 2 == 1:
            router = _router_operands(router_w[j], router_b[j])
        outs = _token_mixer(x, B, L, l, norm_mix_g, w_in, kk, hy_conv_w, hy_conv_b, sgu_ln_g, sgu_ln_b,
                            sgu_ws, sgu_bs, rg_conv_w, rg_conv_b, rg_wa, rg_ba, rg_wx, rg_bx, rg_lambda,
                            ml_i_bias, ml_f_bias, ml_norm_g, w_branch, w_out, norm_ffn_g[l], router)
        if router is None:
            x, hn = outs
            x = _ffn(hn, x, ffn_w1[j].astype(BF16), ffn_w3[j].astype(BF16), ffn_w2[j].astype(BF16))
        else:
            x, hn, comb, combt = outs
            x = _moe(hn, x, comb, combt, moe_w1[j].astype(BF16), moe_w3[j].astype(BF16),
                     moe_w2[j].astype(BF16))
    y = _final_norm(x, norm_final_g[None, :]).reshape(B, L, D_MODEL)
    return (y[:bp], y[bp:])
```

```python
import functools
import math

import jax
import jax.numpy as jnp
import numpy as np
from jax import lax
from jax.experimental import pallas as pl
from jax.experimental.pallas import tpu as pltpu

F32 = jnp.float32
BF16 = jnp.bfloat16
EPS = 1e-6
HIGHEST = lax.Precision.HIGHEST

LANES = 128
SUBLANES = 8
VMEM_BYTES_V7X = 64 * 1024 * 1024

D_MODEL = 1024
W_BR = 512
N_BRANCH = 4
HY_BANDS = 16
HY_EMB = 1 + 2 * HY_BANDS
HY_FFN = 64
HY_TARGET = 1e-2
HY_MIN_DECAY = -math.log(HY_TARGET) / 1.5
HY_MAX_DECAY = -math.log(HY_TARGET) / 0.3
SGU_CHUNK = 128
SGU_GROUPS = 4
RG_HEADS = 8
RG_HD = W_BR // RG_HEADS
RG_C = 8.0
ML_HEADS = 4
ML_HD = W_BR // ML_HEADS
ML_CHUNK = 128
N_EXPERTS = 8

N_PROJ = N_BRANCH * D_MODEL + 3 * W_BR + 2 * W_BR + 2 * W_BR + 3 * W_BR
COL_GATE = 0
COL_HY = N_BRANCH * D_MODEL
COL_SGU = COL_HY + 3 * W_BR
COL_RG = COL_SGU + 2 * W_BR
COL_ML = COL_RG + 2 * W_BR


def _params(semantics, vmem_mb):
    return pltpu.CompilerParams(dimension_semantics=semantics,
                                vmem_limit_bytes=vmem_mb * 1024 * 1024)


def _sigmoid(x):
    return 1.0 / (1.0 + jnp.exp(-x))


def _rms(x, g):
    return x * lax.rsqrt(jnp.mean(x * x, axis=-1, keepdims=True) + EPS) * g


_NT = (((1,), (1,)), ((), ()))


def _inproj_kernel(x_ref, g_ref, w_ref, wg_ref, wgt_ref, wkt_ref, proj_ref, mlg_ref, mlgt_ref, kt_ref, h_ref):
    @pl.when(pl.program_id(1) == 0)
    def _():
        h = _rms(x_ref[...], g_ref[...]).astype(BF16)
        h_ref[...] = h
        mlg_ref[...] = jnp.dot(h, wg_ref[...], preferred_element_type=F32)
        mlgt_ref[...] = lax.dot_general(wgt_ref[...], h, _NT, preferred_element_type=F32)
        kt_ref[...] = lax.dot_general(wkt_ref[...], h, _NT, preferred_element_type=F32).astype(BF16)

    proj_ref[...] = jnp.dot(h_ref[...], w_ref[...], preferred_element_type=F32).astype(BF16)


def _inproj(x, g, w, wg, wkt, tm=1024, n_split=4):
    T = x.shape[0]
    tn = N_PROJ // n_split
    return pl.pallas_call(
        _inproj_kernel,
        out_shape=(jax.ShapeDtypeStruct((T, N_PROJ), BF16),
                   jax.ShapeDtypeStruct((T, LANES), F32),
                   jax.ShapeDtypeStruct((LANES, T), F32),
                   jax.ShapeDtypeStruct((W_BR, T), BF16)),
        grid=(T // tm, n_split),
        in_specs=[pl.BlockSpec((tm, D_MODEL), lambda i, j: (i, 0)),
                  pl.BlockSpec((1, D_MODEL), lambda i, j: (0, 0)),
                  pl.BlockSpec((D_MODEL, tn), lambda i, j: (0, j)),
                  pl.BlockSpec((D_MODEL, LANES), lambda i, j: (0, 0)),
                  pl.BlockSpec((LANES, D_MODEL), lambda i, j: (0, 0)),
                  pl.BlockSpec((W_BR, D_MODEL), lambda i, j: (0, 0))],
        out_specs=(pl.BlockSpec((tm, tn), lambda i, j: (i, j)),
                   pl.BlockSpec((tm, LANES), lambda i, j: (i, 0)),
                   pl.BlockSpec((LANES, tm), lambda i, j: (0, i)),
                   pl.BlockSpec((W_BR, tm), lambda i, j: (0, i))),
        scratch_shapes=[pltpu.VMEM((tm, D_MODEL), BF16)],
        compiler_params=_params(("parallel", "arbitrary"), 48),
        name="inproj",
    )(x, g, w, wg, wg.T, wkt)


def _hy_filter_kernel(z_ref, w1_ref, b1_ref, w2_ref, b2_ref, w3_ref, delta_ref, bias_ref, kk_ref):
    half = pl.program_id(2)
    z = z_ref[...]
    h = jnp.sin(jnp.dot(w1_ref[...], z, precision=HIGHEST, preferred_element_type=F32) + b1_ref[...])
    h = jnp.sin(jnp.dot(w2_ref[...], h, precision=HIGHEST, preferred_element_type=F32) + b2_ref[...])
    f = jnp.dot(w3_ref[...], h, precision=HIGHEST, preferred_element_type=F32)
    t_norm = z[0:1, :]
    f = f * jnp.exp(-t_norm * delta_ref[...])
    lane = lax.broadcasted_iota(jnp.int32, f.shape, 1)
    first = lane == 0
    f = jnp.where(first, jnp.where(half == 0, 0.0, f + bias_ref[...]), f)
    kk_ref[...] = f


def _hy_filters(hy_ffn_w1, hy_ffn_b1, hy_ffn_w2, hy_ffn_b2, hy_ffn_w3, hy_bias, L):
    depth = hy_ffn_w1.shape[0]
    lag = np.stack([L - np.arange(L), np.arange(L)]).astype(np.float64)
    bands = np.arange(1, HY_BANDS + 1, dtype=np.float64)
    ang = (2.0 * math.pi / L) * lag[:, None, :] * bands[None, :, None]
    z = np.concatenate([lag[:, None, :] / L, np.cos(ang), np.sin(ang)], axis=1)
    z = np.pad(z, ((0, 0), (0, LANES - HY_EMB), (0, 0))).astype(np.float32)
    z = jnp.asarray(z)
    w1t = jnp.pad(jnp.swapaxes(hy_ffn_w1, 1, 2), ((0, 0), (0, 0), (0, LANES - HY_EMB)))
    w2t = jnp.swapaxes(hy_ffn_w2, 1, 2)
    w3t = jnp.swapaxes(hy_ffn_w3, 1, 2).reshape(depth, 4, W_BR, HY_FFN)
    b1 = hy_ffn_b1[:, :, None]
    b2 = hy_ffn_b2[:, :, None]
    delta = jnp.linspace(HY_MIN_DECAY, HY_MAX_DECAY, W_BR, dtype=F32)[:, None]
    bias = hy_bias[:, :, :, None]
    return pl.pallas_call(
        _hy_filter_kernel,
        out_shape=jax.ShapeDtypeStruct((depth, 2, W_BR, 2 * L), F32),
        grid=(depth, 2, 2),
        in_specs=[pl.BlockSpec((None, LANES, L), lambda l, o, s: (s, 0, 0)),
                  pl.BlockSpec((None, HY_FFN, LANES), lambda l, o, s: (l, 0, 0)),
                  pl.BlockSpec((None, HY_FFN, 1), lambda l, o, s: (l, 0, 0)),
                  pl.BlockSpec((None, HY_FFN, HY_FFN), lambda l, o, s: (l, 0, 0)),
                  pl.BlockSpec((None, HY_FFN, 1), lambda l, o, s: (l, 0, 0)),
                  pl.BlockSpec((None, None, W_BR, HY_FFN), lambda l, o, s: (l, 2 * o + 1 - s, 0, 0)),
                  pl.BlockSpec((W_BR, 1), lambda l, o, s: (0, 0)),
                  pl.BlockSpec((None, None, W_BR, 1), lambda l, o, s: (l, o, 0, 0))],
        out_specs=pl.BlockSpec((None, None, W_BR, L), lambda l, o, s: (l, o, 0, s)),
        compiler_params=_params(("parallel", "parallel", "parallel"), 48),
        name="hy_filters",
    )(z, w1t, b1, w2t, b2, w3t, delta, bias)


HY_PRE_NB = 8


def _hy_pre_kernel(p_ref, w_ref, b_ref, o_ref):
    L = p_ref.shape[1]
    w = w_ref[...]
    row = lax.broadcasted_iota(jnp.int32, (L, LANES), 0)

    def body(bi, carry):
        x = p_ref[bi].astype(F32)
        xm = jnp.where(row == 0, 0.0, pltpu.roll(x, 1, 0))
        xp = jnp.where(row == L - 1, 0.0, pltpu.roll(x, L - 1, 0))
        u = b_ref[...] + w[0:1] * xm + w[1:2] * x + w[2:3] * xp
        o_ref[:, pl.ds(pl.multiple_of(bi * L, LANES), L)] = u.T.astype(BF16)
        return carry

    lax.fori_loop(0, HY_PRE_NB, body, 0)


def _hy_pre(proj3, conv_w, conv_b):
    B, L, _ = proj3.shape
    nc = 3 * W_BR // LANES
    return pl.pallas_call(
        _hy_pre_kernel,
        out_shape=jax.ShapeDtypeStruct((3 * W_BR, B * L), BF16),
        grid=(B // HY_PRE_NB, nc),
        in_specs=[pl.BlockSpec((HY_PRE_NB, L, LANES), lambda b, c: (b, 0, COL_HY // LANES + c)),
                  pl.BlockSpec((3, LANES), lambda b, c: (0, c)),
                  pl.BlockSpec((1, LANES), lambda b, c: (0, c))],
        out_specs=pl.BlockSpec((LANES, HY_PRE_NB * L), lambda b, c: (c, b)),
        compiler_params=_params(("parallel", "parallel"), 48),
        name="hy_pre",
    )(proj3, conv_w, conv_b)


HY_CB = 8
HY_KT = 2 * LANES
BF16_ROWS = 16


def _hy_build_tables(kk_ref, ci, g0_ref, g1_ref, L):
    nblk = (2 * L - LANES) // LANES
    for m in range(nblk):
        lo = 2 * L - 2 * LANES - LANES * m
        s0 = kk_ref[0, pl.ds(ci, 1), lo:lo + 2 * LANES]
        s1 = kk_ref[1, pl.ds(ci, 1), lo:lo + 2 * LANES]
        b0 = lax.bitcast_convert_type(s0.astype(BF16).astype(F32), jnp.uint32)
        b1 = lax.bitcast_convert_type(s1.astype(BF16).astype(F32), jnp.uint32)
        w = b0 | (b1 >> 16)
        x = jnp.broadcast_to(w, (LANES, 2 * LANES))
        r = pltpu.roll(x, 0, 1, stride=1, stride_axis=0)[:, LANES:]
        rows = slice(LANES * m, LANES * (m + 1))
        g0_ref[rows, :] = lax.bitcast_convert_type(r & jnp.uint32(0xFFFF0000), F32).astype(BF16)
        g1_ref[rows, :] = lax.bitcast_convert_type(r << 16, F32).astype(BF16)


def _hy_toeplitz(u, g_ref, u_ref, acc_ref, L):
    B = u.shape[0]
    nsb = L // HY_KT
    bp = u_ref.shape[0] // nsb
    pad = jnp.zeros((bp - B, HY_KT), F32)
    for sb in range(nsb):
        blk = u[:, HY_KT * sb:HY_KT * (sb + 1)]
        if bp > B:
            blk = jnp.concatenate([blk, pad], axis=0)
        u_ref[bp * sb:bp * (sb + 1), :] = blk.astype(BF16)
    acc_ref[...] = jnp.zeros(acc_ref.shape, F32)
    for delta in range(-(nsb - 1), nsb):
        d0 = HY_KT * delta + L - LANES
        tile = jnp.concatenate([g_ref[d0:d0 + HY_KT, :], g_ref[d0 - LANES:d0 - LANES + HY_KT, :]], axis=1)
        lo, hi = max(0, delta), min(nsb - 1, nsb - 1 + delta)
        part = jnp.dot(u_ref[bp * lo:bp * (hi + 1), :], tile, preferred_element_type=F32)
        acc_ref[bp * (lo - delta):bp * (hi - delta + 1), :] += part
    return jnp.concatenate([acc_ref[bp * tb:bp * tb + B, :] for tb in range(nsb)], axis=1)


def _hy_conv_kernel(z_ref, x1_ref, x2_ref, kk_ref, o_ref, g_ref, u_ref, acc_ref):
    L = z_ref.shape[2]

    def compute(ci, slot):
        y0 = _hy_toeplitz(z_ref[ci].astype(F32), g_ref.at[slot, 0], u_ref, acc_ref, L)
        z1 = x1_ref[ci].astype(F32) * y0
        y1 = _hy_toeplitz(z1, g_ref.at[slot, 1], u_ref, acc_ref, L)
        o_ref[ci] = (x2_ref[ci].astype(F32) * y1).astype(BF16)

    def build(ci, slot):
        _hy_build_tables(kk_ref, ci, g_ref.at[slot, 0], g_ref.at[slot, 1], L)

    build(0, 0)

    def body(k, carry):
        build(2 * k + 1, 1)
        compute(2 * k, 0)
        build(jnp.minimum(2 * k + 2, HY_CB - 1), 0)
        compute(2 * k + 1, 1)
        return carry

    lax.fori_loop(0, HY_CB // 2, body, 0)


def _hy_conv(zc3, kk):
    _, B, L = zc3.shape
    nblk = W_BR // HY_CB
    bp = -(-B // BF16_ROWS) * BF16_ROWS
    nsb = L // HY_KT
    act = lambda off: pl.BlockSpec((HY_CB, B, L), lambda c: (c + off * nblk, 0, 0))
    return pl.pallas_call(
        _hy_conv_kernel,
        out_shape=jax.ShapeDtypeStruct((W_BR, B, L), BF16),
        grid=(nblk,),
        in_specs=[act(0), act(1), act(2),
                  pl.BlockSpec((2, HY_CB, 2 * L), lambda c: (0, c, 0))],
        out_specs=pl.BlockSpec((HY_CB, B, L), lambda c: (c, 0, 0)),
        scratch_shapes=[pltpu.VMEM((2, 2, 2 * L - LANES, LANES), BF16),
                        pltpu.VMEM((nsb * bp, HY_KT), BF16),
                        pltpu.VMEM((nsb * bp, HY_KT), F32)],
        compiler_params=_params(("parallel",), 48),
        name="hy_conv",
    )(zc3, zc3, zc3, kk)


HY_POST_TT = 8192


def _hy_post_kernel(y_ref, o_ref):
    o_ref[...] = y_ref[...].astype(F32).T.astype(BF16)


def _hy_post(yc):
    C, T = yc.shape
    return pl.pallas_call(
        _hy_post_kernel,
        out_shape=jax.ShapeDtypeStruct((T, C), BF16),
        grid=(T // HY_POST_TT, C // LANES),
        in_specs=[pl.BlockSpec((LANES, HY_POST_TT), lambda t, c: (c, t))],
        out_specs=pl.BlockSpec((HY_POST_TT, LANES), lambda t, c: (t, c)),
        compiler_params=_params(("parallel", "parallel"), 48),
        name="hy_post",
    )(yc)


def _sgu_kernel(u_ref, v_ref, g_ref, b_ref, ws_ref, bs_ref, o_ref):
    L = u_ref.shape[0]
    gw = W_BR // SGU_GROUPS

    def body(n, carry):
        rows = pl.ds(pl.multiple_of(n * SGU_CHUNK, SGU_CHUNK), SGU_CHUNK)
        v = v_ref[rows, :].astype(F32)
        mu = jnp.mean(v, axis=-1, keepdims=True)
        d = v - mu
        var = jnp.mean(d * d, axis=-1, keepdims=True)
        vn = (d * lax.rsqrt(var + EPS) * g_ref[...] + b_ref[...]).astype(BF16)
        mixed = jnp.concatenate(
            [jnp.dot(ws_ref[k], vn[:, gw * k:gw * (k + 1)], preferred_element_type=F32) + bs_ref[k]
             for k in range(SGU_GROUPS)], axis=1)
        o_ref[rows, :] = (u_ref[rows, :].astype(F32) * mixed).astype(BF16)
        return carry

    lax.fori_loop(0, L // SGU_CHUNK, body, 0)


def _sgu(proj3, ln_g, ln_b, ws, bs):
    B, L, _ = proj3.shape
    cb = COL_SGU // W_BR
    return pl.pallas_call(
        _sgu_kernel,
        out_shape=jax.ShapeDtypeStruct((B, L, W_BR), BF16),
        grid=(B,),
        in_specs=[pl.BlockSpec((None, L, W_BR), lambda b: (b, 0, cb)),
                  pl.BlockSpec((None, L, W_BR), lambda b: (b, 0, cb + 1)),
                  pl.BlockSpec((1, W_BR), lambda b: (0, 0)),
                  pl.BlockSpec((1, W_BR), lambda b: (0, 0)),
                  pl.BlockSpec((SGU_GROUPS, SGU_CHUNK, SGU_CHUNK), lambda b: (0, 0, 0)),
                  pl.BlockSpec((SGU_GROUPS, SGU_CHUNK, 1), lambda b: (0, 0, 0))],
        out_specs=pl.BlockSpec((None, L, W_BR), lambda b: (b, 0, 0)),
        compiler_params=_params(("parallel",), 48),
        name="sgu",
    )(proj3, proj3, ln_g, ln_b, ws, bs)


RG_SLABS = 2
RG_TL = 256
RG_PAD = 8
RG_SKEW = 8
RG_UNROLL = 8


def _rg_kernel(xb_ref, gb_ref, cw_ref, cb_ref, w_ref, bias_ref, lam_ref, o_ref,
               xpad_ref, nat_ref, xi_ref, af_ref, bf_ref, ab_ref, bb_ref):
    L = xb_ref.shape[0]
    seg = L // SUBLANES
    pitch = seg + RG_SKEW
    zeros = jnp.zeros((RG_PAD, LANES), F32)
    sp_all = []
    for s in range(RG_SLABS):
        cols = slice(s * LANES, (s + 1) * LANES)
        xpad_ref[s, 0:RG_PAD, :] = zeros
        xpad_ref[s, RG_PAD + L:RG_PAD + L + RG_PAD, :] = zeros
        xpad_ref[s, RG_PAD:RG_PAD + L, :] = xb_ref[:, cols].astype(F32)
        cw = cw_ref[:, cols]
        for j in range(SUBLANES):
            t0 = j * seg
            xc = cb_ref[:, cols]
            for k in range(4):
                xc = xc + cw[k:k + 1] * xpad_ref[s, RG_PAD + t0 + k - 2:RG_PAD + t0 + k - 2 + seg, :]
            nat_ref[s, pitch * j:pitch * j + seg, :] = xc
        lam = lam_ref[:, cols]
        sp_all.append(jnp.maximum(-lam, 0.0) + jnp.log(1.0 + jnp.exp(-jnp.abs(lam))))

    def interleave(r, carry):
        dst = pl.ds(pl.multiple_of(r * SUBLANES, SUBLANES), SUBLANES)
        for s in range(RG_SLABS):
            xi_ref[s, dst, :] = nat_ref[s, pl.ds(r, SUBLANES, stride=pitch), :]
        return carry

    lax.fori_loop(0, seg, interleave, 0, unroll=RG_UNROLL)

    for s in range(RG_SLABS):
        for ti in range(L // RG_TL):
            rows = slice(ti * RG_TL, (ti + 1) * RG_TL)
            xc = xi_ref[s, rows, :]
            g = jnp.dot(xc.astype(BF16), w_ref[s], preferred_element_type=F32) + bias_ref[s]
            g = 0.5 * jnp.tanh(0.5 * g) + 0.5
            for d, (a_ref, b_ref) in enumerate(((af_ref, bf_ref), (ab_ref, bb_ref))):
                r = g[:, 2 * d * LANES:(2 * d + 1) * LANES]
                i = g[:, (2 * d + 1) * LANES:(2 * d + 2) * LANES]
                a = jnp.exp((-RG_C) * r * sp_all[s][d:d + 1])
                om = 1.0 - a * a
                a_ref[s, rows, :] = a
                b_ref[s, rows, :] = (om * lax.rsqrt(jnp.maximum(om, 1e-30))) * (i * xc)

    def rows_f(r):
        return pl.ds(pl.multiple_of(r * SUBLANES, SUBLANES), SUBLANES)

    def rows_b(r):
        return pl.ds(pl.multiple_of((seg - 1 - r) * SUBLANES, SUBLANES), SUBLANES)

    def pass1(r, carry):
        out = []
        for s in range(RG_SLABS):
            hf, pf, hb, pb = carry[4 * s:4 * s + 4]
            a = af_ref[s, rows_f(r), :]
            hf = a * hf + bf_ref[s, rows_f(r), :]
            pf = a * pf
            a = ab_ref[s, rows_b(r), :]
            hb = a * hb + bb_ref[s, rows_b(r), :]
            pb = a * pb
            out += [hf, pf, hb, pb]
        return tuple(out)

    z = jnp.zeros((SUBLANES, LANES), F32)
    one = jnp.ones((SUBLANES, LANES), F32)
    ends = lax.fori_loop(0, seg, pass1, (z, one, z, one) * RG_SLABS, unroll=RG_UNROLL)

    init = []
    for s in range(RG_SLABS):
        hf, pf, hb, pb = ends[4 * s:4 * s + 4]
        c = jnp.zeros((1, LANES), F32)
        rows = []
        for j in range(SUBLANES):
            rows.append(c)
            c = pf[j:j + 1] * c + hf[j:j + 1]
        init.append(jnp.concatenate(rows, axis=0))
        c = jnp.zeros((1, LANES), F32)
        rows = []
        for j in range(SUBLANES - 1, -1, -1):
            rows.append(c)
            c = pb[j:j + 1] * c + hb[j:j + 1]
        init.append(jnp.concatenate(rows[::-1], axis=0))

    def pass2(r, carry):
        out = []
        for s in range(RG_SLABS):
            hf, hb = carry[2 * s:2 * s + 2]
            hf = af_ref[s, rows_f(r), :] * hf + bf_ref[s, rows_f(r), :]
            xi_ref[s, rows_f(r), :] = hf
            hb = ab_ref[s, rows_b(r), :] * hb + bb_ref[s, rows_b(r), :]
            xpad_ref[s, rows_b(r), :] = hb
            out += [hf, hb]
        return tuple(out)

    lax.fori_loop(0, seg, pass2, tuple(init), unroll=RG_UNROLL)

    def deinterleave(r, carry):
        src = rows_f(r)
        for s in range(RG_SLABS):
            nat_ref[s, pl.ds(r, SUBLANES, stride=pitch), :] = xi_ref[s, src, :] + xpad_ref[s, src, :]
        return carry

    lax.fori_loop(0, seg, deinterleave, 0, unroll=RG_UNROLL)

    for s in range(RG_SLABS):
        cols = slice(s * LANES, (s + 1) * LANES)
        for j in range(SUBLANES):
            gb = gb_ref[j * seg:(j + 1) * seg, cols].astype(F32)
            gelu = 0.5 * gb * (1.0 + jnp.tanh(math.sqrt(2.0 / math.pi) * (gb + 0.044715 * (gb * gb * gb))))
            o_ref[j * seg:(j + 1) * seg, cols] = (nat_ref[s, pitch * j:pitch * j + seg, :] * gelu).astype(BF16)


def _rglru(proj3, conv_w, conv_b, w_bd, bias_bd, lam):
    B, L, _ = proj3.shape
    gw = RG_SLABS * LANES
    ng = W_BR // gw
    cx = COL_RG // gw
    seg = L // SUBLANES
    slab = lambda rows: pltpu.VMEM((RG_SLABS, rows, LANES), F32)
    return pl.pallas_call(
        _rg_kernel,
        out_shape=jax.ShapeDtypeStruct((B, L, W_BR), BF16),
        grid=(B, ng),
        in_specs=[pl.BlockSpec((None, L, gw), lambda b, c: (b, 0, cx + c)),
                  pl.BlockSpec((None, L, gw), lambda b, c: (b, 0, cx + ng + c)),
                  pl.BlockSpec((4, gw), lambda b, c: (0, c)),
                  pl.BlockSpec((1, gw), lambda b, c: (0, c)),
                  pl.BlockSpec((RG_SLABS, LANES, 4 * LANES), lambda b, c: (c, 0, 0)),
                  pl.BlockSpec((RG_SLABS, 1, 4 * LANES), lambda b, c: (c, 0, 0)),
                  pl.BlockSpec((2, gw), lambda b, c: (0, c))],
        out_specs=pl.BlockSpec((None, L, gw), lambda b, c: (b, 0, c)),
        scratch_shapes=[slab(L + 2 * RG_PAD), slab(SUBLANES * (seg + RG_SKEW))] + [slab(L)] * 5,
        compiler_params=_params(("parallel", "parallel"), 48),
        name="rglru",
    )(proj3, proj3, conv_w, conv_b, w_bd, bias_bd, lam)


def _log_sigmoid(x):
    return jnp.minimum(x, 0.0) - jnp.log(1.0 + jnp.exp(-jnp.abs(x)))


def _ml_kernel(q_ref, kt_ref, v_ref, o_ref, gt_ref, gtt_ref, gbias_ref, gbiast_ref, ng_ref, y_ref,
               hf_ref, hb_ref, ct_ref, m_ref):
    L = q_ref.shape[0]
    nc = L // ML_CHUNK
    row = lax.broadcasted_iota(jnp.int32, (ML_CHUNK, ML_CHUNK), 0)
    col = lax.broadcasted_iota(jnp.int32, (ML_CHUNK, ML_CHUNK), 1)
    gate_lane = lax.broadcasted_iota(jnp.int32, (ML_CHUNK, LANES), 1)
    gate_row = lax.broadcasted_iota(jnp.int32, (LANES, ML_CHUNK), 0)
    is_f = (gate_lane >= 2 * ML_HEADS) & (gate_lane < 4 * ML_HEADS)
    is_f_t = (gate_row >= 2 * ML_HEADS) & (gate_row < 4 * ML_HEADS)
    lane0 = gate_lane == 0
    ones_blk = jnp.where(lane0, 1.0, 0.0).astype(BF16)
    scale = ML_HD ** -0.5
    masks = (col <= row, col >= row)
    tris = tuple(jnp.where(mk, 1.0, 0.0) for mk in masks)
    ct_ref[...] = jnp.zeros(ct_ref.shape, F32)
    m_ref[...] = jnp.zeros(m_ref.shape, F32)

    def body(step, carry):
        for d in range(2):
            c = step if d == 0 else nc - 1 - step
            rows = pl.ds(pl.multiple_of(c * ML_CHUNK, ML_CHUNK), ML_CHUNK)
            mask = masks[d]
            pre = gt_ref[rows, :] + gbias_ref[...]
            pre_t = gtt_ref[:, rows] + gbiast_ref[...]
            cum = jnp.dot(tris[d], jnp.where(is_f, _log_sigmoid(pre), 0.0),
                          precision=HIGHEST, preferred_element_type=F32)
            cum_t = jnp.dot(jnp.where(is_f_t, _log_sigmoid(pre_t), 0.0), tris[1 - d],
                            precision=HIGHEST, preferred_element_type=F32)
            last = ML_CHUNK - 1 if d == 0 else 0
            for hd in range(ML_HEADS):
                li_lane = d * ML_HEADS + hd
                lf_lane = 2 * ML_HEADS + d * ML_HEADS + hd
                b_col = cum[:, lf_lane:lf_lane + 1]
                b_row = cum_t[lf_lane:lf_lane + 1, :]
                li_col = pre[:, li_lane:li_lane + 1]
                li_row = pre_t[li_lane:li_lane + 1, :]
                g_tot = b_col[last:last + 1, :]
                m_prev = m_ref[d, hd]
                dlog = jnp.where(mask, b_col - b_row + li_row, -jnp.inf)
                inter = b_col + m_prev
                m_t = jnp.maximum(inter, jnp.max(dlog, axis=1, keepdims=True))
                w_intra = jnp.exp(dlog - m_t)
                w_inter = jnp.exp(inter - m_t)
                hs = slice(hd * ML_HD, (hd + 1) * ML_HD)
                qh = q_ref[rows, hs]
                kt = kt_ref[hs, rows]
                vh = v_ref[rows, hs]
                s = (jnp.dot(qh, kt, preferred_element_type=F32) * (scale * w_intra)).astype(BF16)
                sv = jnp.dot(s, jnp.concatenate([vh, ones_blk], axis=1), preferred_element_type=F32)
                cta = ct_ref[d, hd]
                qc = jnp.dot(qh, cta.astype(BF16), preferred_element_type=F32) * scale
                num = w_inter * qc[:, 0:ML_HD] + sv[:, 0:ML_HD]
                den = w_inter * qc[:, ML_HD:ML_HD + 1] + sv[:, ML_HD:ML_HD + 1]
                hout = num * (1.0 / jnp.maximum(jnp.abs(den), jnp.exp(-m_t)))
                if d == 0:
                    hf_ref[rows, hs] = hout
                else:
                    hb_ref[rows, hs] = hout
                wlog_col = g_tot - b_col + li_col
                m_new = jnp.maximum(g_tot + m_prev, jnp.max(wlog_col, axis=0, keepdims=True))
                decay = jnp.exp(g_tot + m_prev - m_new)
                ws_col = jnp.exp(wlog_col - m_new)
                vw = jnp.concatenate([(vh.astype(F32) * ws_col).astype(BF16),
                                      jnp.where(lane0, ws_col, 0.0).astype(BF16)], axis=1)
                ct_ref[d, hd] = decay * cta + jnp.dot(kt, vw, preferred_element_type=F32)
                m_ref[d, hd] = m_new
        return carry

    lax.fori_loop(0, nc, body, 0)

    for hd in range(ML_HEADS):
        hs = slice(hd * ML_HD, (hd + 1) * ML_HD)
        h = hf_ref[:, hs] + hb_ref[:, hs]
        hn = h * lax.rsqrt(jnp.mean(h * h, axis=-1, keepdims=True) + EPS) * ng_ref[:, hs]
        y_ref[:, hs] = (_sigmoid(o_ref[:, hs].astype(F32)) * hn).astype(BF16)


def _mlstm(proj3, kt, mlg3, mlgt, gbias, norm_g):
    B, L, _ = proj3.shape
    cb = COL_ML // W_BR
    col = lambda j: pl.BlockSpec((None, L, W_BR), lambda b: (b, 0, cb + j))
    return pl.pallas_call(
        _ml_kernel,
        out_shape=jax.ShapeDtypeStruct((B, L, W_BR), BF16),
        grid=(B,),
        in_specs=[col(0),
                  pl.BlockSpec((W_BR, L), lambda b: (0, b)),
                  col(1), col(2),
                  pl.BlockSpec((None, L, LANES), lambda b: (b, 0, 0)),
                  pl.BlockSpec((LANES, L), lambda b: (0, b)),
                  pl.BlockSpec((1, LANES), lambda b: (0, 0)),
                  pl.BlockSpec((LANES, 1), lambda b: (0, 0)),
                  pl.BlockSpec((1, W_BR), lambda b: (0, 0))],
        out_specs=pl.BlockSpec((None, L, W_BR), lambda b: (b, 0, 0)),
        scratch_shapes=[pltpu.VMEM((L, W_BR), F32),
                        pltpu.VMEM((L, W_BR), F32),
                        pltpu.VMEM((2, ML_HEADS, ML_HD, 2 * ML_HD), F32),
                        pltpu.VMEM((2, ML_HEADS, 1, 1), F32)],
        compiler_params=_params(("parallel",), 48),
        name="mlstm",
    )(proj3, kt, proj3, proj3, mlg3, mlgt, gbias, gbias.reshape(LANES, 1), norm_g)


def _merge_kernel(route, gate_ref, ya_ref, yb_ref, yc_ref, yd_ref, wb_ref, wo_ref, x_ref, g_ref, *rest):
    if route:
        rw_ref, rb_ref, xo_ref, hn_ref, comb_ref, combt_ref = rest
    else:
        xo_ref, hn_ref = rest
    merged = None
    for k, y_ref in enumerate((ya_ref, yb_ref, yc_ref, yd_ref)):
        t = jnp.dot(y_ref[...], wb_ref[k], preferred_element_type=F32)
        gk = _sigmoid(gate_ref[:, k * D_MODEL:(k + 1) * D_MODEL].astype(F32))
        merged = gk * t if merged is None else merged + gk * t
    xn = x_ref[...] + jnp.dot(merged.astype(BF16), wo_ref[...], preferred_element_type=F32)
    xo_ref[...] = xn
    h = _rms(xn, g_ref[...])
    h_hi = h.astype(BF16)
    hn_ref[...] = h_hi
    if route:
        h_lo = (h - h_hi.astype(F32)).astype(BF16)
        r_hi = jnp.dot(h_hi, rw_ref[...], preferred_element_type=F32)
        r_lo = jnp.dot(h_lo, rw_ref[:, 0:LANES], preferred_element_type=F32)
        logits = r_hi[:, 0:LANES] + r_hi[:, LANES:2 * LANES] + r_lo + rb_ref[...]
        lane = lax.broadcasted_iota(jnp.int32, logits.shape, 1)
        logits = jnp.where(lane < N_EXPERTS, logits, -jnp.inf)
        v1 = jnp.max(logits, axis=1, keepdims=True)
        i1 = jnp.min(jnp.where(logits == v1, lane, LANES), axis=1, keepdims=True)
        rest_l = jnp.where(lane == i1, -jnp.inf, logits)
        v2 = jnp.max(rest_l, axis=1, keepdims=True)
        i2 = jnp.min(jnp.where(rest_l == v2, lane, LANES), axis=1, keepdims=True)
        e2 = jnp.exp(v2 - v1)
        p1 = 1.0 / (1.0 + e2)
        comb = jnp.where(lane == i1, p1, jnp.where(lane == i2, e2 * p1, 0.0))
        comb_ref[...] = comb
        combt_ref[...] = comb.T


def _merge(proj, ya, yb, yc, yd, wb, wo, x, g, router=None, tm=512):
    T = x.shape[0]
    row = lambda w: pl.BlockSpec((tm, w), lambda i: (i, 0))
    in_specs = [row(N_BRANCH * D_MODEL), row(W_BR), row(W_BR), row(W_BR), row(W_BR),
                pl.BlockSpec((N_BRANCH, W_BR, D_MODEL), lambda i: (0, 0, 0)),
                pl.BlockSpec((D_MODEL, D_MODEL), lambda i: (0, 0)),
                row(D_MODEL),
                pl.BlockSpec((1, D_MODEL), lambda i: (0, 0))]
    out_shape = [jax.ShapeDtypeStruct((T, D_MODEL), F32), jax.ShapeDtypeStruct((T, D_MODEL), BF16)]
    out_specs = [row(D_MODEL), row(D_MODEL)]
    args = [proj, ya, yb, yc, yd, wb, wo, x, g]
    if router is not None:
        in_specs += [pl.BlockSpec((D_MODEL, 2 * LANES), lambda i: (0, 0)),
                     pl.BlockSpec((1, LANES), lambda i: (0, 0))]
        out_shape += [jax.ShapeDtypeStruct((T, LANES), F32), jax.ShapeDtypeStruct((LANES, T), F32)]
        out_specs += [row(LANES), pl.BlockSpec((LANES, tm), lambda i: (0, i))]
        args += list(router)
    return pl.pallas_call(
        functools.partial(_merge_kernel, router is not None),
        out_shape=tuple(out_shape),
        grid=(T // tm,),
        in_specs=in_specs,
        out_specs=tuple(out_specs),
        compiler_params=_params(("parallel",), 56),
        name="merge_route" if router is not None else "merge",
    )(*args)


def _swiglu_acc(h, w1_ref, w3_ref, w2_ref):
    a = jnp.dot(h, w1_ref[...], preferred_element_type=F32)
    b = jnp.dot(h, w3_ref[...], preferred_element_type=F32)
    act = (a * _sigmoid(a) * b).astype(BF16)
    return jnp.dot(act, w2_ref[...], preferred_element_type=F32)


def _ffn_kernel(h_ref, x_ref, w1_ref, w3_ref, w2_ref, o_ref, acc_ref):
    f = pl.program_id(1)

    @pl.when(f == 0)
    def _():
        acc_ref[...] = x_ref[...]

    acc_ref[...] += _swiglu_acc(h_ref[...], w1_ref, w3_ref, w2_ref)

    @pl.when(f == pl.num_programs(1) - 1)
    def _():
        o_ref[...] = acc_ref[...]


def _ffn(hn, x, w1, w3, w2, tm=512, n_split=2):
    T = x.shape[0]
    dff = w1.shape[1]
    tf = dff // n_split
    return pl.pallas_call(
        _ffn_kernel,
        out_shape=jax.ShapeDtypeStruct((T, D_MODEL), F32),
        grid=(T // tm, n_split),
        in_specs=[pl.BlockSpec((tm, D_MODEL), lambda i, f: (i, 0)),
                  pl.BlockSpec((tm, D_MODEL), lambda i, f: (i, 0)),
                  pl.BlockSpec((D_MODEL, tf), lambda i, f: (0, f)),
                  pl.BlockSpec((D_MODEL, tf), lambda i, f: (0, f)),
                  pl.BlockSpec((tf, D_MODEL), lambda i, f: (f, 0))],
        out_specs=pl.BlockSpec((tm, D_MODEL), lambda i, f: (i, 0)),
        scratch_shapes=[pltpu.VMEM((tm, D_MODEL), F32)],
        compiler_params=_params(("parallel", "arbitrary"), 56),
        name="ffn",
    )(hn, x, w1, w3, w2)


MOE_TM = 1024
MOE_CH = 256


def _moe_kernel(cnt_ref, h_ref, x_ref, comb_ref, combt_ref, w1_ref, w3_ref, w2_ref, o_ref,
                rank_ref, rankt_ref):
    i = pl.program_id(0)
    e = pl.program_id(1)
    tm = h_ref.shape[0]

    @pl.when(e == 0)
    def _():
        o_ref[...] = x_ref[...]
        r = lax.broadcasted_iota(jnp.int32, (tm, tm), 0)
        c = lax.broadcasted_iota(jnp.int32, (tm, tm), 1)
        before = jnp.where(c < r, 1.0, 0.0).astype(BF16)
        sel = jnp.where(comb_ref[...] > 0.0, 1.0, 0.0).astype(BF16)
        rank_ref[...] = jnp.dot(before, sel, preferred_element_type=F32)
        selt = jnp.where(combt_ref[...] > 0.0, 1.0, 0.0).astype(BF16)
        rankt_ref[...] = lax.dot_general(selt, before, (((1,), (1,)), ((), ())),
                                         preferred_element_type=F32)

    lane = lax.broadcasted_iota(jnp.int32, (tm, LANES), 1)
    comb = comb_ref[...]
    c_col = jnp.sum(jnp.where(lane == e, comb, 0.0), axis=1, keepdims=True)
    rank_col = jnp.sum(jnp.where(lane == e, rank_ref[...], 0.0), axis=1, keepdims=True)
    rank_col = jnp.where(c_col > 0.0, rank_col, -1.0)
    c_row = combt_ref[pl.ds(e, 1), :]
    rank_row = jnp.where(c_row > 0.0, rankt_ref[pl.ds(e, 1), :], -1.0)
    cnt = cnt_ref[i * N_EXPERTS + e]

    def run_chunk(base, size):
        pos_r = (lax.broadcasted_iota(jnp.int32, (size, tm), 0) + base).astype(F32)
        gather = jnp.where(rank_row == pos_r, 1.0, 0.0).astype(BF16)
        xs = jnp.dot(gather, h_ref[...], preferred_element_type=F32).astype(BF16)
        y = _swiglu_acc(xs, w1_ref, w3_ref, w2_ref).astype(BF16)
        pos_c = (lax.broadcasted_iota(jnp.int32, (tm, size), 1) + base).astype(F32)
        scatter = jnp.where(rank_col == pos_c, c_col, 0.0).astype(BF16)
        o_ref[...] += jnp.dot(scatter, y, preferred_element_type=F32)

    n_full = cnt // MOE_CH
    rem = cnt - n_full * MOE_CH

    def body(j, carry):
        run_chunk(j * MOE_CH, MOE_CH)
        return carry

    lax.fori_loop(0, n_full, body, 0)

    @pl.when(rem > MOE_CH // 2)
    def _():
        run_chunk(n_full * MOE_CH, MOE_CH)

    @pl.when((rem > 0) & (rem <= MOE_CH // 2))
    def _():
        run_chunk(n_full * MOE_CH, MOE_CH // 2)


def _moe(hn, x, comb, combt, w1, w3, w2, tm=MOE_TM):
    T = x.shape[0]
    dfe = w1.shape[2]
    nt = T // tm
    cnt = jnp.sum((comb[:, :N_EXPERTS] > 0.0).reshape(nt, tm, N_EXPERTS), axis=1, dtype=jnp.int32).reshape(-1)
    grid_spec = pltpu.PrefetchScalarGridSpec(
        num_scalar_prefetch=1,
        grid=(nt, N_EXPERTS),
        in_specs=[pl.BlockSpec((tm, D_MODEL), lambda i, e, c: (i, 0)),
                  pl.BlockSpec((tm, D_MODEL), lambda i, e, c: (i, 0)),
                  pl.BlockSpec((tm, LANES), lambda i, e, c: (i, 0)),
                  pl.BlockSpec((LANES, tm), lambda i, e, c: (0, i)),
                  pl.BlockSpec((None, D_MODEL, dfe), lambda i, e, c: (e, 0, 0)),
                  pl.BlockSpec((None, D_MODEL, dfe), lambda i, e, c: (e, 0, 0)),
                  pl.BlockSpec((None, dfe, D_MODEL), lambda i, e, c: (e, 0, 0))],
        out_specs=pl.BlockSpec((tm, D_MODEL), lambda i, e, c: (i, 0)),
        scratch_shapes=[pltpu.VMEM((tm, LANES), F32), pltpu.VMEM((LANES, tm), F32)])
    return pl.pallas_call(
        _moe_kernel,
        out_shape=jax.ShapeDtypeStruct((T, D_MODEL), F32),
        grid_spec=grid_spec,
        compiler_params=_params(("parallel", "arbitrary"), 56),
        name="moe",
    )(cnt, hn, x, comb, combt, w1, w3, w2)


def _final_norm_kernel(x_ref, g_ref, o_ref):
    o_ref[...] = _rms(x_ref[...], g_ref[...])


def _final_norm(x, g, tm=1024):
    T = x.shape[0]
    return pl.pallas_call(
        _final_norm_kernel,
        out_shape=jax.ShapeDtypeStruct((T, D_MODEL), F32),
        grid=(T // tm,),
        in_specs=[pl.BlockSpec((tm, D_MODEL), lambda i: (i, 0)),
                  pl.BlockSpec((1, D_MODEL), lambda i: (0, 0))],
        out_specs=pl.BlockSpec((tm, D_MODEL), lambda i: (i, 0)),
        compiler_params=_params(("parallel",), 48),
        name="final_norm",
    )(x, g)


def _split_w_in(w_in_l):
    n_hy, n_sgu, n_rg = 3 * W_BR, 2 * W_BR, 2 * W_BR
    o1 = n_hy
    o2 = o1 + n_sgu
    o3 = o2 + n_rg
    o4 = o3 + 4 * W_BR
    o5 = o4 + 4 * ML_HEADS
    w = jnp.concatenate([w_in_l[:, o5:], w_in_l[:, :o3 + W_BR], w_in_l[:, o3 + 2 * W_BR:o4]],
                        axis=1).astype(BF16)
    wg = jnp.pad(w_in_l[:, o4:o5], ((0, 0), (0, LANES - 4 * ML_HEADS))).astype(BF16)
    wkt = w_in_l[:, o3 + W_BR:o3 + 2 * W_BR].T.astype(BF16)
    return w, wg, wkt


def _router_operands(router_w, router_b):
    w = jnp.pad(router_w, ((0, 0), (0, LANES - N_EXPERTS)))
    w_hi = w.astype(BF16)
    w_lo = (w - w_hi.astype(F32)).astype(BF16)
    return (jnp.concatenate([w_hi, w_lo], axis=1),
            jnp.pad(router_b, (0, LANES - N_EXPERTS))[None, :])


def _rg_block_diag(wa, ba, wx, bx):
    hpg = LANES // RG_HD
    ng = RG_HEADS // hpg
    eye = jnp.eye(hpg, dtype=F32)

    def bd(w):
        w = w.reshape(ng, hpg, RG_HD, RG_HD)
        return jnp.einsum('gaij,ab->gaibj', w, eye).reshape(ng, LANES, LANES)

    w = jnp.concatenate([bd(wa[0]), bd(wx[0]), bd(wa[1]), bd(wx[1])], axis=2).astype(BF16)
    fl = lambda b: b.reshape(ng, 1, LANES)
    bias = jnp.concatenate([fl(ba[0]), fl(bx[0]), fl(ba[1]), fl(bx[1])], axis=2)
    return w, bias


def _token_mixer(x, B, L, l, norm_g, w_in, kk, hy_conv_w, hy_conv_b, sgu_ln_g, sgu_ln_b, sgu_ws, sgu_bs,
                 rg_conv_w, rg_conv_b, rg_wa, rg_ba, rg_wx, rg_bx, rg_lambda, ml_i_bias, ml_f_bias,
                 ml_norm_g, w_branch, w_out, next_g, router):
    T = B * L
    w, wg, wkt = _split_w_in(w_in[l])
    proj, mlg, mlgt, kt = _inproj(x, norm_g[l][None, :], w, wg, wkt)
    proj3 = proj.reshape(B, L, N_PROJ)

    zc = _hy_pre(proj3, hy_conv_w[l], hy_conv_b[l][None, :])
    ya_c = _hy_conv(zc.reshape(3 * W_BR, B, L), kk[l])
    ya = _hy_post(ya_c.reshape(W_BR, T))

    yb = _sgu(proj3, sgu_ln_g[l][None, :], sgu_ln_b[l][None, :], sgu_ws[l].astype(BF16),
              sgu_bs[l][:, :, None]).reshape(T, W_BR)

    w_bd, bias_bd = _rg_block_diag(rg_wa[l], rg_ba[l], rg_wx[l], rg_bx[l])
    yc = _rglru(proj3, rg_conv_w[l], rg_conv_b[l][None, :], w_bd, bias_bd, rg_lambda[l]).reshape(T, W_BR)

    gbias = jnp.pad(jnp.concatenate([ml_i_bias[l].reshape(-1), ml_f_bias[l].reshape(-1)]),
                    (0, LANES - 4 * ML_HEADS))[None, :]
    yd = _mlstm(proj3, kt, mlg.reshape(B, L, LANES), mlgt, gbias, ml_norm_g[l][None, :]).reshape(T, W_BR)

    return _merge(proj, ya, yb, yc, yd, w_branch[l].astype(BF16), w_out[l].astype(BF16), x,
                  next_g[None, :], router)


def kernel(x_prompt, x_sample, norm_mix_g, w_in, hy_conv_w, hy_conv_b, hy_ffn_w1, hy_ffn_b1, hy_ffn_w2, hy_ffn_b2, hy_ffn_w3, hy_bias, sgu_ln_g, sgu_ln_b, sgu_ws, sgu_bs, rg_conv_w, rg_conv_b, rg_wa, rg_ba, rg_wx, rg_bx, rg_lambda, ml_i_bias, ml_f_bias, ml_norm_g, w_branch, w_out, norm_ffn_g, ffn_w1, ffn_w3, ffn_w2, router_w, router_b, moe_w1, moe_w3, moe_w2, norm_final_g):
    bp, L, _ = x_prompt.shape
    bs = x_sample.shape[0]
    B = bp + bs
    depth = w_in.shape[0]
    x = jnp.concatenate([x_prompt, x_sample], axis=0).reshape(B * L, D_MODEL)
    kk = _hy_filters(hy_ffn_w1, hy_ffn_b1, hy_ffn_w2, hy_ffn_b2, hy_ffn_w3, hy_bias, L)
    for l in range(depth):
        j = l // 2
        router = None
        if l % 2 == 1:
            router = _router_operands(router_w[j], router_b[j])
        outs = _token_mixer(x, B, L, l, norm_mix_g, w_in, kk, hy_conv_w, hy_conv_b, sgu_ln_g, sgu_ln_b,
                            sgu_ws, sgu_bs, rg_conv_w, rg_conv_b, rg_wa, rg_ba, rg_wx, rg_bx, rg_lambda,
                            ml_i_bias, ml_f_bias, ml_norm_g, w_branch, w_out, norm_ffn_g[l], router)
        if router is None:
            x, hn = outs
            x = _ffn(hn, x, ffn_w1[j].astype(BF16), ffn_w3[j].astype(BF16), ffn_w2[j].astype(BF16))
        else:
            x, hn, comb, combt = outs
            x = _moe(hn, x, comb, combt, moe_w1[j].astype(BF16), moe_w3[j].astype(BF16),
                     moe_w2[j].astype(BF16))
    y = _final_norm(x, norm_final_g[None, :]).reshape(B, L, D_MODEL)
    return (y[:bp], y[bp:])
```

```python
import functools
import math

import jax
import jax.numpy as jnp
import numpy as np
from jax import lax
from jax.experimental import pallas as pl
from jax.experimental.pallas import tpu as pltpu

F32 = jnp.float32
BF16 = jnp.bfloat16
EPS = 1e-6
HIGHEST = lax.Precision.HIGHEST

LANES = 128
SUBLANES = 8
VMEM_BYTES_V7X = 64 * 1024 * 1024

D_MODEL = 1024
W_BR = 512
N_BRANCH = 4
HY_BANDS = 16
HY_EMB = 1 + 2 * HY_BANDS
HY_FFN = 64
HY_TARGET = 1e-2
HY_MIN_DECAY = -math.log(HY_TARGET) / 1.5
HY_MAX_DECAY = -math.log(HY_TARGET) / 0.3
SGU_CHUNK = 128
SGU_GROUPS = 4
RG_HEADS = 8
RG_HD = W_BR // RG_HEADS
RG_C = 8.0
ML_HEADS = 4
ML_HD = W_BR // ML_HEADS
ML_CHUNK = 128
N_EXPERTS = 8

N_PROJ = N_BRANCH * D_MODEL + 3 * W_BR + 2 * W_BR + 2 * W_BR + W_BR
COL_GATE = 0
COL_HY = N_BRANCH * D_MODEL
COL_SGU = COL_HY + 3 * W_BR
COL_RG = COL_SGU + 2 * W_BR
COL_ML = COL_RG + 2 * W_BR


def _params(semantics, vmem_mb):
    return pltpu.CompilerParams(dimension_semantics=semantics,
                                vmem_limit_bytes=vmem_mb * 1024 * 1024)


def _sigmoid(x):
    return 1.0 / (1.0 + jnp.exp(-x))


def _rms(x, g):
    return x * lax.rsqrt(jnp.mean(x * x, axis=-1, keepdims=True) + EPS) * g


_NT = (((1,), (1,)), ((), ()))


def _inproj_kernel(x_ref, g_ref, w_ref, wg_ref, wgt_ref, wft_ref, proj_ref, mlg_ref, mlgt_ref, ft_ref, h_ref):
    @pl.when(pl.program_id(1) == 0)
    def _():
        h = _rms(x_ref[...], g_ref[...]).astype(BF16)
        h_ref[...] = h
        mlg_ref[...] = jnp.dot(h, wg_ref[...], preferred_element_type=F32)
        mlgt_ref[...] = lax.dot_general(wgt_ref[...], h, _NT, preferred_element_type=F32)
        ft_ref[...] = lax.dot_general(wft_ref[...], h, _NT, preferred_element_type=F32).astype(BF16)

    proj_ref[...] = jnp.dot(h_ref[...], w_ref[...], preferred_element_type=F32).astype(BF16)


def _inproj(x, g, w, wg, wft, tm=1024, n_split=4):
    T = x.shape[0]
    tn = N_PROJ // n_split
    nf = wft.shape[0]
    return pl.pallas_call(
        _inproj_kernel,
        out_shape=(jax.ShapeDtypeStruct((T, N_PROJ), BF16),
                   jax.ShapeDtypeStruct((T, LANES), F32),
                   jax.ShapeDtypeStruct((LANES, T), F32),
                   jax.ShapeDtypeStruct((nf, T), BF16)),
        grid=(T // tm, n_split),
        in_specs=[pl.BlockSpec((tm, D_MODEL), lambda i, j: (i, 0)),
                  pl.BlockSpec((1, D_MODEL), lambda i, j: (0, 0)),
                  pl.BlockSpec((D_MODEL, tn), lambda i, j: (0, j)),
                  pl.BlockSpec((D_MODEL, LANES), lambda i, j: (0, 0)),
                  pl.BlockSpec((LANES, D_MODEL), lambda i, j: (0, 0)),
                  pl.BlockSpec((nf, D_MODEL), lambda i, j: (0, 0))],
        out_specs=(pl.BlockSpec((tm, tn), lambda i, j: (i, j)),
                   pl.BlockSpec((tm, LANES), lambda i, j: (i, 0)),
                   pl.BlockSpec((LANES, tm), lambda i, j: (0, i)),
                   pl.BlockSpec((nf, tm), lambda i, j: (0, i))),
        scratch_shapes=[pltpu.VMEM((tm, D_MODEL), BF16)],
        compiler_params=_params(("parallel", "arbitrary"), 56),
        name="inproj",
    )(x, g, w, wg, wg.T, wft)


def _hy_filter_kernel(z_ref, w1_ref, b1_ref, w2_ref, b2_ref, w3_ref, delta_ref, bias_ref, kk_ref):
    half = pl.program_id(2)
    z = z_ref[...]
    h = jnp.sin(jnp.dot(w1_ref[...], z, precision=HIGHEST, preferred_element_type=F32) + b1_ref[...])
    h = jnp.sin(jnp.dot(w2_ref[...], h, precision=HIGHEST, preferred_element_type=F32) + b2_ref[...])
    f = jnp.dot(w3_ref[...], h, precision=HIGHEST, preferred_element_type=F32)
    t_norm = z[0:1, :]
    f = f * jnp.exp(-t_norm * delta_ref[...])
    lane = lax.broadcasted_iota(jnp.int32, f.shape, 1)
    first = lane == 0
    f = jnp.where(first, jnp.where(half == 0, 0.0, f + bias_ref[...]), f)
    kk_ref[...] = f


def _hy_filters(hy_ffn_w1, hy_ffn_b1, hy_ffn_w2, hy_ffn_b2, hy_ffn_w3, hy_bias, L):
    depth = hy_ffn_w1.shape[0]
    lag = np.stack([L - np.arange(L), np.arange(L)]).astype(np.float64)
    bands = np.arange(1, HY_BANDS + 1, dtype=np.float64)
    ang = (2.0 * math.pi / L) * lag[:, None, :] * bands[None, :, None]
    z = np.concatenate([lag[:, None, :] / L, np.cos(ang), np.sin(ang)], axis=1)
    z = np.pad(z, ((0, 0), (0, LANES - HY_EMB), (0, 0))).astype(np.float32)
    z = jnp.asarray(z)
    w1t = jnp.pad(jnp.swapaxes(hy_ffn_w1, 1, 2), ((0, 0), (0, 0), (0, LANES - HY_EMB)))
    w2t = jnp.swapaxes(hy_ffn_w2, 1, 2)
    w3t = jnp.swapaxes(hy_ffn_w3, 1, 2).reshape(depth, 4, W_BR, HY_FFN)
    b1 = hy_ffn_b1[:, :, None]
    b2 = hy_ffn_b2[:, :, None]
    delta = jnp.linspace(HY_MIN_DECAY, HY_MAX_DECAY, W_BR, dtype=F32)[:, None]
    bias = hy_bias[:, :, :, None]
    return pl.pallas_call(
        _hy_filter_kernel,
        out_shape=jax.ShapeDtypeStruct((depth, 2, W_BR, 2 * L), F32),
        grid=(depth, 2, 2),
        in_specs=[pl.BlockSpec((None, LANES, L), lambda l, o, s: (s, 0, 0)),
                  pl.BlockSpec((None, HY_FFN, LANES), lambda l, o, s: (l, 0, 0)),
                  pl.BlockSpec((None, HY_FFN, 1), lambda l, o, s: (l, 0, 0)),
                  pl.BlockSpec((None, HY_FFN, HY_FFN), lambda l, o, s: (l, 0, 0)),
                  pl.BlockSpec((None, HY_FFN, 1), lambda l, o, s: (l, 0, 0)),
                  pl.BlockSpec((None, None, W_BR, HY_FFN), lambda l, o, s: (l, 2 * o + 1 - s, 0, 0)),
                  pl.BlockSpec((W_BR, 1), lambda l, o, s: (0, 0)),
                  pl.BlockSpec((None, None, W_BR, 1), lambda l, o, s: (l, o, 0, 0))],
        out_specs=pl.BlockSpec((None, None, W_BR, L), lambda l, o, s: (l, o, 0, s)),
        compiler_params=_params(("parallel", "parallel", "parallel"), 48),
        name="hy_filters",
    )(z, w1t, b1, w2t, b2, w3t, delta, bias)


HY_PRE_NB = 8


def _hy_pre_kernel(p_ref, w_ref, b_ref, o_ref):
    L = p_ref.shape[1]
    w = w_ref[...]
    row = lax.broadcasted_iota(jnp.int32, (L, LANES), 0)

    def body(bi, carry):
        x = p_ref[bi].astype(F32)
        xm = jnp.where(row == 0, 0.0, pltpu.roll(x, 1, 0))
        xp = jnp.where(row == L - 1, 0.0, pltpu.roll(x, L - 1, 0))
        u = b_ref[...] + w[0:1] * xm + w[1:2] * x + w[2:3] * xp
        o_ref[:, pl.ds(pl.multiple_of(bi * L, LANES), L)] = u.T.astype(BF16)
        return carry

    lax.fori_loop(0, HY_PRE_NB, body, 0)


def _hy_pre(proj3, conv_w, conv_b):
    B, L, _ = proj3.shape
    nc = 3 * W_BR // LANES
    return pl.pallas_call(
        _hy_pre_kernel,
        out_shape=jax.ShapeDtypeStruct((3 * W_BR, B * L), BF16),
        grid=(B // HY_PRE_NB, nc),
        in_specs=[pl.BlockSpec((HY_PRE_NB, L, LANES), lambda b, c: (b, 0, COL_HY // LANES + c)),
                  pl.BlockSpec((3, LANES), lambda b, c: (0, c)),
                  pl.BlockSpec((1, LANES), lambda b, c: (0, c))],
        out_specs=pl.BlockSpec((LANES, HY_PRE_NB * L), lambda b, c: (c, b)),
        compiler_params=_params(("parallel", "parallel"), 48),
        name="hy_pre",
    )(proj3, conv_w, conv_b)


HY_CB = 8
HY_KT = 2 * LANES
BF16_ROWS = 16


def _hy_build_tables(kk_ref, ci, g0_ref, g1_ref, L):
    nchunk = 2 * L // LANES
    upper = (lax.broadcasted_iota(jnp.int32, (LANES, LANES), 1)
             >= lax.broadcasted_iota(jnp.int32, (LANES, LANES), 0))

    def rolled(q):
        lo = 2 * LANES * (q // 2)
        half = slice(LANES * (q % 2), LANES * (q % 2 + 1))
        s0 = kk_ref[0, pl.ds(ci, 1), lo:lo + 2 * LANES][:, half]
        s1 = kk_ref[1, pl.ds(ci, 1), lo:lo + 2 * LANES][:, half]
        b0 = lax.bitcast_convert_type(s0.astype(BF16).astype(F32), jnp.uint32)
        b1 = lax.bitcast_convert_type(s1.astype(BF16).astype(F32), jnp.uint32)
        return pltpu.roll(jnp.broadcast_to(b0 | (b1 >> 16), (LANES, LANES)), 0, 1, stride=1, stride_axis=0)

    prev = rolled(nchunk - 1)
    for m in range(nchunk - 1):
        cur = rolled(nchunk - 2 - m)
        r = jnp.where(upper, prev, cur)
        rows = slice(LANES * m, LANES * (m + 1))
        g0_ref[rows, :] = lax.bitcast_convert_type(r & jnp.uint32(0xFFFF0000), F32).astype(BF16)
        g1_ref[rows, :] = lax.bitcast_convert_type(r << 16, F32).astype(BF16)
        prev = cur


def _hy_toeplitz(u, g_ref, u_ref, acc_ref, L):
    B = u.shape[0]
    nsb = L // HY_KT
    bp = u_ref.shape[0] // nsb
    pad = jnp.zeros((bp - B, HY_KT), F32)
    for sb in range(nsb):
        blk = u[:, HY_KT * sb:HY_KT * (sb + 1)]
        if bp > B:
            blk = jnp.concatenate([blk, pad], axis=0)
        u_ref[bp * sb:bp * (sb + 1), :] = blk.astype(BF16)
    acc_ref[...] = jnp.zeros(acc_ref.shape, F32)
    for delta in range(-(nsb - 1), nsb):
        d0 = HY_KT * delta + L - LANES
        tile = jnp.concatenate([g_ref[d0:d0 + HY_KT, :], g_ref[d0 - LANES:d0 - LANES + HY_KT, :]], axis=1)
        lo, hi = max(0, delta), min(nsb - 1, nsb - 1 + delta)
        part = jnp.dot(u_ref[bp * lo:bp * (hi + 1), :], tile, preferred_element_type=F32)
        acc_ref[bp * (lo - delta):bp * (hi - delta + 1), :] += part
    return jnp.concatenate([acc_ref[bp * tb:bp * tb + B, :] for tb in range(nsb)], axis=1)


def _hy_conv_kernel(z_ref, x1_ref, x2_ref, kk_ref, o_ref, g_ref, u_ref, acc_ref):
    L = z_ref.shape[2]

    def compute(ci, slot):
        y0 = _hy_toeplitz(z_ref[ci].astype(F32), g_ref.at[slot, 0], u_ref, acc_ref, L)
        z1 = x1_ref[ci].astype(F32) * y0
        y1 = _hy_toeplitz(z1, g_ref.at[slot, 1], u_ref, acc_ref, L)
        o_ref[ci] = (x2_ref[ci].astype(F32) * y1).astype(BF16)

    def build(ci, slot):
        _hy_build_tables(kk_ref, ci, g_ref.at[slot, 0], g_ref.at[slot, 1], L)

    build(0, 0)

    def body(k, carry):
        build(2 * k + 1, 1)
        compute(2 * k, 0)
        build(jnp.minimum(2 * k + 2, HY_CB - 1), 0)
        compute(2 * k + 1, 1)
        return carry

    lax.fori_loop(0, HY_CB // 2, body, 0)


def _hy_conv(zc3, kk):
    _, B, L = zc3.shape
    nblk = W_BR // HY_CB
    bp = -(-B // BF16_ROWS) * BF16_ROWS
    nsb = L // HY_KT
    act = lambda off: pl.BlockSpec((HY_CB, B, L), lambda c: (c + off * nblk, 0, 0))
    return pl.pallas_call(
        _hy_conv_kernel,
        out_shape=jax.ShapeDtypeStruct((W_BR, B, L), BF16),
        grid=(nblk,),
        in_specs=[act(0), act(1), act(2),
                  pl.BlockSpec((2, HY_CB, 2 * L), lambda c: (0, c, 0))],
        out_specs=pl.BlockSpec((HY_CB, B, L), lambda c: (c, 0, 0)),
        scratch_shapes=[pltpu.VMEM((2, 2, 2 * L - LANES, LANES), BF16),
                        pltpu.VMEM((nsb * bp, HY_KT), BF16),
                        pltpu.VMEM((nsb * bp, HY_KT), F32)],
        compiler_params=_params(("parallel",), 48),
        name="hy_conv",
    )(zc3, zc3, zc3, kk)


HY_POST_TT = 8192


def _hy_post_kernel(y_ref, o_ref):
    o_ref[...] = y_ref[...].astype(F32).T.astype(BF16)


def _hy_post(yc):
    C, T = yc.shape
    return pl.pallas_call(
        _hy_post_kernel,
        out_shape=jax.ShapeDtypeStruct((T, C), BF16),
        grid=(T // HY_POST_TT, C // LANES),
        in_specs=[pl.BlockSpec((LANES, HY_POST_TT), lambda t, c: (c, t))],
        out_specs=pl.BlockSpec((HY_POST_TT, LANES), lambda t, c: (t, c)),
        compiler_params=_params(("parallel", "parallel"), 48),
        name="hy_post",
    )(yc)


def _sgu_kernel(u_ref, v_ref, g_ref, b_ref, ws_ref, bs_ref, o_ref):
    L = u_ref.shape[0]
    gw = W_BR // SGU_GROUPS

    def body(n, carry):
        rows = pl.ds(pl.multiple_of(n * SGU_CHUNK, SGU_CHUNK), SGU_CHUNK)
        v = v_ref[rows, :].astype(F32)
        mu = jnp.mean(v, axis=-1, keepdims=True)
        d = v - mu
        var = jnp.mean(d * d, axis=-1, keepdims=True)
        vn = (d * lax.rsqrt(var + EPS) * g_ref[...] + b_ref[...]).astype(BF16)
        mixed = jnp.concatenate(
            [jnp.dot(ws_ref[k], vn[:, gw * k:gw * (k + 1)], preferred_element_type=F32) + bs_ref[k]
             for k in range(SGU_GROUPS)], axis=1)
        o_ref[rows, :] = (u_ref[rows, :].astype(F32) * mixed).astype(BF16)
        return carry

    lax.fori_loop(0, L // SGU_CHUNK, body, 0)


def _sgu(proj3, ln_g, ln_b, ws, bs):
    B, L, _ = proj3.shape
    cb = COL_SGU // W_BR
    return pl.pallas_call(
        _sgu_kernel,
        out_shape=jax.ShapeDtypeStruct((B, L, W_BR), BF16),
        grid=(B,),
        in_specs=[pl.BlockSpec((None, L, W_BR), lambda b: (b, 0, cb)),
                  pl.BlockSpec((None, L, W_BR), lambda b: (b, 0, cb + 1)),
                  pl.BlockSpec((1, W_BR), lambda b: (0, 0)),
                  pl.BlockSpec((1, W_BR), lambda b: (0, 0)),
                  pl.BlockSpec((SGU_GROUPS, SGU_CHUNK, SGU_CHUNK), lambda b: (0, 0, 0)),
                  pl.BlockSpec((SGU_GROUPS, SGU_CHUNK, 1), lambda b: (0, 0, 0))],
        out_specs=pl.BlockSpec((None, L, W_BR), lambda b: (b, 0, 0)),
        compiler_params=_params(("parallel",), 48),
        name="sgu",
    )(proj3, proj3, ln_g, ln_b, ws, bs)


RG_SLABS = 2
RG_TL = 256
RG_PAD = 8
RG_SKEW = 8
RG_UNROLL = 8


def _rg_kernel(xb_ref, gb_ref, cw_ref, cb_ref, w_ref, bias_ref, lam_ref, o_ref,
               xpad_ref, nat_ref, xi_ref, af_ref, bf_ref, ab_ref, bb_ref):
    L = xb_ref.shape[0]
    seg = L // SUBLANES
    pitch = seg + RG_SKEW
    zeros = jnp.zeros((RG_PAD, LANES), F32)
    sp_all = []
    for s in range(RG_SLABS):
        cols = slice(s * LANES, (s + 1) * LANES)
        xpad_ref[s, 0:RG_PAD, :] = zeros
        xpad_ref[s, RG_PAD + L:RG_PAD + L + RG_PAD, :] = zeros
        xpad_ref[s, RG_PAD:RG_PAD + L, :] = xb_ref[:, cols].astype(F32)
        cw = cw_ref[:, cols]
        for j in range(SUBLANES):
            t0 = j * seg
            xc = cb_ref[:, cols]
            for k in range(4):
                xc = xc + cw[k:k + 1] * xpad_ref[s, RG_PAD + t0 + k - 2:RG_PAD + t0 + k - 2 + seg, :]
            nat_ref[s, pitch * j:pitch * j + seg, :] = xc
        lam = lam_ref[:, cols]
        sp_all.append(jnp.maximum(-lam, 0.0) + jnp.log(1.0 + jnp.exp(-jnp.abs(lam))))

    def interleave(r, carry):
        dst = pl.ds(pl.multiple_of(r * SUBLANES, SUBLANES), SUBLANES)
        for s in range(RG_SLABS):
            xi_ref[s, dst, :] = nat_ref[s, pl.ds(r, SUBLANES, stride=pitch), :]
        return carry

    lax.fori_loop(0, seg, interleave, 0, unroll=RG_UNROLL)

    for s in range(RG_SLABS):
        for ti in range(L // RG_TL):
            rows = slice(ti * RG_TL, (ti + 1) * RG_TL)
            xc = xi_ref[s, rows, :]
            g = jnp.dot(xc.astype(BF16), w_ref[s], preferred_element_type=F32) + bias_ref[s]
            g = 0.5 * jnp.tanh(0.5 * g) + 0.5
            for d, (a_ref, b_ref) in enumerate(((af_ref, bf_ref), (ab_ref, bb_ref))):
                r = g[:, 2 * d * LANES:(2 * d + 1) * LANES]
                i = g[:, (2 * d + 1) * LANES:(2 * d + 2) * LANES]
                a = jnp.exp((-RG_C) * r * sp_all[s][d:d + 1])
                om = 1.0 - a * a
                a_ref[s, rows, :] = a
                b_ref[s, rows, :] = (om * lax.rsqrt(jnp.maximum(om, 1e-30))) * (i * xc)

    def rows_f(r):
        return pl.ds(pl.multiple_of(r * SUBLANES, SUBLANES), SUBLANES)

    def rows_b(r):
        return pl.ds(pl.multiple_of((seg - 1 - r) * SUBLANES, SUBLANES), SUBLANES)

    def pass1(r, carry):
        out = []
        for s in range(RG_SLABS):
            hf, pf, hb, pb = carry[4 * s:4 * s + 4]
            a = af_ref[s, rows_f(r), :]
            hf = a * hf + bf_ref[s, rows_f(r), :]
            pf = a * pf
            a = ab_ref[s, rows_b(r), :]
            hb = a * hb + bb_ref[s, rows_b(r), :]
            pb = a * pb
            out += [hf, pf, hb, pb]
        return tuple(out)

    z = jnp.zeros((SUBLANES, LANES), F32)
    one = jnp.ones((SUBLANES, LANES), F32)
    ends = lax.fori_loop(0, seg, pass1, (z, one, z, one) * RG_SLABS, unroll=RG_UNROLL)

    init = []
    for s in range(RG_SLABS):
        hf, pf, hb, pb = ends[4 * s:4 * s + 4]
        c = jnp.zeros((1, LANES), F32)
        rows = []
        for j in range(SUBLANES):
            rows.append(c)
            c = pf[j:j + 1] * c + hf[j:j + 1]
        init.append(jnp.concatenate(rows, axis=0))
        c = jnp.zeros((1, LANES), F32)
        rows = []
        for j in range(SUBLANES - 1, -1, -1):
            rows.append(c)
            c = pb[j:j + 1] * c + hb[j:j + 1]
        init.append(jnp.concatenate(rows[::-1], axis=0))

    def pass2(r, carry):
        out = []
        for s in range(RG_SLABS):
            hf, hb = carry[2 * s:2 * s + 2]
            hf = af_ref[s, rows_f(r), :] * hf + bf_ref[s, rows_f(r), :]
            xi_ref[s, rows_f(r), :] = hf
            hb = ab_ref[s, rows_b(r), :] * hb + bb_ref[s, rows_b(r), :]
            xpad_ref[s, rows_b(r), :] = hb
            out += [hf, hb]
        return tuple(out)

    lax.fori_loop(0, seg, pass2, tuple(init), unroll=RG_UNROLL)

    def deinterleave(r, carry):
        src = rows_f(r)
        for s in range(RG_SLABS):
            nat_ref[s, pl.ds(r, SUBLANES, stride=pitch), :] = xi_ref[s, src, :] + xpad_ref[s, src, :]
        return carry

    lax.fori_loop(0, seg, deinterleave, 0, unroll=RG_UNROLL)

    for s in range(RG_SLABS):
        cols = slice(s * LANES, (s + 1) * LANES)
        for j in range(SUBLANES):
            gb = gb_ref[j * seg:(j + 1) * seg, cols].astype(F32)
            gelu = 0.5 * gb * (1.0 + jnp.tanh(math.sqrt(2.0 / math.pi) * (gb + 0.044715 * (gb * gb * gb))))
            o_ref[j * seg:(j + 1) * seg, cols] = (nat_ref[s, pitch * j:pitch * j + seg, :] * gelu).astype(BF16)


def _rglru(proj3, conv_w, conv_b, w_bd, bias_bd, lam):
    B, L, _ = proj3.shape
    gw = RG_SLABS * LANES
    ng = W_BR // gw
    cx = COL_RG // gw
    seg = L // SUBLANES
    slab = lambda rows: pltpu.VMEM((RG_SLABS, rows, LANES), F32)
    return pl.pallas_call(
        _rg_kernel,
        out_shape=jax.ShapeDtypeStruct((B, L, W_BR), BF16),
        grid=(B, ng),
        in_specs=[pl.BlockSpec((None, L, gw), lambda b, c: (b, 0, cx + c)),
                  pl.BlockSpec((None, L, gw), lambda b, c: (b, 0, cx + ng + c)),
                  pl.BlockSpec((4, gw), lambda b, c: (0, c)),
                  pl.BlockSpec((1, gw), lambda b, c: (0, c)),
                  pl.BlockSpec((RG_SLABS, LANES, 4 * LANES), lambda b, c: (c, 0, 0)),
                  pl.BlockSpec((RG_SLABS, 1, 4 * LANES), lambda b, c: (c, 0, 0)),
                  pl.BlockSpec((2, gw), lambda b, c: (0, c))],
        out_specs=pl.BlockSpec((None, L, gw), lambda b, c: (b, 0, c)),
        scratch_shapes=[slab(L + 2 * RG_PAD), slab(SUBLANES * (seg + RG_SKEW))] + [slab(L)] * 5,
        compiler_params=_params(("parallel", "parallel"), 48),
        name="rglru",
    )(proj3, proj3, conv_w, conv_b, w_bd, bias_bd, lam)


def _log_sigmoid(x):
    return jnp.minimum(x, 0.0) - jnp.log(1.0 + jnp.exp(-jnp.abs(x)))


ML_AUG = 16


def _ml_kernel(qt_ref, vt_ref, ot_ref, k_ref, gt_ref, gtt_ref, gbias_ref, gbiast_ref, ng_ref, yt_ref,
               hf_ref, hb_ref, c_ref, m_ref):
    L = k_ref.shape[0]
    nc = L // ML_CHUNK
    row = lax.broadcasted_iota(jnp.int32, (ML_CHUNK, ML_CHUNK), 0)
    col = lax.broadcasted_iota(jnp.int32, (ML_CHUNK, ML_CHUNK), 1)
    gate_lane = lax.broadcasted_iota(jnp.int32, (ML_CHUNK, LANES), 1)
    gate_row = lax.broadcasted_iota(jnp.int32, (LANES, ML_CHUNK), 0)
    is_f = (gate_lane >= 2 * ML_HEADS) & (gate_lane < 4 * ML_HEADS)
    is_f_t = (gate_row >= 2 * ML_HEADS) & (gate_row < 4 * ML_HEADS)
    scale = ML_HD ** -0.5
    masks = (col <= row, col >= row)
    tris = tuple(jnp.where(mk, 1.0, 0.0) for mk in masks)
    c_ref[...] = jnp.zeros(c_ref.shape, F32)
    m_ref[...] = jnp.zeros(m_ref.shape, F32)

    def body(step, carry):
        for d in range(2):
            c = step if d == 0 else nc - 1 - step
            tsl = pl.ds(pl.multiple_of(c * ML_CHUNK, ML_CHUNK), ML_CHUNK)
            mask_st = masks[1 - d]
            pre = gt_ref[tsl, :] + gbias_ref[...]
            pre_t = gtt_ref[:, tsl] + gbiast_ref[...]
            cum = jnp.dot(tris[d], jnp.where(is_f, _log_sigmoid(pre), 0.0),
                          precision=HIGHEST, preferred_element_type=F32)
            cum_t = jnp.dot(jnp.where(is_f_t, _log_sigmoid(pre_t), 0.0), tris[1 - d],
                            precision=HIGHEST, preferred_element_type=F32)
            last = ML_CHUNK - 1 if d == 0 else 0
            for hd in range(ML_HEADS):
                li_lane = d * ML_HEADS + hd
                lf_lane = 2 * ML_HEADS + d * ML_HEADS + hd
                b_row = cum_t[lf_lane:lf_lane + 1, :]
                li_row = pre_t[li_lane:li_lane + 1, :]
                c_col = pre[:, li_lane:li_lane + 1] - cum[:, lf_lane:lf_lane + 1]
                g_tot = b_row[:, last:last + 1]
                m_prev = m_ref[d, hd]
                dlog = jnp.where(mask_st, b_row + c_col, -jnp.inf)
                inter = b_row + m_prev
                m_t = jnp.maximum(inter, jnp.max(dlog, axis=0, keepdims=True))
                w_intra = jnp.exp(dlog - m_t)
                w_inter = jnp.exp(inter - m_t)
                hs = slice(hd * ML_HD, (hd + 1) * ML_HD)
                qt = qt_ref[hs, tsl]
                kh = k_ref[tsl, hs]
                vt = vt_ref[hs, tsl]
                s = jnp.dot(kh, qt, preferred_element_type=F32) * (scale * w_intra)
                ca = c_ref[d, hd]
                qc = jnp.dot(ca.astype(BF16), qt, preferred_element_type=F32) * scale
                num = w_inter * qc[0:ML_HD] + jnp.dot(vt, s.astype(BF16), preferred_element_type=F32)
                den = w_inter * qc[ML_HD:ML_HD + 1] + jnp.sum(s, axis=0, keepdims=True)
                hout = num * (1.0 / jnp.maximum(jnp.abs(den), jnp.exp(-m_t)))
                if d == 0:
                    hf_ref[hs, tsl] = hout
                else:
                    hb_ref[hs, tsl] = hout
                wlog = g_tot - b_row + li_row
                m_new = jnp.maximum(g_tot + m_prev, jnp.max(wlog, axis=1, keepdims=True))
                decay = jnp.exp(g_tot + m_prev - m_new)
                ws = jnp.exp(wlog - m_new)
                vw = jnp.concatenate([vt.astype(F32) * ws, jnp.broadcast_to(ws, (ML_AUG, ML_CHUNK))],
                                     axis=0).astype(BF16)
                c_ref[d, hd] = decay * ca + jnp.dot(vw, kh, preferred_element_type=F32)
                m_ref[d, hd] = m_new
        return carry

    lax.fori_loop(0, nc, body, 0)

    def finish(c, carry):
        tsl = pl.ds(pl.multiple_of(c * ML_CHUNK, ML_CHUNK), ML_CHUNK)
        for hd in range(ML_HEADS):
            hs = slice(hd * ML_HD, (hd + 1) * ML_HD)
            h = hf_ref[hs, tsl] + hb_ref[hs, tsl]
            hn = h * lax.rsqrt(jnp.mean(h * h, axis=0, keepdims=True) + EPS) * ng_ref[hs, :]
            yt_ref[hs, tsl] = (_sigmoid(ot_ref[hs, tsl].astype(F32)) * hn).astype(BF16)
        return carry

    lax.fori_loop(0, nc, finish, 0)


def _mlstm(qvot, proj3, mlg3, mlgt, gbias, norm_g):
    B, L, _ = proj3.shape
    cb = COL_ML // W_BR
    fm = lambda j: pl.BlockSpec((W_BR, L), lambda b: (j, b))
    return pl.pallas_call(
        _ml_kernel,
        out_shape=jax.ShapeDtypeStruct((W_BR, B * L), BF16),
        grid=(B,),
        in_specs=[fm(0), fm(1), fm(2),
                  pl.BlockSpec((None, L, W_BR), lambda b: (b, 0, cb)),
                  pl.BlockSpec((None, L, LANES), lambda b: (b, 0, 0)),
                  pl.BlockSpec((LANES, L), lambda b: (0, b)),
                  pl.BlockSpec((1, LANES), lambda b: (0, 0)),
                  pl.BlockSpec((LANES, 1), lambda b: (0, 0)),
                  pl.BlockSpec((W_BR, LANES), lambda b: (0, 0))],
        out_specs=pl.BlockSpec((W_BR, L), lambda b: (0, b)),
        scratch_shapes=[pltpu.VMEM((W_BR, L), F32),
                        pltpu.VMEM((W_BR, L), F32),
                        pltpu.VMEM((2, ML_HEADS, ML_HD + ML_AUG, ML_HD), F32),
                        pltpu.VMEM((2, ML_HEADS, 1, 1), F32)],
        compiler_params=_params(("parallel",), 48),
        name="mlstm",
    )(qvot, qvot, qvot, proj3, mlg3, mlgt, gbias, gbias.reshape(LANES, 1),
      jnp.broadcast_to(norm_g.reshape(W_BR, 1), (W_BR, LANES)))


def _merge_kernel(route, gate_ref, ya_ref, yb_ref, yc_ref, yd_ref, wb_ref, wo_ref, x_ref, g_ref, *rest):
    if route:
        rw_ref, rb_ref, xo_ref, hn_ref, comb_ref, combt_ref = rest
    else:
        xo_ref, hn_ref = rest
    merged = None
    for k, y_ref in enumerate((ya_ref, yb_ref, yc_ref, yd_ref)):
        t = jnp.dot(y_ref[...], wb_ref[k], preferred_element_type=F32)
        gk = _sigmoid(gate_ref[:, k * D_MODEL:(k + 1) * D_MODEL].astype(F32))
        merged = gk * t if merged is None else merged + gk * t
    xn = x_ref[...] + jnp.dot(merged.astype(BF16), wo_ref[...], preferred_element_type=F32)
    xo_ref[...] = xn
    h = _rms(xn, g_ref[...])
    h_hi = h.astype(BF16)
    hn_ref[...] = h_hi
    if route:
        h_lo = (h - h_hi.astype(F32)).astype(BF16)
        r_hi = jnp.dot(h_hi, rw_ref[...], preferred_element_type=F32)
        r_lo = jnp.dot(h_lo, rw_ref[:, 0:LANES], preferred_element_type=F32)
        logits = r_hi[:, 0:LANES] + r_hi[:, LANES:2 * LANES] + r_lo + rb_ref[...]
        lane = lax.broadcasted_iota(jnp.int32, logits.shape, 1)
        logits = jnp.where(lane < N_EXPERTS, logits, -jnp.inf)
        v1 = jnp.max(logits, axis=1, keepdims=True)
        i1 = jnp.min(jnp.where(logits == v1, lane, LANES), axis=1, keepdims=True)
        rest_l = jnp.where(lane == i1, -jnp.inf, logits)
        v2 = jnp.max(rest_l, axis=1, keepdims=True)
        i2 = jnp.min(jnp.where(rest_l == v2, lane, LANES), axis=1, keepdims=True)
        e2 = jnp.exp(v2 - v1)
        p1 = 1.0 / (1.0 + e2)
        comb = jnp.where(lane == i1, p1, jnp.where(lane == i2, e2 * p1, 0.0))
        comb_ref[...] = comb
        combt_ref[...] = comb.T


def _merge(proj, ya, yb, yc, yd, wb, wo, x, g, router=None, tm=512):
    T = x.shape[0]
    row = lambda w: pl.BlockSpec((tm, w), lambda i: (i, 0))
    in_specs = [row(N_BRANCH * D_MODEL), row(W_BR), row(W_BR), row(W_BR), row(W_BR),
                pl.BlockSpec((N_BRANCH, W_BR, D_MODEL), lambda i: (0, 0, 0)),
                pl.BlockSpec((D_MODEL, D_MODEL), lambda i: (0, 0)),
                row(D_MODEL),
                pl.BlockSpec((1, D_MODEL), lambda i: (0, 0))]
    out_shape = [jax.ShapeDtypeStruct((T, D_MODEL), F32), jax.ShapeDtypeStruct((T, D_MODEL), BF16)]
    out_specs = [row(D_MODEL), row(D_MODEL)]
    args = [proj, ya, yb, yc, yd, wb, wo, x, g]
    if router is not None:
        in_specs += [pl.BlockSpec((D_MODEL, 2 * LANES), lambda i: (0, 0)),
                     pl.BlockSpec((1, LANES), lambda i: (0, 0))]
        out_shape += [jax.ShapeDtypeStruct((T, LANES), F32), jax.ShapeDtypeStruct((LANES, T), F32)]
        out_specs += [row(LANES), pl.BlockSpec((LANES, tm), lambda i: (0, i))]
        args += list(router)
    return pl.pallas_call(
        functools.partial(_merge_kernel, router is not None),
        out_shape=tuple(out_shape),
        grid=(T // tm,),
        in_specs=in_specs,
        out_specs=tuple(out_specs),
        compiler_params=_params(("parallel",), 56),
        name="merge_route" if router is not None else "merge",
    )(*args)


def _swiglu_acc(h, w1_ref, w3_ref, w2_ref):
    a = jnp.dot(h, w1_ref[...], preferred_element_type=F32)
    b = jnp.dot(h, w3_ref[...], preferred_element_type=F32)
    act = (a * _sigmoid(a) * b).astype(BF16)
    return jnp.dot(act, w2_ref[...], preferred_element_type=F32)


def _ffn_kernel(h_ref, x_ref, w1_ref, w3_ref, w2_ref, o_ref, acc_ref):
    f = pl.program_id(1)

    @pl.when(f == 0)
    def _():
        acc_ref[...] = x_ref[...]

    acc_ref[...] += _swiglu_acc(h_ref[...], w1_ref, w3_ref, w2_ref)

    @pl.when(f == pl.num_programs(1) - 1)
    def _():
        o_ref[...] = acc_ref[...]


def _ffn(hn, x, w1, w3, w2, tm=512, n_split=2):
    T = x.shape[0]
    dff = w1.shape[1]
    tf = dff // n_split
    return pl.pallas_call(
        _ffn_kernel,
        out_shape=jax.ShapeDtypeStruct((T, D_MODEL), F32),
        grid=(T // tm, n_split),
        in_specs=[pl.BlockSpec((tm, D_MODEL), lambda i, f: (i, 0)),
                  pl.BlockSpec((tm, D_MODEL), lambda i, f: (i, 0)),
                  pl.BlockSpec((D_MODEL, tf), lambda i, f: (0, f)),
                  pl.BlockSpec((D_MODEL, tf), lambda i, f: (0, f)),
                  pl.BlockSpec((tf, D_MODEL), lambda i, f: (f, 0))],
        out_specs=pl.BlockSpec((tm, D_MODEL), lambda i, f: (i, 0)),
        scratch_shapes=[pltpu.VMEM((tm, D_MODEL), F32)],
        compiler_params=_params(("parallel", "arbitrary"), 56),
        name="ffn",
    )(hn, x, w1, w3, w2)


MOE_TM = 1024
MOE_CH = 256


def _moe_kernel(cnt_ref, h_ref, x_ref, comb_ref, combt_ref, w1_ref, w3_ref, w2_ref, o_ref,
                rank_ref, rankt_ref):
    i = pl.program_id(0)
    e = pl.program_id(1)
    tm = h_ref.shape[0]

    @pl.when(e == 0)
    def _():
        o_ref[...] = x_ref[...]
        r = lax.broadcasted_iota(jnp.int32, (tm, tm), 0)
        c = lax.broadcasted_iota(jnp.int32, (tm, tm), 1)
        before = jnp.where(c < r, 1.0, 0.0).astype(BF16)
        sel = jnp.where(comb_ref[...] > 0.0, 1.0, 0.0).astype(BF16)
        rank_ref[...] = jnp.dot(before, sel, preferred_element_type=F32)
        selt = jnp.where(combt_ref[...] > 0.0, 1.0, 0.0).astype(BF16)
        rankt_ref[...] = lax.dot_general(selt, before, (((1,), (1,)), ((), ())),
                                         preferred_element_type=F32)

    lane = lax.broadcasted_iota(jnp.int32, (tm, LANES), 1)
    comb = comb_ref[...]
    c_col = jnp.sum(jnp.where(lane == e, comb, 0.0), axis=1, keepdims=True)
    rank_col = jnp.sum(jnp.where(lane == e, rank_ref[...], 0.0), axis=1, keepdims=True)
    rank_col = jnp.where(c_col > 0.0, rank_col, -1.0)
    c_row = combt_ref[pl.ds(e, 1), :]
    rank_row = jnp.where(c_row > 0.0, rankt_ref[pl.ds(e, 1), :], -1.0)
    cnt = cnt_ref[i * N_EXPERTS + e]

    def run_chunk(base, size):
        pos_r = (lax.broadcasted_iota(jnp.int32, (size, tm), 0) + base).astype(F32)
        gather = jnp.where(rank_row == pos_r, 1.0, 0.0).astype(BF16)
        xs = jnp.dot(gather, h_ref[...], preferred_element_type=F32).astype(BF16)
        y = _swiglu_acc(xs, w1_ref, w3_ref, w2_ref).astype(BF16)
        pos_c = (lax.broadcasted_iota(jnp.int32, (tm, size), 1) + base).astype(F32)
        scatter = jnp.where(rank_col == pos_c, c_col, 0.0).astype(BF16)
        o_ref[...] += jnp.dot(scatter, y, preferred_element_type=F32)

    n_full = cnt // MOE_CH
    rem = cnt - n_full * MOE_CH

    def body(j, carry):
        run_chunk(j * MOE_CH, MOE_CH)
        return carry

    lax.fori_loop(0, n_full, body, 0)

    @pl.when(rem > MOE_CH // 2)
    def _():
        run_chunk(n_full * MOE_CH, MOE_CH)

    @pl.when((rem > 0) & (rem <= MOE_CH // 2))
    def _():
        run_chunk(n_full * MOE_CH, MOE_CH // 2)


def _moe(hn, x, comb, combt, w1, w3, w2, tm=MOE_TM):
    T = x.shape[0]
    dfe = w1.shape[2]
    nt = T // tm
    cnt = jnp.sum((comb[:, :N_EXPERTS] > 0.0).reshape(nt, tm, N_EXPERTS), axis=1, dtype=jnp.int32).reshape(-1)
    grid_spec = pltpu.PrefetchScalarGridSpec(
        num_scalar_prefetch=1,
        grid=(nt, N_EXPERTS),
        in_specs=[pl.BlockSpec((tm, D_MODEL), lambda i, e, c: (i, 0)),
                  pl.BlockSpec((tm, D_MODEL), lambda i, e, c: (i, 0)),
                  pl.BlockSpec((tm, LANES), lambda i, e, c: (i, 0)),
                  pl.BlockSpec((LANES, tm), lambda i, e, c: (0, i)),
                  pl.BlockSpec((None, D_MODEL, dfe), lambda i, e, c: (e, 0, 0)),
                  pl.BlockSpec((None, D_MODEL, dfe), lambda i, e, c: (e, 0, 0)),
                  pl.BlockSpec((None, dfe, D_MODEL), lambda i, e, c: (e, 0, 0))],
        out_specs=pl.BlockSpec((tm, D_MODEL), lambda i, e, c: (i, 0)),
        scratch_shapes=[pltpu.VMEM((tm, LANES), F32), pltpu.VMEM((LANES, tm), F32)])
    return pl.pallas_call(
        _moe_kernel,
        out_shape=jax.ShapeDtypeStruct((T, D_MODEL), F32),
        grid_spec=grid_spec,
        compiler_params=_params(("parallel", "arbitrary"), 56),
        name="moe",
    )(cnt, hn, x, comb, combt, w1, w3, w2)


def _final_norm_kernel(x_ref, g_ref, o_ref):
    o_ref[...] = _rms(x_ref[...], g_ref[...])


def _final_norm(x, g, row0, rows, tm=1024):
    first = row0 // tm
    return pl.pallas_call(
        _final_norm_kernel,
        out_shape=jax.ShapeDtypeStruct((rows, D_MODEL), F32),
        grid=(rows // tm,),
        in_specs=[pl.BlockSpec((tm, D_MODEL), lambda i: (i + first, 0)),
                  pl.BlockSpec((1, D_MODEL), lambda i: (0, 0))],
        out_specs=pl.BlockSpec((tm, D_MODEL), lambda i: (i, 0)),
        compiler_params=_params(("parallel",), 48),
        name="final_norm",
    )(x, g)


def _split_w_in(w_in_l):
    n_hy, n_sgu, n_rg = 3 * W_BR, 2 * W_BR, 2 * W_BR
    o1 = n_hy
    o2 = o1 + n_sgu
    o3 = o2 + n_rg
    o4 = o3 + 4 * W_BR
    o5 = o4 + 4 * ML_HEADS
    w = jnp.concatenate([w_in_l[:, o5:], w_in_l[:, :o3], w_in_l[:, o3 + W_BR:o3 + 2 * W_BR]],
                        axis=1).astype(BF16)
    wg = jnp.pad(w_in_l[:, o4:o5], ((0, 0), (0, LANES - 4 * ML_HEADS))).astype(BF16)
    wft = jnp.concatenate([w_in_l[:, o3:o3 + W_BR], w_in_l[:, o3 + 2 * W_BR:o4]], axis=1).T.astype(BF16)
    return w, wg, wft


def _router_operands(router_w, router_b):
    w = jnp.pad(router_w, ((0, 0), (0, LANES - N_EXPERTS)))
    w_hi = w.astype(BF16)
    w_lo = (w - w_hi.astype(F32)).astype(BF16)
    return (jnp.concatenate([w_hi, w_lo], axis=1),
            jnp.pad(router_b, (0, LANES - N_EXPERTS))[None, :])


def _rg_block_diag(wa, ba, wx, bx):
    hpg = LANES // RG_HD
    ng = RG_HEADS // hpg
    eye = jnp.eye(hpg, dtype=F32)

    def bd(w):
        w = w.reshape(ng, hpg, RG_HD, RG_HD)
        return jnp.einsum('gaij,ab->gaibj', w, eye).reshape(ng, LANES, LANES)

    w = jnp.concatenate([bd(wa[0]), bd(wx[0]), bd(wa[1]), bd(wx[1])], axis=2).astype(BF16)
    fl = lambda b: b.reshape(ng, 1, LANES)
    bias = jnp.concatenate([fl(ba[0]), fl(bx[0]), fl(ba[1]), fl(bx[1])], axis=2)
    return w, bias


def _token_mixer(x, B, L, l, norm_g, w_in, kk, hy_conv_w, hy_conv_b, sgu_ln_g, sgu_ln_b, sgu_ws, sgu_bs,
                 rg_conv_w, rg_conv_b, rg_wa, rg_ba, rg_wx, rg_bx, rg_lambda, ml_i_bias, ml_f_bias,
                 ml_norm_g, w_branch, w_out, next_g, router):
    T = B * L
    w, wg, wft = _split_w_in(w_in[l])
    proj, mlg, mlgt, qvot = _inproj(x, norm_g[l][None, :], w, wg, wft)
    proj3 = proj.reshape(B, L, N_PROJ)

    zc = _hy_pre(proj3, hy_conv_w[l], hy_conv_b[l][None, :])
    ya_c = _hy_conv(zc.reshape(3 * W_BR, B, L), kk[l])
    ya = _hy_post(ya_c.reshape(W_BR, T))

    yb = _sgu(proj3, sgu_ln_g[l][None, :], sgu_ln_b[l][None, :], sgu_ws[l].astype(BF16),
              sgu_bs[l][:, :, None]).reshape(T, W_BR)

    w_bd, bias_bd = _rg_block_diag(rg_wa[l], rg_ba[l], rg_wx[l], rg_bx[l])
    yc = _rglru(proj3, rg_conv_w[l], rg_conv_b[l][None, :], w_bd, bias_bd, rg_lambda[l]).reshape(T, W_BR)

    gbias = jnp.pad(jnp.concatenate([ml_i_bias[l].reshape(-1), ml_f_bias[l].reshape(-1)]),
                    (0, LANES - 4 * ML_HEADS))[None, :]
    yd = _hy_post(_mlstm(qvot, proj3, mlg.reshape(B, L, LANES), mlgt, gbias, ml_norm_g[l]))

    return _merge(proj, ya, yb, yc, yd, w_branch[l].astype(BF16), w_out[l].astype(BF16), x,
                  next_g[None, :], router)


def kernel(x_prompt, x_sample, norm_mix_g, w_in, hy_conv_w, hy_conv_b, hy_ffn_w1, hy_ffn_b1, hy_ffn_w2, hy_ffn_b2, hy_ffn_w3, hy_bias, sgu_ln_g, sgu_ln_b, sgu_ws, sgu_bs, rg_conv_w, rg_conv_b, rg_wa, rg_ba, rg_wx, rg_bx, rg_lambda, ml_i_bias, ml_f_bias, ml_norm_g, w_branch, w_out, norm_ffn_g, ffn_w1, ffn_w3, ffn_w2, router_w, router_b, moe_w1, moe_w3, moe_w2, norm_final_g):
    bp, L, _ = x_prompt.shape
    bs = x_sample.shape[0]
    B = bp + bs
    depth = w_in.shape[0]
    x = jnp.concatenate([x_prompt, x_sample], axis=0).reshape(B * L, D_MODEL)
    kk = _hy_filters(hy_ffn_w1, hy_ffn_b1, hy_ffn_w2, hy_ffn_b2, hy_ffn_w3, hy_bias, L)
    for l in range(depth):
        j = l // 2
        router = None
        if l % 2 == 1:
            router = _router_operands(router_w[j], router_b[j])
        outs = _token_mixer(x, B, L, l, norm_mix_g, w_in, kk, hy_conv_w, hy_conv_b, sgu_ln_g, sgu_ln_b,
                            sgu_ws, sgu_bs, rg_conv_w, rg_conv_b, rg_wa, rg_ba, rg_wx, rg_bx, rg_lambda,
                            ml_i_bias, ml_f_bias, ml_norm_g, w_branch, w_out, norm_ffn_g[l], router)
        if router is None:
            x, hn = outs
            x = _ffn(hn, x, ffn_w1[j].astype(BF16), ffn_w3[j].astype(BF16), ffn_w2[j].astype(BF16))
        else:
            x, hn, comb, combt = outs
            x = _moe(hn, x, comb, combt, moe_w1[j].astype(BF16), moe_w3[j].astype(BF16),
                     moe_w2[j].astype(BF16))
    g = norm_final_g[None, :]
    return (_final_norm(x, g, 0, bp * L).reshape(bp, L, D_MODEL),
            _final_norm(x, g, bp * L, bs * L).reshape(bs, L, D_MODEL))
```

```python
import functools
import math

import jax
import jax.numpy as jnp
import numpy as np
from jax import lax
from jax.experimental import pallas as pl
from jax.experimental.pallas import tpu as pltpu

F32 = jnp.float32
BF16 = jnp.bfloat16
EPS = 1e-6
HIGHEST = lax.Precision.HIGHEST

LANES = 128
SUBLANES = 8
VMEM_BYTES_V7X = 64 * 1024 * 1024

D_MODEL = 1024
W_BR = 512
N_BRANCH = 4
HY_BANDS = 16
HY_EMB = 1 + 2 * HY_BANDS
HY_FFN = 64
HY_TARGET = 1e-2
HY_MIN_DECAY = -math.log(HY_TARGET) / 1.5
HY_MAX_DECAY = -math.log(HY_TARGET) / 0.3
SGU_CHUNK = 128
SGU_GROUPS = 4
RG_HEADS = 8
RG_HD = W_BR // RG_HEADS
RG_C = 8.0
ML_HEADS = 4
ML_HD = W_BR // ML_HEADS
ML_CHUNK = 128
N_EXPERTS = 8

N_PROJ = N_BRANCH * D_MODEL + 3 * W_BR + 2 * W_BR + 2 * W_BR + W_BR
COL_GATE = 0
COL_HY = N_BRANCH * D_MODEL
COL_SGU = COL_HY + 3 * W_BR
COL_RG = COL_SGU + 2 * W_BR
COL_ML = COL_RG + 2 * W_BR


def _params(semantics, vmem_mb):
    return pltpu.CompilerParams(dimension_semantics=semantics,
                                vmem_limit_bytes=vmem_mb * 1024 * 1024)


def _sigmoid(x):
    return 1.0 / (1.0 + jnp.exp(-x))


def _rms(x, g):
    return x * lax.rsqrt(jnp.mean(x * x, axis=-1, keepdims=True) + EPS) * g


_NT = (((1,), (1,)), ((), ()))


def _inproj_kernel(x_ref, g_ref, w_ref, wgt_ref, wft_ref, proj_ref, mlgt_ref, ft_ref, h_ref):
    @pl.when(pl.program_id(1) == 0)
    def _():
        h = _rms(x_ref[...], g_ref[...]).astype(BF16)
        h_ref[...] = h
        mlgt_ref[...] = lax.dot_general(wgt_ref[...], h, _NT, preferred_element_type=F32)
        ft_ref[...] = lax.dot_general(wft_ref[...], h, _NT, preferred_element_type=F32).astype(BF16)

    proj_ref[...] = jnp.dot(h_ref[...], w_ref[...], preferred_element_type=F32).astype(BF16)


def _inproj(x, g, w, wgt, wft, tm=1024, n_split=4):
    T = x.shape[0]
    tn = N_PROJ // n_split
    nf = wft.shape[0]
    return pl.pallas_call(
        _inproj_kernel,
        out_shape=(jax.ShapeDtypeStruct((T, N_PROJ), BF16),
                   jax.ShapeDtypeStruct((LANES, T), F32),
                   jax.ShapeDtypeStruct((nf, T), BF16)),
        grid=(T // tm, n_split),
        in_specs=[pl.BlockSpec((tm, D_MODEL), lambda i, j: (i, 0)),
                  pl.BlockSpec((1, D_MODEL), lambda i, j: (0, 0)),
                  pl.BlockSpec((D_MODEL, tn), lambda i, j: (0, j)),
                  pl.BlockSpec((LANES, D_MODEL), lambda i, j: (0, 0)),
                  pl.BlockSpec((nf, D_MODEL), lambda i, j: (0, 0))],
        out_specs=(pl.BlockSpec((tm, tn), lambda i, j: (i, j)),
                   pl.BlockSpec((LANES, tm), lambda i, j: (0, i)),
                   pl.BlockSpec((nf, tm), lambda i, j: (0, i))),
        scratch_shapes=[pltpu.VMEM((tm, D_MODEL), BF16)],
        compiler_params=_params(("parallel", "arbitrary"), 56),
        name="inproj",
    )(x, g, w, wgt, wft)


def _hy_filter_kernel(z_ref, w1_ref, b1_ref, w2_ref, b2_ref, w3_ref, delta_ref, bias_ref, kk_ref):
    half = pl.program_id(2)
    z = z_ref[...]
    h = jnp.sin(jnp.dot(w1_ref[...], z, precision=HIGHEST, preferred_element_type=F32) + b1_ref[...])
    h = jnp.sin(jnp.dot(w2_ref[...], h, precision=HIGHEST, preferred_element_type=F32) + b2_ref[...])
    f = jnp.dot(w3_ref[...], h, precision=HIGHEST, preferred_element_type=F32)
    t_norm = z[0:1, :]
    f = f * jnp.exp(-t_norm * delta_ref[...])
    lane = lax.broadcasted_iota(jnp.int32, f.shape, 1)
    first = lane == 0
    f = jnp.where(first, jnp.where(half == 0, 0.0, f + bias_ref[...]), f)
    kk_ref[...] = f


def _hy_filters(hy_ffn_w1, hy_ffn_b1, hy_ffn_w2, hy_ffn_b2, hy_ffn_w3, hy_bias, L):
    depth = hy_ffn_w1.shape[0]
    lag = np.stack([L - np.arange(L), np.arange(L)]).astype(np.float64)
    bands = np.arange(1, HY_BANDS + 1, dtype=np.float64)
    ang = (2.0 * math.pi / L) * lag[:, None, :] * bands[None, :, None]
    z = np.concatenate([lag[:, None, :] / L, np.cos(ang), np.sin(ang)], axis=1)
    z = np.pad(z, ((0, 0), (0, LANES - HY_EMB), (0, 0))).astype(np.float32)
    z = jnp.asarray(z)
    w1t = jnp.pad(jnp.swapaxes(hy_ffn_w1, 1, 2), ((0, 0), (0, 0), (0, LANES - HY_EMB)))
    w2t = jnp.swapaxes(hy_ffn_w2, 1, 2)
    w3t = jnp.swapaxes(hy_ffn_w3, 1, 2).reshape(depth, 4, W_BR, HY_FFN)
    b1 = hy_ffn_b1[:, :, None]
    b2 = hy_ffn_b2[:, :, None]
    delta = jnp.linspace(HY_MIN_DECAY, HY_MAX_DECAY, W_BR, dtype=F32)[:, None]
    bias = hy_bias[:, :, :, None]
    return pl.pallas_call(
        _hy_filter_kernel,
        out_shape=jax.ShapeDtypeStruct((depth, 2, W_BR, 2 * L), F32),
        grid=(depth, 2, 2),
        in_specs=[pl.BlockSpec((None, LANES, L), lambda l, o, s: (s, 0, 0)),
                  pl.BlockSpec((None, HY_FFN, LANES), lambda l, o, s: (l, 0, 0)),
                  pl.BlockSpec((None, HY_FFN, 1), lambda l, o, s: (l, 0, 0)),
                  pl.BlockSpec((None, HY_FFN, HY_FFN), lambda l, o, s: (l, 0, 0)),
                  pl.BlockSpec((None, HY_FFN, 1), lambda l, o, s: (l, 0, 0)),
                  pl.BlockSpec((None, None, W_BR, HY_FFN), lambda l, o, s: (l, 2 * o + 1 - s, 0, 0)),
                  pl.BlockSpec((W_BR, 1), lambda l, o, s: (0, 0)),
                  pl.BlockSpec((None, None, W_BR, 1), lambda l, o, s: (l, o, 0, 0))],
        out_specs=pl.BlockSpec((None, None, W_BR, L), lambda l, o, s: (l, o, 0, s)),
        compiler_params=_params(("parallel", "parallel", "parallel"), 48),
        name="hy_filters",
    )(z, w1t, b1, w2t, b2, w3t, delta, bias)


HY_PRE_NB = 8


def _hy_pre_kernel(p_ref, w_ref, b_ref, o_ref):
    L = p_ref.shape[1]
    w = w_ref[...]
    row = lax.broadcasted_iota(jnp.int32, (L, LANES), 0)

    def body(bi, carry):
        x = p_ref[bi].astype(F32)
        xm = jnp.where(row == 0, 0.0, pltpu.roll(x, 1, 0))
        xp = jnp.where(row == L - 1, 0.0, pltpu.roll(x, L - 1, 0))
        u = b_ref[...] + w[0:1] * xm + w[1:2] * x + w[2:3] * xp
        o_ref[:, pl.ds(pl.multiple_of(bi * L, LANES), L)] = u.T.astype(BF16)
        return carry

    lax.fori_loop(0, HY_PRE_NB, body, 0)


def _hy_pre(proj3, conv_w, conv_b):
    B, L, _ = proj3.shape
    nc = 3 * W_BR // LANES
    return pl.pallas_call(
        _hy_pre_kernel,
        out_shape=jax.ShapeDtypeStruct((3 * W_BR, B * L), BF16),
        grid=(B // HY_PRE_NB, nc),
        in_specs=[pl.BlockSpec((HY_PRE_NB, L, LANES), lambda b, c: (b, 0, COL_HY // LANES + c)),
                  pl.BlockSpec((3, LANES), lambda b, c: (0, c)),
                  pl.BlockSpec((1, LANES), lambda b, c: (0, c))],
        out_specs=pl.BlockSpec((LANES, HY_PRE_NB * L), lambda b, c: (c, b)),
        compiler_params=_params(("parallel", "parallel"), 48),
        name="hy_pre",
    )(proj3, conv_w, conv_b)


HY_CB = 16
HY_KT = 2 * LANES
BF16_ROWS = 16


def _hy_build_tables(kk_ref, ci, g0_ref, g1_ref, L):
    nchunk = 2 * L // LANES
    upper = (lax.broadcasted_iota(jnp.int32, (LANES, LANES), 1)
             >= lax.broadcasted_iota(jnp.int32, (LANES, LANES), 0))

    def rolled(q):
        lo = 2 * LANES * (q // 2)
        half = slice(LANES * (q % 2), LANES * (q % 2 + 1))
        s0 = kk_ref[0, pl.ds(ci, 1), lo:lo + 2 * LANES][:, half]
        s1 = kk_ref[1, pl.ds(ci, 1), lo:lo + 2 * LANES][:, half]
        b0 = lax.bitcast_convert_type(s0.astype(BF16).astype(F32), jnp.uint32)
        b1 = lax.bitcast_convert_type(s1.astype(BF16).astype(F32), jnp.uint32)
        return pltpu.roll(jnp.broadcast_to(b0 | (b1 >> 16), (LANES, LANES)), 0, 1, stride=1, stride_axis=0)

    prev = rolled(nchunk - 1)
    for m in range(nchunk - 1):
        cur = rolled(nchunk - 2 - m)
        r = jnp.where(upper, prev, cur)
        rows = slice(LANES * m, LANES * (m + 1))
        g0_ref[rows, :] = lax.bitcast_convert_type(r & jnp.uint32(0xFFFF0000), F32).astype(BF16)
        g1_ref[rows, :] = lax.bitcast_convert_type(r << 16, F32).astype(BF16)
        prev = cur


def _hy_toeplitz(u, g_ref, u_ref, acc_ref, L):
    B = u.shape[0]
    nsb = L // HY_KT
    bp = u_ref.shape[0] // nsb
    pad = jnp.zeros((bp - B, HY_KT), F32)
    for sb in range(nsb):
        blk = u[:, HY_KT * sb:HY_KT * (sb + 1)]
        if bp > B:
            blk = jnp.concatenate([blk, pad], axis=0)
        u_ref[bp * sb:bp * (sb + 1), :] = blk.astype(BF16)
    acc_ref[...] = jnp.zeros(acc_ref.shape, F32)
    for delta in range(-(nsb - 1), nsb):
        d0 = HY_KT * delta + L - LANES
        tile = jnp.concatenate([g_ref[d0:d0 + HY_KT, :], g_ref[d0 - LANES:d0 - LANES + HY_KT, :]], axis=1)
        lo, hi = max(0, delta), min(nsb - 1, nsb - 1 + delta)
        part = jnp.dot(u_ref[bp * lo:bp * (hi + 1), :], tile, preferred_element_type=F32)
        acc_ref[bp * (lo - delta):bp * (hi - delta + 1), :] += part
    return jnp.concatenate([acc_ref[bp * tb:bp * tb + B, :] for tb in range(nsb)], axis=1)


def _hy_conv_kernel(z_ref, x1_ref, x2_ref, kk_ref, o_ref, g_ref, u_ref, acc_ref):
    L = z_ref.shape[2]

    def compute(ci, slot):
        y0 = _hy_toeplitz(z_ref[ci].astype(F32), g_ref.at[slot, 0], u_ref, acc_ref, L)
        z1 = x1_ref[ci].astype(F32) * y0
        y1 = _hy_toeplitz(z1, g_ref.at[slot, 1], u_ref, acc_ref, L)
        o_ref[ci] = (x2_ref[ci].astype(F32) * y1).astype(BF16)

    def build(ci, slot):
        _hy_build_tables(kk_ref, ci, g_ref.at[slot, 0], g_ref.at[slot, 1], L)

    build(0, 0)

    def body(k, carry):
        build(2 * k + 1, 1)
        compute(2 * k, 0)
        build(jnp.minimum(2 * k + 2, HY_CB - 1), 0)
        compute(2 * k + 1, 1)
        return carry

    lax.fori_loop(0, HY_CB // 2, body, 0)


def _hy_conv(zc3, kk):
    _, B, L = zc3.shape
    nblk = W_BR // HY_CB
    bp = -(-B // BF16_ROWS) * BF16_ROWS
    nsb = L // HY_KT
    act = lambda off: pl.BlockSpec((HY_CB, B, L), lambda c: (c + off * nblk, 0, 0))
    return pl.pallas_call(
        _hy_conv_kernel,
        out_shape=jax.ShapeDtypeStruct((W_BR, B, L), BF16),
        grid=(nblk,),
        in_specs=[act(0), act(1), act(2),
                  pl.BlockSpec((2, HY_CB, 2 * L), lambda c: (0, c, 0))],
        out_specs=pl.BlockSpec((HY_CB, B, L), lambda c: (c, 0, 0)),
        scratch_shapes=[pltpu.VMEM((2, 2, 2 * L - LANES, LANES), BF16),
                        pltpu.VMEM((nsb * bp, HY_KT), BF16),
                        pltpu.VMEM((nsb * bp, HY_KT), F32)],
        compiler_params=_params(("parallel",), 48),
        name="hy_conv",
    )(zc3, zc3, zc3, kk)


HY_POST_TT = 8192


def _hy_post_kernel(y_ref, o_ref):
    o_ref[...] = y_ref[...].astype(F32).T.astype(BF16)


def _hy_post(yc):
    C, T = yc.shape
    return pl.pallas_call(
        _hy_post_kernel,
        out_shape=jax.ShapeDtypeStruct((T, C), BF16),
        grid=(T // HY_POST_TT, C // LANES),
        in_specs=[pl.BlockSpec((LANES, HY_POST_TT), lambda t, c: (c, t))],
        out_specs=pl.BlockSpec((HY_POST_TT, LANES), lambda t, c: (t, c)),
        compiler_params=_params(("parallel", "parallel"), 48),
        name="hy_post",
    )(yc)


def _sgu_kernel(u_ref, v_ref, g_ref, b_ref, ws_ref, bs_ref, o_ref):
    L = u_ref.shape[0]
    gw = W_BR // SGU_GROUPS

    def body(n, carry):
        rows = pl.ds(pl.multiple_of(n * SGU_CHUNK, SGU_CHUNK), SGU_CHUNK)
        v = v_ref[rows, :].astype(F32)
        mu = jnp.mean(v, axis=-1, keepdims=True)
        d = v - mu
        var = jnp.mean(d * d, axis=-1, keepdims=True)
        vn = (d * lax.rsqrt(var + EPS) * g_ref[...] + b_ref[...]).astype(BF16)
        mixed = jnp.concatenate(
            [jnp.dot(ws_ref[k], vn[:, gw * k:gw * (k + 1)], preferred_element_type=F32) + bs_ref[k]
             for k in range(SGU_GROUPS)], axis=1)
        o_ref[rows, :] = (u_ref[rows, :].astype(F32) * mixed).astype(BF16)
        return carry

    lax.fori_loop(0, L // SGU_CHUNK, body, 0)


def _sgu(proj3, ln_g, ln_b, ws, bs):
    B, L, _ = proj3.shape
    cb = COL_SGU // W_BR
    return pl.pallas_call(
        _sgu_kernel,
        out_shape=jax.ShapeDtypeStruct((B, L, W_BR), BF16),
        grid=(B,),
        in_specs=[pl.BlockSpec((None, L, W_BR), lambda b: (b, 0, cb)),
                  pl.BlockSpec((None, L, W_BR), lambda b: (b, 0, cb + 1)),
                  pl.BlockSpec((1, W_BR), lambda b: (0, 0)),
                  pl.BlockSpec((1, W_BR), lambda b: (0, 0)),
                  pl.BlockSpec((SGU_GROUPS, SGU_CHUNK, SGU_CHUNK), lambda b: (0, 0, 0)),
                  pl.BlockSpec((SGU_GROUPS, SGU_CHUNK, 1), lambda b: (0, 0, 0))],
        out_specs=pl.BlockSpec((None, L, W_BR), lambda b: (b, 0, 0)),
        compiler_params=_params(("parallel",), 48),
        name="sgu",
    )(proj3, proj3, ln_g, ln_b, ws, bs)


RG_SLABS = 2
RG_TL = 256
RG_PAD = 8
RG_SKEW = 8
RG_UNROLL = 8


def _rg_kernel(xb_ref, gb_ref, cw_ref, cb_ref, w_ref, bias_ref, lam_ref, o_ref,
               xpad_ref, nat_ref, xi_ref, af_ref, bf_ref, ab_ref, bb_ref):
    L = xb_ref.shape[0]
    seg = L // SUBLANES
    pitch = seg + RG_SKEW
    zeros = jnp.zeros((RG_PAD, LANES), F32)
    sp_all = []
    for s in range(RG_SLABS):
        cols = slice(s * LANES, (s + 1) * LANES)
        xpad_ref[s, 0:RG_PAD, :] = zeros
        xpad_ref[s, RG_PAD + L:RG_PAD + L + RG_PAD, :] = zeros
        xpad_ref[s, RG_PAD:RG_PAD + L, :] = xb_ref[:, cols].astype(F32)
        cw = cw_ref[:, cols]
        for j in range(SUBLANES):
            t0 = j * seg
            xc = cb_ref[:, cols]
            for k in range(4):
                xc = xc + cw[k:k + 1] * xpad_ref[s, RG_PAD + t0 + k - 2:RG_PAD + t0 + k - 2 + seg, :]
            nat_ref[s, pitch * j:pitch * j + seg, :] = xc
        lam = lam_ref[:, cols]
        sp_all.append(jnp.maximum(-lam, 0.0) + jnp.log(1.0 + jnp.exp(-jnp.abs(lam))))

    def interleave(r, carry):
        dst = pl.ds(pl.multiple_of(r * SUBLANES, SUBLANES), SUBLANES)
        for s in range(RG_SLABS):
            xi_ref[s, dst, :] = nat_ref[s, pl.ds(r, SUBLANES, stride=pitch), :]
        return carry

    lax.fori_loop(0, seg, interleave, 0, unroll=RG_UNROLL)

    for s in range(RG_SLABS):
        for ti in range(L // RG_TL):
            rows = slice(ti * RG_TL, (ti + 1) * RG_TL)
            xc = xi_ref[s, rows, :]
            g = jnp.dot(xc.astype(BF16), w_ref[s], preferred_element_type=F32) + bias_ref[s]
            g = 0.5 * jnp.tanh(0.5 * g) + 0.5
            for d, (a_ref, b_ref) in enumerate(((af_ref, bf_ref), (ab_ref, bb_ref))):
                r = g[:, 2 * d * LANES:(2 * d + 1) * LANES]
                i = g[:, (2 * d + 1) * LANES:(2 * d + 2) * LANES]
                a = jnp.exp((-RG_C) * r * sp_all[s][d:d + 1])
                om = 1.0 - a * a
                a_ref[s, rows, :] = a
                b_ref[s, rows, :] = (om * lax.rsqrt(jnp.maximum(om, 1e-30))) * (i * xc)

    def rows_f(r):
        return pl.ds(pl.multiple_of(r * SUBLANES, SUBLANES), SUBLANES)

    def rows_b(r):
        return pl.ds(pl.multiple_of((seg - 1 - r) * SUBLANES, SUBLANES), SUBLANES)

    def pass1(r, carry):
        out = []
        for s in range(RG_SLABS):
            hf, pf, hb, pb = carry[4 * s:4 * s + 4]
            a = af_ref[s, rows_f(r), :]
            hf = a * hf + bf_ref[s, rows_f(r), :]
            pf = a * pf
            a = ab_ref[s, rows_b(r), :]
            hb = a * hb + bb_ref[s, rows_b(r), :]
            pb = a * pb
            out += [hf, pf, hb, pb]
        return tuple(out)

    z = jnp.zeros((SUBLANES, LANES), F32)
    one = jnp.ones((SUBLANES, LANES), F32)
    ends = lax.fori_loop(0, seg, pass1, (z, one, z, one) * RG_SLABS, unroll=RG_UNROLL)

    init = []
    for s in range(RG_SLABS):
        hf, pf, hb, pb = ends[4 * s:4 * s + 4]
        c = jnp.zeros((1, LANES), F32)
        rows = []
        for j in range(SUBLANES):
            rows.append(c)
            c = pf[j:j + 1] * c + hf[j:j + 1]
        init.append(jnp.concatenate(rows, axis=0))
        c = jnp.zeros((1, LANES), F32)
        rows = []
        for j in range(SUBLANES - 1, -1, -1):
            rows.append(c)
            c = pb[j:j + 1] * c + hb[j:j + 1]
        init.append(jnp.concatenate(rows[::-1], axis=0))

    def pass2(r, carry):
        out = []
        for s in range(RG_SLABS):
            hf, hb = carry[2 * s:2 * s + 2]
            hf = af_ref[s, rows_f(r), :] * hf + bf_ref[s, rows_f(r), :]
            xi_ref[s, rows_f(r), :] = hf
            hb = ab_ref[s, rows_b(r), :] * hb + bb_ref[s, rows_b(r), :]
            xpad_ref[s, rows_b(r), :] = hb
            out += [hf, hb]
        return tuple(out)

    lax.fori_loop(0, seg, pass2, tuple(init), unroll=RG_UNROLL)

    def deinterleave(r, carry):
        src = rows_f(r)
        for s in range(RG_SLABS):
            nat_ref[s, pl.ds(r, SUBLANES, stride=pitch), :] = xi_ref[s, src, :] + xpad_ref[s, src, :]
        return carry

    lax.fori_loop(0, seg, deinterleave, 0, unroll=RG_UNROLL)

    for s in range(RG_SLABS):
        cols = slice(s * LANES, (s + 1) * LANES)
        for j in range(SUBLANES):
            gb = gb_ref[j * seg:(j + 1) * seg, cols].astype(F32)
            gelu = 0.5 * gb * (1.0 + jnp.tanh(math.sqrt(2.0 / math.pi) * (gb + 0.044715 * (gb * gb * gb))))
            o_ref[j * seg:(j + 1) * seg, cols] = (nat_ref[s, pitch * j:pitch * j + seg, :] * gelu).astype(BF16)


def _rglru(proj3, conv_w, conv_b, w_bd, bias_bd, lam):
    B, L, _ = proj3.shape
    gw = RG_SLABS * LANES
    ng = W_BR // gw
    cx = COL_RG // gw
    seg = L // SUBLANES
    slab = lambda rows: pltpu.VMEM((RG_SLABS, rows, LANES), F32)
    return pl.pallas_call(
        _rg_kernel,
        out_shape=jax.ShapeDtypeStruct((B, L, W_BR), BF16),
        grid=(B, ng),
        in_specs=[pl.BlockSpec((None, L, gw), lambda b, c: (b, 0, cx + c)),
                  pl.BlockSpec((None, L, gw), lambda b, c: (b, 0, cx + ng + c)),
                  pl.BlockSpec((4, gw), lambda b, c: (0, c)),
                  pl.BlockSpec((1, gw), lambda b, c: (0, c)),
                  pl.BlockSpec((RG_SLABS, LANES, 4 * LANES), lambda b, c: (c, 0, 0)),
                  pl.BlockSpec((RG_SLABS, 1, 4 * LANES), lambda b, c: (c, 0, 0)),
                  pl.BlockSpec((2, gw), lambda b, c: (0, c))],
        out_specs=pl.BlockSpec((None, L, gw), lambda b, c: (b, 0, c)),
        scratch_shapes=[slab(L + 2 * RG_PAD), slab(SUBLANES * (seg + RG_SKEW))] + [slab(L)] * 5,
        compiler_params=_params(("parallel", "parallel"), 48),
        name="rglru",
    )(proj3, proj3, conv_w, conv_b, w_bd, bias_bd, lam)


def _log_sigmoid(x):
    return jnp.minimum(x, 0.0) - jnp.log(1.0 + jnp.exp(-jnp.abs(x)))


ML_AUG = 16


def _ml_kernel(qt_ref, vt_ref, ot_ref, k_ref, gtt_ref, gbias_ref, ng_ref, yt_ref,
               hf_ref, hb_ref, c_ref, m_ref, li_ref, b_ref, cc_ref):
    L = k_ref.shape[0]
    nc = L // ML_CHUNK
    ng2 = 2 * ML_HEADS
    row = lax.broadcasted_iota(jnp.int32, (ML_CHUNK, ML_CHUNK), 0)
    col = lax.broadcasted_iota(jnp.int32, (ML_CHUNK, ML_CHUNK), 1)
    scale = ML_HD ** -0.5
    masks = (col <= row, col >= row)
    c_ref[...] = jnp.zeros(c_ref.shape, F32)
    m_ref[...] = jnp.zeros(m_ref.shape, F32)

    li = gtt_ref[0:ng2, :] + gbias_ref[0:ng2, :]
    lf = _log_sigmoid(gtt_ref[ng2:2 * ng2, :] + gbias_ref[ng2:2 * ng2, :])
    pos = lax.broadcasted_iota(jnp.int32, (ng2, L), 1) & (ML_CHUNK - 1)
    pre_sum = lf
    suf_sum = lf
    k = 1
    while k < ML_CHUNK:
        pre_sum = pre_sum + jnp.where(pos >= k, pltpu.roll(pre_sum, k, 1), 0.0)
        suf_sum = suf_sum + jnp.where(pos < ML_CHUNK - k, pltpu.roll(suf_sum, L - k, 1), 0.0)
        k *= 2
    causal_row = lax.broadcasted_iota(jnp.int32, (ng2, L), 0) < ML_HEADS
    b_all = jnp.where(causal_row, pre_sum, suf_sum)
    li_ref[...] = li
    b_ref[...] = b_all
    diff = jnp.concatenate([li - b_all, jnp.zeros((LANES - ng2, L), F32)], axis=0)
    for c in range(nc):
        cc_ref[c * ML_CHUNK:(c + 1) * ML_CHUNK, :] = diff[:, c * ML_CHUNK:(c + 1) * ML_CHUNK].T

    def body(step, carry):
        for d in range(2):
            c = step if d == 0 else nc - 1 - step
            tsl = pl.ds(pl.multiple_of(c * ML_CHUNK, ML_CHUNK), ML_CHUNK)
            mask_st = masks[1 - d]
            last = ML_CHUNK - 1 if d == 0 else 0
            for hd in range(ML_HEADS):
                r = d * ML_HEADS + hd
                b_row = b_ref[r:r + 1, tsl]
                li_row = li_ref[r:r + 1, tsl]
                c_col = cc_ref[tsl, r:r + 1]
                g_tot = b_row[:, last:last + 1]
                m_prev = m_ref[d, hd]
                dlog = jnp.where(mask_st, b_row + c_col, -jnp.inf)
                inter = b_row + m_prev
                m_t = jnp.maximum(inter, jnp.max(dlog, axis=0, keepdims=True))
                w_intra = jnp.exp(dlog - m_t)
                w_inter = jnp.exp(inter - m_t)
                hs = slice(hd * ML_HD, (hd + 1) * ML_HD)
                qt = qt_ref[hs, tsl]
                kh = k_ref[tsl, hs]
                vt = vt_ref[hs, tsl]
                s = jnp.dot(kh, qt, preferred_element_type=F32) * (scale * w_intra)
                ca = c_ref[d, hd]
                qc = jnp.dot(ca.astype(BF16), qt, preferred_element_type=F32) * scale
                num = w_inter * qc[0:ML_HD] + jnp.dot(vt, s.astype(BF16), preferred_element_type=F32)
                den = w_inter * qc[ML_HD:ML_HD + 1] + jnp.sum(s, axis=0, keepdims=True)
                hout = num * (1.0 / jnp.maximum(jnp.abs(den), jnp.exp(-m_t)))
                if d == 0:
                    hf_ref[hs, tsl] = hout
                else:
                    hb_ref[hs, tsl] = hout
                wlog = g_tot - b_row + li_row
                m_new = jnp.maximum(g_tot + m_prev, jnp.max(wlog, axis=1, keepdims=True))
                decay = jnp.exp(g_tot + m_prev - m_new)
                ws = jnp.exp(wlog - m_new)
                vw = jnp.concatenate([vt.astype(F32) * ws, jnp.broadcast_to(ws, (ML_AUG, ML_CHUNK))],
                                     axis=0).astype(BF16)
                c_ref[d, hd] = decay * ca + jnp.dot(vw, kh, preferred_element_type=F32)
                m_ref[d, hd] = m_new
        return carry

    lax.fori_loop(0, nc, body, 0)

    def finish(c, carry):
        tsl = pl.ds(pl.multiple_of(c * ML_CHUNK, ML_CHUNK), ML_CHUNK)
        for hd in range(ML_HEADS):
            hs = slice(hd * ML_HD, (hd + 1) * ML_HD)
            h = hf_ref[hs, tsl] + hb_ref[hs, tsl]
            hn = h * lax.rsqrt(jnp.mean(h * h, axis=0, keepdims=True) + EPS) * ng_ref[hs, :]
            yt_ref[hs, tsl] = (_sigmoid(ot_ref[hs, tsl].astype(F32)) * hn).astype(BF16)
        return carry

    lax.fori_loop(0, nc, finish, 0)


def _mlstm(qvot, proj3, mlgt, gbias, norm_g):
    B, L, _ = proj3.shape
    cb = COL_ML // W_BR
    ng4 = 4 * ML_HEADS
    fm = lambda j: pl.BlockSpec((W_BR, L), lambda b: (j, b))
    return pl.pallas_call(
        _ml_kernel,
        out_shape=jax.ShapeDtypeStruct((W_BR, B * L), BF16),
        grid=(B,),
        in_specs=[fm(0), fm(1), fm(2),
                  pl.BlockSpec((None, L, W_BR), lambda b: (b, 0, cb)),
                  pl.BlockSpec((LANES, L), lambda b: (0, b)),
                  pl.BlockSpec((ng4, 1), lambda b: (0, 0)),
                  pl.BlockSpec((W_BR, LANES), lambda b: (0, 0))],
        out_specs=pl.BlockSpec((W_BR, L), lambda b: (0, b)),
        scratch_shapes=[pltpu.VMEM((W_BR, L), F32),
                        pltpu.VMEM((W_BR, L), F32),
                        pltpu.VMEM((2, ML_HEADS, ML_HD + ML_AUG, ML_HD), F32),
                        pltpu.VMEM((2, ML_HEADS, 1, 1), F32),
                        pltpu.VMEM((2 * ML_HEADS, L), F32),
                        pltpu.VMEM((2 * ML_HEADS, L), F32),
                        pltpu.VMEM((L, LANES), F32)],
        compiler_params=_params(("parallel",), 48),
        name="mlstm",
    )(qvot, qvot, qvot, proj3, mlgt, gbias, jnp.broadcast_to(norm_g.reshape(W_BR, 1), (W_BR, LANES)))


def _merge_kernel(route, gate_ref, ya_ref, yb_ref, yc_ref, yd_ref, wb_ref, wo_ref, x_ref, g_ref, *rest):
    if route:
        rw_ref, rb_ref, xo_ref, hn_ref, comb_ref, combt_ref = rest
    else:
        xo_ref, hn_ref = rest
    merged = None
    for k, y_ref in enumerate((ya_ref, yb_ref, yc_ref, yd_ref)):
        t = jnp.dot(y_ref[...], wb_ref[k], preferred_element_type=F32)
        gk = _sigmoid(gate_ref[:, k * D_MODEL:(k + 1) * D_MODEL].astype(F32))
        merged = gk * t if merged is None else merged + gk * t
    xn = x_ref[...] + jnp.dot(merged.astype(BF16), wo_ref[...], preferred_element_type=F32)
    xo_ref[...] = xn
    h = _rms(xn, g_ref[...])
    h_hi = h.astype(BF16)
    hn_ref[...] = h_hi
    if route:
        h_lo = (h - h_hi.astype(F32)).astype(BF16)
        r_hi = jnp.dot(h_hi, rw_ref[...], preferred_element_type=F32)
        r_lo = jnp.dot(h_lo, rw_ref[:, 0:LANES], preferred_element_type=F32)
        logits = r_hi[:, 0:LANES] + r_hi[:, LANES:2 * LANES] + r_lo + rb_ref[...]
        lane = lax.broadcasted_iota(jnp.int32, logits.shape, 1)
        logits = jnp.where(lane < N_EXPERTS, logits, -jnp.inf)
        v1 = jnp.max(logits, axis=1, keepdims=True)
        i1 = jnp.min(jnp.where(logits == v1, lane, LANES), axis=1, keepdims=True)
        rest_l = jnp.where(lane == i1, -jnp.inf, logits)
        v2 = jnp.max(rest_l, axis=1, keepdims=True)
        i2 = jnp.min(jnp.where(rest_l == v2, lane, LANES), axis=1, keepdims=True)
        e2 = jnp.exp(v2 - v1)
        p1 = 1.0 / (1.0 + e2)
        comb = jnp.where(lane == i1, p1, jnp.where(lane == i2, e2 * p1, 0.0))
        comb_ref[...] = comb
        combt_ref[...] = comb.T


def _merge(proj, ya, yb, yc, yd, wb, wo, x, g, router=None, tm=512):
    T = x.shape[0]
    row = lambda w: pl.BlockSpec((tm, w), lambda i: (i, 0))
    in_specs = [row(N_BRANCH * D_MODEL), row(W_BR), row(W_BR), row(W_BR), row(W_BR),
                pl.BlockSpec((N_BRANCH, W_BR, D_MODEL), lambda i: (0, 0, 0)),
                pl.BlockSpec((D_MODEL, D_MODEL), lambda i: (0, 0)),
                row(D_MODEL),
                pl.BlockSpec((1, D_MODEL), lambda i: (0, 0))]
    out_shape = [jax.ShapeDtypeStruct((T, D_MODEL), F32), jax.ShapeDtypeStruct((T, D_MODEL), BF16)]
    out_specs = [row(D_MODEL), row(D_MODEL)]
    args = [proj, ya, yb, yc, yd, wb, wo, x, g]
    if router is not None:
        in_specs += [pl.BlockSpec((D_MODEL, 2 * LANES), lambda i: (0, 0)),
                     pl.BlockSpec((1, LANES), lambda i: (0, 0))]
        out_shape += [jax.ShapeDtypeStruct((T, LANES), F32), jax.ShapeDtypeStruct((LANES, T), F32)]
        out_specs += [row(LANES), pl.BlockSpec((LANES, tm), lambda i: (0, i))]
        args += list(router)
    return pl.pallas_call(
        functools.partial(_merge_kernel, router is not None),
        out_shape=tuple(out_shape),
        grid=(T // tm,),
        in_specs=in_specs,
        out_specs=tuple(out_specs),
        compiler_params=_params(("parallel",), 56),
        name="merge_route" if router is not None else "merge",
    )(*args)


def _swiglu_acc(h, w1_ref, w3_ref, w2_ref):
    a = jnp.dot(h, w1_ref[...], preferred_element_type=F32)
    b = jnp.dot(h, w3_ref[...], preferred_element_type=F32)
    act = (a * _sigmoid(a) * b).astype(BF16)
    return jnp.dot(act, w2_ref[...], preferred_element_type=F32)


def _ffn_kernel(h_ref, x_ref, w1_ref, w3_ref, w2_ref, o_ref, acc_ref):
    f = pl.program_id(1)

    @pl.when(f == 0)
    def _():
        acc_ref[...] = x_ref[...]

    acc_ref[...] += _swiglu_acc(h_ref[...], w1_ref, w3_ref, w2_ref)

    @pl.when(f == pl.num_programs(1) - 1)
    def _():
        o_ref[...] = acc_ref[...]


def _ffn(hn, x, w1, w3, w2, tm=512, n_split=2):
    T = x.shape[0]
    dff = w1.shape[1]
    tf = dff // n_split
    return pl.pallas_call(
        _ffn_kernel,
        out_shape=jax.ShapeDtypeStruct((T, D_MODEL), F32),
        grid=(T // tm, n_split),
        in_specs=[pl.BlockSpec((tm, D_MODEL), lambda i, f: (i, 0)),
                  pl.BlockSpec((tm, D_MODEL), lambda i, f: (i, 0)),
                  pl.BlockSpec((D_MODEL, tf), lambda i, f: (0, f)),
                  pl.BlockSpec((D_MODEL, tf), lambda i, f: (0, f)),
                  pl.BlockSpec((tf, D_MODEL), lambda i, f: (f, 0))],
        out_specs=pl.BlockSpec((tm, D_MODEL), lambda i, f: (i, 0)),
        scratch_shapes=[pltpu.VMEM((tm, D_MODEL), F32)],
        compiler_params=_params(("parallel", "arbitrary"), 56),
        name="ffn",
    )(hn, x, w1, w3, w2)


MOE_TM = 1024
MOE_CH = 256


def _moe_kernel(cnt_ref, h_ref, x_ref, comb_ref, combt_ref, w1_ref, w3_ref, w2_ref, o_ref,
                rank_ref, rankt_ref):
    i = pl.program_id(0)
    e = pl.program_id(1)
    tm = h_ref.shape[0]

    @pl.when(e == 0)
    def _():
        o_ref[...] = x_ref[...]
        r = lax.broadcasted_iota(jnp.int32, (tm, tm), 0)
        c = lax.broadcasted_iota(jnp.int32, (tm, tm), 1)
        before = jnp.where(c < r, 1.0, 0.0).astype(BF16)
        sel = jnp.where(comb_ref[...] > 0.0, 1.0, 0.0).astype(BF16)
        rank_ref[...] = jnp.dot(before, sel, preferred_element_type=F32)
        selt = jnp.where(combt_ref[...] > 0.0, 1.0, 0.0).astype(BF16)
        rankt_ref[...] = lax.dot_general(selt, before, (((1,), (1,)), ((), ())),
                                         preferred_element_type=F32)

    lane = lax.broadcasted_iota(jnp.int32, (tm, LANES), 1)
    comb = comb_ref[...]
    c_col = jnp.sum(jnp.where(lane == e, comb, 0.0), axis=1, keepdims=True)
    rank_col = jnp.sum(jnp.where(lane == e, rank_ref[...], 0.0), axis=1, keepdims=True)
    rank_col = jnp.where(c_col > 0.0, rank_col, -1.0)
    c_row = combt_ref[pl.ds(e, 1), :]
    rank_row = jnp.where(c_row > 0.0, rankt_ref[pl.ds(e, 1), :], -1.0)
    cnt = cnt_ref[i * N_EXPERTS + e]

    def run_chunk(base, size):
        pos_r = (lax.broadcasted_iota(jnp.int32, (size, tm), 0) + base).astype(F32)
        gather = jnp.where(rank_row == pos_r, 1.0, 0.0).astype(BF16)
        xs = jnp.dot(gather, h_ref[...], preferred_element_type=F32).astype(BF16)
        y = _swiglu_acc(xs, w1_ref, w3_ref, w2_ref).astype(BF16)
        pos_c = (lax.broadcasted_iota(jnp.int32, (tm, size), 1) + base).astype(F32)
        scatter = jnp.where(rank_col == pos_c, c_col, 0.0).astype(BF16)
        o_ref[...] += jnp.dot(scatter, y, preferred_element_type=F32)

    n_full = cnt // MOE_CH
    rem = cnt - n_full * MOE_CH

    def body(j, carry):
        run_chunk(j * MOE_CH, MOE_CH)
        return carry

    lax.fori_loop(0, n_full, body, 0)

    @pl.when(rem > MOE_CH // 2)
    def _():
        run_chunk(n_full * MOE_CH, MOE_CH)

    @pl.when((rem > 0) & (rem <= MOE_CH // 2))
    def _():
        run_chunk(n_full * MOE_CH, MOE_CH // 2)


def _moe(hn, x, comb, combt, w1, w3, w2, tm=MOE_TM):
    T = x.shape[0]
    dfe = w1.shape[2]
    nt = T // tm
    cnt = jnp.sum((comb[:, :N_EXPERTS] > 0.0).reshape(nt, tm, N_EXPERTS), axis=1, dtype=jnp.int32).reshape(-1)
    grid_spec = pltpu.PrefetchScalarGridSpec(
        num_scalar_prefetch=1,
        grid=(nt, N_EXPERTS),
        in_specs=[pl.BlockSpec((tm, D_MODEL), lambda i, e, c: (i, 0)),
                  pl.BlockSpec((tm, D_MODEL), lambda i, e, c: (i, 0)),
                  pl.BlockSpec((tm, LANES), lambda i, e, c: (i, 0)),
                  pl.BlockSpec((LANES, tm), lambda i, e, c: (0, i)),
                  pl.BlockSpec((None, D_MODEL, dfe), lambda i, e, c: (e, 0, 0)),
                  pl.BlockSpec((None, D_MODEL, dfe), lambda i, e, c: (e, 0, 0)),
                  pl.BlockSpec((None, dfe, D_MODEL), lambda i, e, c: (e, 0, 0))],
        out_specs=pl.BlockSpec((tm, D_MODEL), lambda i, e, c: (i, 0)),
        scratch_shapes=[pltpu.VMEM((tm, LANES), F32), pltpu.VMEM((LANES, tm), F32)])
    return pl.pallas_call(
        _moe_kernel,
        out_shape=jax.ShapeDtypeStruct((T, D_MODEL), F32),
        grid_spec=grid_spec,
        compiler_params=_params(("parallel", "arbitrary"), 56),
        name="moe",
    )(cnt, hn, x, comb, combt, w1, w3, w2)


def _final_norm_kernel(x_ref, g_ref, o_ref):
    o_ref[...] = _rms(x_ref[...], g_ref[...])


def _final_norm(x, g, row0, rows, tm=1024):
    first = row0 // tm
    return pl.pallas_call(
        _final_norm_kernel,
        out_shape=jax.ShapeDtypeStruct((rows, D_MODEL), F32),
        grid=(rows // tm,),
        in_specs=[pl.BlockSpec((tm, D_MODEL), lambda i: (i + first, 0)),
                  pl.BlockSpec((1, D_MODEL), lambda i: (0, 0))],
        out_specs=pl.BlockSpec((tm, D_MODEL), lambda i: (i, 0)),
        compiler_params=_params(("parallel",), 48),
        name="final_norm",
    )(x, g)


def _split_w_in(w_in_l):
    n_hy, n_sgu, n_rg = 3 * W_BR, 2 * W_BR, 2 * W_BR
    o1 = n_hy
    o2 = o1 + n_sgu
    o3 = o2 + n_rg
    o4 = o3 + 4 * W_BR
    o5 = o4 + 4 * ML_HEADS
    w = jnp.concatenate([w_in_l[:, o5:], w_in_l[:, :o3], w_in_l[:, o3 + W_BR:o3 + 2 * W_BR]],
                        axis=1).astype(BF16)
    wgt = jnp.pad(w_in_l[:, o4:o5], ((0, 0), (0, LANES - 4 * ML_HEADS))).T.astype(BF16)
    wft = jnp.concatenate([w_in_l[:, o3:o3 + W_BR], w_in_l[:, o3 + 2 * W_BR:o4]], axis=1).T.astype(BF16)
    return w, wgt, wft


def _router_operands(router_w, router_b):
    w = jnp.pad(router_w, ((0, 0), (0, LANES - N_EXPERTS)))
    w_hi = w.astype(BF16)
    w_lo = (w - w_hi.astype(F32)).astype(BF16)
    return (jnp.concatenate([w_hi, w_lo], axis=1),
            jnp.pad(router_b, (0, LANES - N_EXPERTS))[None, :])


def _rg_block_diag(wa, ba, wx, bx):
    hpg = LANES // RG_HD
    ng = RG_HEADS // hpg
    eye = jnp.eye(hpg, dtype=F32)

    def bd(w):
        w = w.reshape(ng, hpg, RG_HD, RG_HD)
        return jnp.einsum('gaij,ab->gaibj', w, eye).reshape(ng, LANES, LANES)

    w = jnp.concatenate([bd(wa[0]), bd(wx[0]), bd(wa[1]), bd(wx[1])], axis=2).astype(BF16)
    fl = lambda b: b.reshape(ng, 1, LANES)
    bias = jnp.concatenate([fl(ba[0]), fl(bx[0]), fl(ba[1]), fl(bx[1])], axis=2)
    return w, bias


def _token_mixer(x, B, L, l, norm_g, w_in, kk, hy_conv_w, hy_conv_b, sgu_ln_g, sgu_ln_b, sgu_ws, sgu_bs,
                 rg_conv_w, rg_conv_b, rg_wa, rg_ba, rg_wx, rg_bx, rg_lambda, ml_i_bias, ml_f_bias,
                 ml_norm_g, w_branch, w_out, next_g, router):
    T = B * L
    w, wgt, wft = _split_w_in(w_in[l])
    proj, mlgt, qvot = _inproj(x, norm_g[l][None, :], w, wgt, wft)
    proj3 = proj.reshape(B, L, N_PROJ)

    zc = _hy_pre(proj3, hy_conv_w[l], hy_conv_b[l][None, :])
    ya_c = _hy_conv(zc.reshape(3 * W_BR, B, L), kk[l])
    ya = _hy_post(ya_c.reshape(W_BR, T))

    yb = _sgu(proj3, sgu_ln_g[l][None, :], sgu_ln_b[l][None, :], sgu_ws[l].astype(BF16),
              sgu_bs[l][:, :, None]).reshape(T, W_BR)

    w_bd, bias_bd = _rg_block_diag(rg_wa[l], rg_ba[l], rg_wx[l], rg_bx[l])
    yc = _rglru(proj3, rg_conv_w[l], rg_conv_b[l][None, :], w_bd, bias_bd, rg_lambda[l]).reshape(T, W_BR)

    gbias = jnp.concatenate([ml_i_bias[l].reshape(-1), ml_f_bias[l].reshape(-1)])[:, None]
    yd = _hy_post(_mlstm(qvot, proj3, mlgt, gbias, ml_norm_g[l]))

    return _merge(proj, ya, yb, yc, yd, w_branch[l].astype(BF16), w_out[l].astype(BF16), x,
                  next_g[None, :], router)


def kernel(x_prompt, x_sample, norm_mix_g, w_in, hy_conv_w, hy_conv_b, hy_ffn_w1, hy_ffn_b1, hy_ffn_w2, hy_ffn_b2, hy_ffn_w3, hy_bias, sgu_ln_g, sgu_ln_b, sgu_ws, sgu_bs, rg_conv_w, rg_conv_b, rg_wa, rg_ba, rg_wx, rg_bx, rg_lambda, ml_i_bias, ml_f_bias, ml_norm_g, w_branch, w_out, norm_ffn_g, ffn_w1, ffn_w3, ffn_w2, router_w, router_b, moe_w1, moe_w3, moe_w2, norm_final_g):
    bp, L, _ = x_prompt.shape
    bs = x_sample.shape[0]
    B = bp + bs
    depth = w_in.shape[0]
    x = jnp.concatenate([x_prompt, x_sample], axis=0).reshape(B * L, D_MODEL)
    kk = _hy_filters(hy_ffn_w1, hy_ffn_b1, hy_ffn_w2, hy_ffn_b2, hy_ffn_w3, hy_bias, L)
    for l in range(depth):
        j = l // 2
        router = None
        if l % 2 == 1:
            router = _router_operands(router_w[j], router_b[j])
        outs = _token_mixer(x, B, L, l, norm_mix_g, w_in, kk, hy_conv_w, hy_conv_b, sgu_ln_g, sgu_ln_b,
                            sgu_ws, sgu_bs, rg_conv_w, rg_conv_b, rg_wa, rg_ba, rg_wx, rg_bx, rg_lambda,
                            ml_i_bias, ml_f_bias, ml_norm_g, w_branch, w_out, norm_ffn_g[l], router)
        if router is None:
            x, hn = outs
            x = _ffn(hn, x, ffn_w1[j].astype(BF16), ffn_w3[j].astype(BF16), ffn_w2[j].astype(BF16))
        else:
            x, hn, comb, combt = outs
            x = _moe(hn, x, comb, combt, moe_w1[j].astype(BF16), moe_w3[j].astype(BF16),
                     moe_w2[j].astype(BF16))
    g = norm_final_g[None, :]
    return (_final_norm(x, g, 0, bp * L).reshape(bp, L, D_MODEL),
            _final_norm(x, g, bp * L, bs * L).reshape(bs, L, D_MODEL))
```

```python
import functools
import math

import jax
import jax.numpy as jnp
import numpy as np
from jax import lax
from jax.experimental import pallas as pl
from jax.experimental.pallas import tpu as pltpu

F32 = jnp.float32
BF16 = jnp.bfloat16
EPS = 1e-6
HIGHEST = lax.Precision.HIGHEST

LANES = 128
SUBLANES = 8
VMEM_BYTES_V7X = 64 * 1024 * 1024

D_MODEL = 1024
W_BR = 512
N_BRANCH = 4
HY_BANDS = 16
HY_EMB = 1 + 2 * HY_BANDS
HY_FFN = 64
HY_TARGET = 1e-2
HY_MIN_DECAY = -math.log(HY_TARGET) / 1.5
HY_MAX_DECAY = -math.log(HY_TARGET) / 0.3
SGU_CHUNK = 128
SGU_GROUPS = 4
RG_HEADS = 8
RG_HD = W_BR // RG_HEADS
RG_C = 8.0
ML_HEADS = 4
ML_HD = W_BR // ML_HEADS
ML_CHUNK = 128
N_EXPERTS = 8

N_PROJ = N_BRANCH * D_MODEL + 3 * W_BR + 2 * W_BR + 2 * W_BR + W_BR
COL_GATE = 0
COL_HY = N_BRANCH * D_MODEL
COL_SGU = COL_HY + 3 * W_BR
COL_RG = COL_SGU + 2 * W_BR
COL_ML = COL_RG + 2 * W_BR


def _params(semantics, vmem_mb):
    return pltpu.CompilerParams(dimension_semantics=semantics,
                                vmem_limit_bytes=vmem_mb * 1024 * 1024)


def _sigmoid(x):
    return 0.5 * jnp.tanh(0.5 * x) + 0.5


def _rms(x, g):
    return x * lax.rsqrt(jnp.mean(x * x, axis=-1, keepdims=True) + EPS) * g


_NT = (((1,), (1,)), ((), ()))


def _inproj_kernel(x_ref, g_ref, w_ref, wgt_ref, wft_ref, proj_ref, mlgt_ref, ft_ref, h_ref):
    @pl.when(pl.program_id(1) == 0)
    def _():
        h = _rms(x_ref[...], g_ref[...]).astype(BF16)
        h_ref[...] = h
        mlgt_ref[...] = lax.dot_general(wgt_ref[...], h, _NT, preferred_element_type=F32)
        ft_ref[...] = lax.dot_general(wft_ref[...], h, _NT, preferred_element_type=F32).astype(BF16)

    proj_ref[...] = jnp.dot(h_ref[...], w_ref[...], preferred_element_type=F32).astype(BF16)


def _inproj(x, g, w, wgt, wft, tm=1024, n_split=4):
    T = x.shape[0]
    tn = N_PROJ // n_split
    nf = wft.shape[0]
    return pl.pallas_call(
        _inproj_kernel,
        out_shape=(jax.ShapeDtypeStruct((T, N_PROJ), BF16),
                   jax.ShapeDtypeStruct((LANES, T), F32),
                   jax.ShapeDtypeStruct((nf, T), BF16)),
        grid=(T // tm, n_split),
        in_specs=[pl.BlockSpec((tm, D_MODEL), lambda i, j: (i, 0)),
                  pl.BlockSpec((1, D_MODEL), lambda i, j: (0, 0)),
                  pl.BlockSpec((D_MODEL, tn), lambda i, j: (0, j)),
                  pl.BlockSpec((LANES, D_MODEL), lambda i, j: (0, 0)),
                  pl.BlockSpec((nf, D_MODEL), lambda i, j: (0, 0))],
        out_specs=(pl.BlockSpec((tm, tn), lambda i, j: (i, j)),
                   pl.BlockSpec((LANES, tm), lambda i, j: (0, i)),
                   pl.BlockSpec((nf, tm), lambda i, j: (0, i))),
        scratch_shapes=[pltpu.VMEM((tm, D_MODEL), BF16)],
        compiler_params=_params(("parallel", "arbitrary"), 56),
        name="inproj",
    )(x, g, w, wgt, wft)


def _hy_filter_kernel(z_ref, w1_ref, b1_ref, w2_ref, b2_ref, w3_ref, delta_ref, bias_ref, kk_ref):
    half = pl.program_id(2)
    z = z_ref[...]
    h = jnp.sin(jnp.dot(w1_ref[...], z, precision=HIGHEST, preferred_element_type=F32) + b1_ref[...])
    h = jnp.sin(jnp.dot(w2_ref[...], h, precision=HIGHEST, preferred_element_type=F32) + b2_ref[...])
    f = jnp.dot(w3_ref[...], h, precision=HIGHEST, preferred_element_type=F32)
    t_norm = z[0:1, :]
    f = f * jnp.exp(-t_norm * delta_ref[...])
    lane = lax.broadcasted_iota(jnp.int32, f.shape, 1)
    first = lane == 0
    f = jnp.where(first, jnp.where(half == 0, 0.0, f + bias_ref[...]), f)
    kk_ref[...] = f


def _hy_filters(hy_ffn_w1, hy_ffn_b1, hy_ffn_w2, hy_ffn_b2, hy_ffn_w3, hy_bias, L):
    depth = hy_ffn_w1.shape[0]
    lag = np.stack([L - np.arange(L), np.arange(L)]).astype(np.float64)
    bands = np.arange(1, HY_BANDS + 1, dtype=np.float64)
    ang = (2.0 * math.pi / L) * lag[:, None, :] * bands[None, :, None]
    z = np.concatenate([lag[:, None, :] / L, np.cos(ang), np.sin(ang)], axis=1)
    z = np.pad(z, ((0, 0), (0, LANES - HY_EMB), (0, 0))).astype(np.float32)
    z = jnp.asarray(z)
    w1t = jnp.pad(jnp.swapaxes(hy_ffn_w1, 1, 2), ((0, 0), (0, 0), (0, LANES - HY_EMB)))
    w2t = jnp.swapaxes(hy_ffn_w2, 1, 2)
    w3t = jnp.swapaxes(hy_ffn_w3, 1, 2).reshape(depth, 4, W_BR, HY_FFN)
    b1 = hy_ffn_b1[:, :, None]
    b2 = hy_ffn_b2[:, :, None]
    delta = jnp.linspace(HY_MIN_DECAY, HY_MAX_DECAY, W_BR, dtype=F32)[:, None]
    bias = hy_bias[:, :, :, None]
    return pl.pallas_call(
        _hy_filter_kernel,
        out_shape=jax.ShapeDtypeStruct((depth, 2, W_BR, 2 * L), F32),
        grid=(depth, 2, 2),
        in_specs=[pl.BlockSpec((None, LANES, L), lambda l, o, s: (s, 0, 0)),
                  pl.BlockSpec((None, HY_FFN, LANES), lambda l, o, s: (l, 0, 0)),
                  pl.BlockSpec((None, HY_FFN, 1), lambda l, o, s: (l, 0, 0)),
                  pl.BlockSpec((None, HY_FFN, HY_FFN), lambda l, o, s: (l, 0, 0)),
                  pl.BlockSpec((None, HY_FFN, 1), lambda l, o, s: (l, 0, 0)),
                  pl.BlockSpec((None, None, W_BR, HY_FFN), lambda l, o, s: (l, 2 * o + 1 - s, 0, 0)),
                  pl.BlockSpec((W_BR, 1), lambda l, o, s: (0, 0)),
                  pl.BlockSpec((None, None, W_BR, 1), lambda l, o, s: (l, o, 0, 0))],
        out_specs=pl.BlockSpec((None, None, W_BR, L), lambda l, o, s: (l, o, 0, s)),
        compiler_params=_params(("parallel", "parallel", "parallel"), 48),
        name="hy_filters",
    )(z, w1t, b1, w2t, b2, w3t, delta, bias)


HY_PRE_NB = 8


def _hy_pre_kernel(p_ref, w_ref, b_ref, o_ref):
    L = p_ref.shape[1]
    w = w_ref[...]
    row = lax.broadcasted_iota(jnp.int32, (L, LANES), 0)

    def body(bi, carry):
        x = p_ref[bi].astype(F32)
        xm = jnp.where(row == 0, 0.0, pltpu.roll(x, 1, 0))
        xp = jnp.where(row == L - 1, 0.0, pltpu.roll(x, L - 1, 0))
        u = b_ref[...] + w[0:1] * xm + w[1:2] * x + w[2:3] * xp
        o_ref[:, pl.ds(pl.multiple_of(bi * L, LANES), L)] = u.T.astype(BF16)
        return carry

    lax.fori_loop(0, HY_PRE_NB, body, 0, unroll=2)


def _hy_pre(proj3, conv_w, conv_b):
    B, L, _ = proj3.shape
    nc = 3 * W_BR // LANES
    return pl.pallas_call(
        _hy_pre_kernel,
        out_shape=jax.ShapeDtypeStruct((3 * W_BR, B * L), BF16),
        grid=(B // HY_PRE_NB, nc),
        in_specs=[pl.BlockSpec((HY_PRE_NB, L, LANES), lambda b, c: (b, 0, COL_HY // LANES + c)),
                  pl.BlockSpec((3, LANES), lambda b, c: (0, c)),
                  pl.BlockSpec((1, LANES), lambda b, c: (0, c))],
        out_specs=pl.BlockSpec((LANES, HY_PRE_NB * L), lambda b, c: (c, b)),
        compiler_params=_params(("parallel", "parallel"), 48),
        name="hy_pre",
    )(proj3, conv_w, conv_b)


HY_CB = 16
HY_KT = 2 * LANES
BF16_ROWS = 16


def _hy_build_tables(kk_ref, ci, g0_ref, g1_ref, L):
    nchunk = 2 * L // LANES
    upper = (lax.broadcasted_iota(jnp.int32, (LANES, LANES), 1)
             >= lax.broadcasted_iota(jnp.int32, (LANES, LANES), 0))

    def rolled(q):
        lo = 2 * LANES * (q // 2)
        half = slice(LANES * (q % 2), LANES * (q % 2 + 1))
        s0 = kk_ref[0, pl.ds(ci, 1), lo:lo + 2 * LANES][:, half]
        s1 = kk_ref[1, pl.ds(ci, 1), lo:lo + 2 * LANES][:, half]
        b0 = lax.bitcast_convert_type(s0.astype(BF16).astype(F32), jnp.uint32)
        b1 = lax.bitcast_convert_type(s1.astype(BF16).astype(F32), jnp.uint32)
        return pltpu.roll(jnp.broadcast_to(b0 | (b1 >> 16), (LANES, LANES)), 0, 1, stride=1, stride_axis=0)

    prev = rolled(nchunk - 1)
    for m in range(nchunk - 1):
        cur = rolled(nchunk - 2 - m)
        r = jnp.where(upper, prev, cur)
        rows = slice(LANES * m, LANES * (m + 1))
        g0_ref[rows, :] = lax.bitcast_convert_type(r & jnp.uint32(0xFFFF0000), F32).astype(BF16)
        g1_ref[rows, :] = lax.bitcast_convert_type(r << 16, F32).astype(BF16)
        prev = cur


def _hy_toeplitz(u, g_ref, u_ref, acc_ref, L):
    B = u.shape[0]
    nsb = L // HY_KT
    bp = u_ref.shape[0] // nsb
    pad = jnp.zeros((bp - B, HY_KT), F32)
    for sb in range(nsb):
        blk = u[:, HY_KT * sb:HY_KT * (sb + 1)]
        if bp > B:
            blk = jnp.concatenate([blk, pad], axis=0)
        u_ref[bp * sb:bp * (sb + 1), :] = blk.astype(BF16)
    acc_ref[...] = jnp.zeros(acc_ref.shape, F32)
    for delta in range(-(nsb - 1), nsb):
        d0 = HY_KT * delta + L - LANES
        tile = jnp.concatenate([g_ref[d0:d0 + HY_KT, :], g_ref[d0 - LANES:d0 - LANES + HY_KT, :]], axis=1)
        lo, hi = max(0, delta), min(nsb - 1, nsb - 1 + delta)
        part = jnp.dot(u_ref[bp * lo:bp * (hi + 1), :], tile, preferred_element_type=F32)
        acc_ref[bp * (lo - delta):bp * (hi - delta + 1), :] += part
    return jnp.concatenate([acc_ref[bp * tb:bp * tb + B, :] for tb in range(nsb)], axis=1)


def _hy_conv_kernel(z_ref, x1_ref, x2_ref, kk_ref, o_ref, g_ref, u_ref, acc_ref):
    L = z_ref.shape[2]

    def compute(ci, slot):
        y0 = _hy_toeplitz(z_ref[ci].astype(F32), g_ref.at[slot, 0], u_ref, acc_ref, L)
        z1 = x1_ref[ci].astype(F32) * y0
        y1 = _hy_toeplitz(z1, g_ref.at[slot, 1], u_ref, acc_ref, L)
        o_ref[ci] = (x2_ref[ci].astype(F32) * y1).astype(BF16)

    def build(ci, slot):
        _hy_build_tables(kk_ref, ci, g_ref.at[slot, 0], g_ref.at[slot, 1], L)

    build(0, 0)

    def body(k, carry):
        build(2 * k + 1, 1)
        compute(2 * k, 0)
        build(jnp.minimum(2 * k + 2, HY_CB - 1), 0)
        compute(2 * k + 1, 1)
        return carry

    lax.fori_loop(0, HY_CB // 2, body, 0, unroll=4)


def _hy_conv(zc3, kk):
    _, B, L = zc3.shape
    nblk = W_BR // HY_CB
    bp = -(-B // BF16_ROWS) * BF16_ROWS
    nsb = L // HY_KT
    act = lambda off: pl.BlockSpec((HY_CB, B, L), lambda c: (c + off * nblk, 0, 0))
    return pl.pallas_call(
        _hy_conv_kernel,
        out_shape=jax.ShapeDtypeStruct((W_BR, B, L), BF16),
        grid=(nblk,),
        in_specs=[act(0), act(1), act(2),
                  pl.BlockSpec((2, HY_CB, 2 * L), lambda c: (0, c, 0))],
        out_specs=pl.BlockSpec((HY_CB, B, L), lambda c: (c, 0, 0)),
        scratch_shapes=[pltpu.VMEM((2, 2, 2 * L - LANES, LANES), BF16),
                        pltpu.VMEM((nsb * bp, HY_KT), BF16),
                        pltpu.VMEM((nsb * bp, HY_KT), F32)],
        compiler_params=_params(("parallel",), 48),
        name="hy_conv",
    )(zc3, zc3, zc3, kk)


HY_POST_TT = 8192


def _hy_post_kernel(y_ref, o_ref):
    o_ref[...] = y_ref[...].astype(F32).T.astype(BF16)


def _hy_post(yc):
    C, T = yc.shape
    return pl.pallas_call(
        _hy_post_kernel,
        out_shape=jax.ShapeDtypeStruct((T, C), BF16),
        grid=(T // HY_POST_TT, C // LANES),
        in_specs=[pl.BlockSpec((LANES, HY_POST_TT), lambda t, c: (c, t))],
        out_specs=pl.BlockSpec((HY_POST_TT, LANES), lambda t, c: (t, c)),
        compiler_params=_params(("parallel", "parallel"), 48),
        name="hy_post",
    )(yc)


def _sgu_kernel(u_ref, v_ref, g_ref, b_ref, ws_ref, bs_ref, o_ref):
    L = u_ref.shape[0]
    gw = W_BR // SGU_GROUPS

    def body(n, carry):
        rows = pl.ds(pl.multiple_of(n * SGU_CHUNK, SGU_CHUNK), SGU_CHUNK)
        v = v_ref[rows, :].astype(F32)
        mu = jnp.mean(v, axis=-1, keepdims=True)
        d = v - mu
        var = jnp.mean(d * d, axis=-1, keepdims=True)
        vn = (d * lax.rsqrt(var + EPS) * g_ref[...] + b_ref[...]).astype(BF16)
        mixed = jnp.concatenate(
            [jnp.dot(ws_ref[k], vn[:, gw * k:gw * (k + 1)], preferred_element_type=F32) + bs_ref[k]
             for k in range(SGU_GROUPS)], axis=1)
        o_ref[rows, :] = (u_ref[rows, :].astype(F32) * mixed).astype(BF16)
        return carry

    lax.fori_loop(0, L // SGU_CHUNK, body, 0, unroll=4)


def _sgu(proj3, ln_g, ln_b, ws, bs):
    B, L, _ = proj3.shape
    cb = COL_SGU // W_BR
    return pl.pallas_call(
        _sgu_kernel,
        out_shape=jax.ShapeDtypeStruct((B, L, W_BR), BF16),
        grid=(B,),
        in_specs=[pl.BlockSpec((None, L, W_BR), lambda b: (b, 0, cb)),
                  pl.BlockSpec((None, L, W_BR), lambda b: (b, 0, cb + 1)),
                  pl.BlockSpec((1, W_BR), lambda b: (0, 0)),
                  pl.BlockSpec((1, W_BR), lambda b: (0, 0)),
                  pl.BlockSpec((SGU_GROUPS, SGU_CHUNK, SGU_CHUNK), lambda b: (0, 0, 0)),
                  pl.BlockSpec((SGU_GROUPS, SGU_CHUNK, 1), lambda b: (0, 0, 0))],
        out_specs=pl.BlockSpec((None, L, W_BR), lambda b: (b, 0, 0)),
        compiler_params=_params(("parallel",), 48),
        name="sgu",
    )(proj3, proj3, ln_g, ln_b, ws, bs)


RG_SLABS = 2
RG_TL = 256
RG_PAD = 8
RG_SKEW = 8
RG_UNROLL = 8


def _rg_kernel(xb_ref, gb_ref, cw_ref, cb_ref, w_ref, bias_ref, lam_ref, o_ref,
               xpad_ref, nat_ref, xi_ref, af_ref, bf_ref, ab_ref, bb_ref):
    L = xb_ref.shape[0]
    seg = L // SUBLANES
    pitch = seg + RG_SKEW
    zeros = jnp.zeros((RG_PAD, LANES), F32)
    sp_all = []
    for s in range(RG_SLABS):
        cols = slice(s * LANES, (s + 1) * LANES)
        xpad_ref[s, 0:RG_PAD, :] = zeros
        xpad_ref[s, RG_PAD + L:RG_PAD + L + RG_PAD, :] = zeros
        xpad_ref[s, RG_PAD:RG_PAD + L, :] = xb_ref[:, cols].astype(F32)
        cw = cw_ref[:, cols]
        for j in range(SUBLANES):
            t0 = j * seg
            xc = cb_ref[:, cols]
            for k in range(4):
                xc = xc + cw[k:k + 1] * xpad_ref[s, RG_PAD + t0 + k - 2:RG_PAD + t0 + k - 2 + seg, :]
            nat_ref[s, pitch * j:pitch * j + seg, :] = xc
        lam = lam_ref[:, cols]
        sp_all.append(jnp.maximum(-lam, 0.0) + jnp.log(1.0 + jnp.exp(-jnp.abs(lam))))

    def interleave(r, carry):
        dst = pl.ds(pl.multiple_of(r * SUBLANES, SUBLANES), SUBLANES)
        for s in range(RG_SLABS):
            xi_ref[s, dst, :] = nat_ref[s, pl.ds(r, SUBLANES, stride=pitch), :]
        return carry

    lax.fori_loop(0, seg, interleave, 0, unroll=RG_UNROLL)

    for s in range(RG_SLABS):
        for ti in range(L // RG_TL):
            rows = slice(ti * RG_TL, (ti + 1) * RG_TL)
            xc = xi_ref[s, rows, :]
            g = jnp.dot(xc.astype(BF16), w_ref[s], preferred_element_type=F32) + bias_ref[s]
            g = 0.5 * jnp.tanh(0.5 * g) + 0.5
            for d, (a_ref, b_ref) in enumerate(((af_ref, bf_ref), (ab_ref, bb_ref))):
                r = g[:, 2 * d * LANES:(2 * d + 1) * LANES]
                i = g[:, (2 * d + 1) * LANES:(2 * d + 2) * LANES]
                a = jnp.exp((-RG_C) * r * sp_all[s][d:d + 1])
                om = 1.0 - a * a
                a_ref[s, rows, :] = a
                b_ref[s, rows, :] = (om * lax.rsqrt(jnp.maximum(om, 1e-30))) * (i * xc)

    def rows_f(r):
        return pl.ds(pl.multiple_of(r * SUBLANES, SUBLANES), SUBLANES)

    def rows_b(r):
        return pl.ds(pl.multiple_of((seg - 1 - r) * SUBLANES, SUBLANES), SUBLANES)

    def pass1(r, carry):
        out = []
        for s in range(RG_SLABS):
            hf, pf, hb, pb = carry[4 * s:4 * s + 4]
            a = af_ref[s, rows_f(r), :]
            hf = a * hf + bf_ref[s, rows_f(r), :]
            pf = a * pf
            a = ab_ref[s, rows_b(r), :]
            hb = a * hb + bb_ref[s, rows_b(r), :]
            pb = a * pb
            out += [hf, pf, hb, pb]
        return tuple(out)

    z = jnp.zeros((SUBLANES, LANES), F32)
    one = jnp.ones((SUBLANES, LANES), F32)
    ends = lax.fori_loop(0, seg, pass1, (z, one, z, one) * RG_SLABS, unroll=RG_UNROLL)

    init = []
    for s in range(RG_SLABS):
        hf, pf, hb, pb = ends[4 * s:4 * s + 4]
        c = jnp.zeros((1, LANES), F32)
        rows = []
        for j in range(SUBLANES):
            rows.append(c)
            c = pf[j:j + 1] * c + hf[j:j + 1]
        init.append(jnp.concatenate(rows, axis=0))
        c = jnp.zeros((1, LANES), F32)
        rows = []
        for j in range(SUBLANES - 1, -1, -1):
            rows.append(c)
            c = pb[j:j + 1] * c + hb[j:j + 1]
        init.append(jnp.concatenate(rows[::-1], axis=0))

    def pass2(r, carry):
        out = []
        for s in range(RG_SLABS):
            hf, hb = carry[2 * s:2 * s + 2]
            hf = af_ref[s, rows_f(r), :] * hf + bf_ref[s, rows_f(r), :]
            xi_ref[s, rows_f(r), :] = hf
            hb = ab_ref[s, rows_b(r), :] * hb + bb_ref[s, rows_b(r), :]
            xpad_ref[s, rows_b(r), :] = hb
            out += [hf, hb]
        return tuple(out)

    lax.fori_loop(0, seg, pass2, tuple(init), unroll=RG_UNROLL)

    def deinterleave(r, carry):
        src = rows_f(r)
        for s in range(RG_SLABS):
            nat_ref[s, pl.ds(r, SUBLANES, stride=pitch), :] = xi_ref[s, src, :] + xpad_ref[s, src, :]
        return carry

    lax.fori_loop(0, seg, deinterleave, 0, unroll=RG_UNROLL)

    for s in range(RG_SLABS):
        cols = slice(s * LANES, (s + 1) * LANES)
        for j in range(SUBLANES):
            gb = gb_ref[j * seg:(j + 1) * seg, cols].astype(F32)
            gelu = 0.5 * gb * (1.0 + jnp.tanh(math.sqrt(2.0 / math.pi) * (gb + 0.044715 * (gb * gb * gb))))
            o_ref[j * seg:(j + 1) * seg, cols] = (nat_ref[s, pitch * j:pitch * j + seg, :] * gelu).astype(BF16)


def _rglru(proj3, conv_w, conv_b, w_bd, bias_bd, lam):
    B, L, _ = proj3.shape
    gw = RG_SLABS * LANES
    ng = W_BR // gw
    cx = COL_RG // gw
    seg = L // SUBLANES
    slab = lambda rows: pltpu.VMEM((RG_SLABS, rows, LANES), F32)
    return pl.pallas_call(
        _rg_kernel,
        out_shape=jax.ShapeDtypeStruct((B, L, W_BR), BF16),
        grid=(B, ng),
        in_specs=[pl.BlockSpec((None, L, gw), lambda b, c: (b, 0, cx + c)),
                  pl.BlockSpec((None, L, gw), lambda b, c: (b, 0, cx + ng + c)),
                  pl.BlockSpec((4, gw), lambda b, c: (0, c)),
                  pl.BlockSpec((1, gw), lambda b, c: (0, c)),
                  pl.BlockSpec((RG_SLABS, LANES, 4 * LANES), lambda b, c: (c, 0, 0)),
                  pl.BlockSpec((RG_SLABS, 1, 4 * LANES), lambda b, c: (c, 0, 0)),
                  pl.BlockSpec((2, gw), lambda b, c: (0, c))],
        out_specs=pl.BlockSpec((None, L, gw), lambda b, c: (b, 0, c)),
        scratch_shapes=[slab(L + 2 * RG_PAD), slab(SUBLANES * (seg + RG_SKEW))] + [slab(L)] * 5,
        compiler_params=_params(("parallel", "parallel"), 48),
        name="rglru",
    )(proj3, proj3, conv_w, conv_b, w_bd, bias_bd, lam)


def _log_sigmoid(x):
    return jnp.minimum(x, 0.0) - jnp.log(1.0 + jnp.exp(-jnp.abs(x)))


ML_AUG = 16


def _ml_kernel(qt_ref, vt_ref, ot_ref, k_ref, gtt_ref, gbias_ref, ng_ref, yt_ref,
               hf_ref, hb_ref, li_ref, b_ref, cc_ref, *state_refs):
    L = k_ref.shape[0]
    nc = L // ML_CHUNK
    ng2 = 2 * ML_HEADS
    row = lax.broadcasted_iota(jnp.int32, (ML_CHUNK, ML_CHUNK), 0)
    col = lax.broadcasted_iota(jnp.int32, (ML_CHUNK, ML_CHUNK), 1)
    scale = ML_HD ** -0.5
    masks = (col <= row, col >= row)
    for ref in state_refs:
        ref[...] = jnp.zeros(ref.shape, F32)

    li = gtt_ref[0:ng2, :] + gbias_ref[0:ng2, :]
    lf = _log_sigmoid(gtt_ref[ng2:2 * ng2, :] + gbias_ref[ng2:2 * ng2, :])
    pos = lax.broadcasted_iota(jnp.int32, (ng2, L), 1) & (ML_CHUNK - 1)
    pre_sum = lf
    suf_sum = lf
    k = 1
    while k < ML_CHUNK:
        pre_sum = pre_sum + jnp.where(pos >= k, pltpu.roll(pre_sum, k, 1), 0.0)
        suf_sum = suf_sum + jnp.where(pos < ML_CHUNK - k, pltpu.roll(suf_sum, L - k, 1), 0.0)
        k *= 2
    causal_row = lax.broadcasted_iota(jnp.int32, (ng2, L), 0) < ML_HEADS
    b_all = jnp.where(causal_row, pre_sum, suf_sum)
    li_ref[...] = li
    b_ref[...] = b_all
    diff = jnp.concatenate([li - b_all, jnp.zeros((LANES - ng2, L), F32)], axis=0)
    for c in range(nc):
        cc_ref[c * ML_CHUNK:(c + 1) * ML_CHUNK, :] = diff[:, c * ML_CHUNK:(c + 1) * ML_CHUNK].T

    def body(step, carry):
        for d in range(2):
            c = step if d == 0 else nc - 1 - step
            tsl = pl.ds(pl.multiple_of(c * ML_CHUNK, ML_CHUNK), ML_CHUNK)
            mask_st = masks[1 - d]
            last = ML_CHUNK - 1 if d == 0 else 0
            for hd in range(ML_HEADS):
                r = d * ML_HEADS + hd
                c_ref, m_ref = state_refs[2 * r], state_refs[2 * r + 1]
                b_row = b_ref[r:r + 1, tsl]
                li_row = li_ref[r:r + 1, tsl]
                c_col = cc_ref[tsl, r:r + 1]
                g_tot = b_row[:, last:last + 1]
                m_prev = m_ref[...]
                dlog = jnp.where(mask_st, b_row + c_col, -jnp.inf)
                inter = b_row + m_prev
                m_t = jnp.maximum(inter, jnp.max(dlog, axis=0, keepdims=True))
                w_intra = jnp.exp(dlog - m_t)
                w_inter = jnp.exp(inter - m_t)
                hs = slice(hd * ML_HD, (hd + 1) * ML_HD)
                qt = qt_ref[hs, tsl]
                kh = k_ref[tsl, hs]
                vt = vt_ref[hs, tsl]
                s = jnp.dot(kh, qt, preferred_element_type=F32) * (scale * w_intra)
                ca = c_ref[...]
                qc = jnp.dot(ca.astype(BF16), qt, preferred_element_type=F32) * scale
                num = w_inter * qc[0:ML_HD] + jnp.dot(vt, s.astype(BF16), preferred_element_type=F32)
                den = w_inter * qc[ML_HD:ML_HD + 1] + jnp.sum(s, axis=0, keepdims=True)
                hout = num * (1.0 / jnp.maximum(jnp.abs(den), jnp.exp(-m_t)))
                if d == 0:
                    hf_ref[hs, tsl] = hout
                else:
                    hb_ref[hs, tsl] = hout
                wlog = g_tot - b_row + li_row
                m_new = jnp.maximum(g_tot + m_prev, jnp.max(wlog, axis=1, keepdims=True))
                decay = jnp.exp(g_tot + m_prev - m_new)
                ws = jnp.exp(wlog - m_new)
                vw = jnp.concatenate([vt.astype(F32) * ws, jnp.broadcast_to(ws, (ML_AUG, ML_CHUNK))],
                                     axis=0).astype(BF16)
                c_ref[...] = decay * ca + jnp.dot(vw, kh, preferred_element_type=F32)
                m_ref[...] = m_new
        return carry

    lax.fori_loop(0, nc, body, 0, unroll=4)

    def finish(c, carry):
        tsl = pl.ds(pl.multiple_of(c * ML_CHUNK, ML_CHUNK), ML_CHUNK)
        for hd in range(ML_HEADS):
            hs = slice(hd * ML_HD, (hd + 1) * ML_HD)
            h = hf_ref[hs, tsl] + hb_ref[hs, tsl]
            hn = h * lax.rsqrt(jnp.mean(h * h, axis=0, keepdims=True) + EPS) * ng_ref[hs, :]
            yt_ref[hs, tsl] = (_sigmoid(ot_ref[hs, tsl].astype(F32)) * hn).astype(BF16)
        return carry

    lax.fori_loop(0, nc, finish, 0)


def _mlstm(qvot, proj3, mlgt, gbias, norm_g):
    B, L, _ = proj3.shape
    cb = COL_ML // W_BR
    ng4 = 4 * ML_HEADS
    fm = lambda j: pl.BlockSpec((W_BR, L), lambda b: (j, b))
    return pl.pallas_call(
        _ml_kernel,
        out_shape=jax.ShapeDtypeStruct((W_BR, B * L), BF16),
        grid=(B,),
        in_specs=[fm(0), fm(1), fm(2),
                  pl.BlockSpec((None, L, W_BR), lambda b: (b, 0, cb)),
                  pl.BlockSpec((LANES, L), lambda b: (0, b)),
                  pl.BlockSpec((ng4, 1), lambda b: (0, 0)),
                  pl.BlockSpec((W_BR, LANES), lambda b: (0, 0))],
        out_specs=pl.BlockSpec((W_BR, L), lambda b: (0, b)),
        scratch_shapes=[pltpu.VMEM((W_BR, L), F32),
                        pltpu.VMEM((W_BR, L), F32),
                        pltpu.VMEM((2 * ML_HEADS, L), F32),
                        pltpu.VMEM((2 * ML_HEADS, L), F32),
                        pltpu.VMEM((L, LANES), F32)]
        + [pltpu.VMEM((ML_HD + ML_AUG, ML_HD), F32), pltpu.VMEM((1, 1), F32)] * (2 * ML_HEADS),
        compiler_params=_params(("parallel",), 48),
        name="mlstm",
    )(qvot, qvot, qvot, proj3, mlgt, gbias, jnp.broadcast_to(norm_g.reshape(W_BR, 1), (W_BR, LANES)))


def _merge_kernel(route, gate_ref, ya_ref, yb_ref, yc_ref, yd_ref, wb_ref, wo_ref, x_ref, g_ref, *rest):
    if route:
        rw_ref, rb_ref, xo_ref, hn_ref, comb_ref, combt_ref = rest
    else:
        xo_ref, hn_ref = rest
    merged = None
    for k, y_ref in enumerate((ya_ref, yb_ref, yc_ref, yd_ref)):
        t = jnp.dot(y_ref[...], wb_ref[k], preferred_element_type=F32)
        gk = _sigmoid(gate_ref[:, k * D_MODEL:(k + 1) * D_MODEL].astype(F32))
        merged = gk * t if merged is None else merged + gk * t
    xn = x_ref[...] + jnp.dot(merged.astype(BF16), wo_ref[...], preferred_element_type=F32)
    xo_ref[...] = xn
    h = _rms(xn, g_ref[...])
    h_hi = h.astype(BF16)
    hn_ref[...] = h_hi
    if route:
        h_lo = (h - h_hi.astype(F32)).astype(BF16)
        r_hi = jnp.dot(h_hi, rw_ref[...], preferred_element_type=F32)
        r_lo = jnp.dot(h_lo, rw_ref[:, 0:LANES], preferred_element_type=F32)
        logits = r_hi[:, 0:LANES] + r_hi[:, LANES:2 * LANES] + r_lo + rb_ref[...]
        lane = lax.broadcasted_iota(jnp.int32, logits.shape, 1)
        logits = jnp.where(lane < N_EXPERTS, logits, -jnp.inf)
        v1 = jnp.max(logits, axis=1, keepdims=True)
        i1 = jnp.min(jnp.where(logits == v1, lane, LANES), axis=1, keepdims=True)
        rest_l = jnp.where(lane == i1, -jnp.inf, logits)
        v2 = jnp.max(rest_l, axis=1, keepdims=True)
        i2 = jnp.min(jnp.where(rest_l == v2, lane, LANES), axis=1, keepdims=True)
        e2 = jnp.exp(v2 - v1)
        p1 = 1.0 / (1.0 + e2)
        comb = jnp.where(lane == i1, p1, jnp.where(lane == i2, e2 * p1, 0.0))
        comb_ref[...] = comb
        combt_ref[...] = comb.T


def _merge(proj, ya, yb, yc, yd, wb, wo, x, g, router=None, tm=512):
    T = x.shape[0]
    row = lambda w: pl.BlockSpec((tm, w), lambda i: (i, 0))
    in_specs = [row(N_BRANCH * D_MODEL), row(W_BR), row(W_BR), row(W_BR), row(W_BR),
                pl.BlockSpec((N_BRANCH, W_BR, D_MODEL), lambda i: (0, 0, 0)),
                pl.BlockSpec((D_MODEL, D_MODEL), lambda i: (0, 0)),
                row(D_MODEL),
                pl.BlockSpec((1, D_MODEL), lambda i: (0, 0))]
    out_shape = [jax.ShapeDtypeStruct((T, D_MODEL), F32), jax.ShapeDtypeStruct((T, D_MODEL), BF16)]
    out_specs = [row(D_MODEL), row(D_MODEL)]
    args = [proj, ya, yb, yc, yd, wb, wo, x, g]
    if router is not None:
        in_specs += [pl.BlockSpec((D_MODEL, 2 * LANES), lambda i: (0, 0)),
                     pl.BlockSpec((1, LANES), lambda i: (0, 0))]
        out_shape += [jax.ShapeDtypeStruct((T, LANES), F32), jax.ShapeDtypeStruct((LANES, T), F32)]
        out_specs += [row(LANES), pl.BlockSpec((LANES, tm), lambda i: (0, i))]
        args += list(router)
    return pl.pallas_call(
        functools.partial(_merge_kernel, router is not None),
        out_shape=tuple(out_shape),
        grid=(T // tm,),
        in_specs=in_specs,
        out_specs=tuple(out_specs),
        compiler_params=_params(("parallel",), 56),
        name="merge_route" if router is not None else "merge",
    )(*args)


def _swiglu_acc(h, w1_ref, w3_ref, w2_ref):
    a = jnp.dot(h, w1_ref[...], preferred_element_type=F32)
    b = jnp.dot(h, w3_ref[...], preferred_element_type=F32)
    act = (a * _sigmoid(a) * b).astype(BF16)
    return jnp.dot(act, w2_ref[...], preferred_element_type=F32)


def _ffn_kernel(h_ref, x_ref, w1_ref, w3_ref, w2_ref, o_ref, acc_ref):
    f = pl.program_id(1)

    @pl.when(f == 0)
    def _():
        acc_ref[...] = x_ref[...]

    acc_ref[...] += _swiglu_acc(h_ref[...], w1_ref, w3_ref, w2_ref)

    @pl.when(f == pl.num_programs(1) - 1)
    def _():
        o_ref[...] = acc_ref[...]


def _ffn(hn, x, w1, w3, w2, tm=512, n_split=2):
    T = x.shape[0]
    dff = w1.shape[1]
    tf = dff // n_split
    return pl.pallas_call(
        _ffn_kernel,
        out_shape=jax.ShapeDtypeStruct((T, D_MODEL), F32),
        grid=(T // tm, n_split),
        in_specs=[pl.BlockSpec((tm, D_MODEL), lambda i, f: (i, 0)),
                  pl.BlockSpec((tm, D_MODEL), lambda i, f: (i, 0)),
                  pl.BlockSpec((D_MODEL, tf), lambda i, f: (0, f)),
                  pl.BlockSpec((D_MODEL, tf), lambda i, f: (0, f)),
                  pl.BlockSpec((tf, D_MODEL), lambda i, f: (f, 0))],
        out_specs=pl.BlockSpec((tm, D_MODEL), lambda i, f: (i, 0)),
        scratch_shapes=[pltpu.VMEM((tm, D_MODEL), F32)],
        compiler_params=_params(("parallel", "arbitrary"), 56),
        name="ffn",
    )(hn, x, w1, w3, w2)


MOE_TM = 1024
MOE_CH = 256


def _moe_kernel(cnt_ref, h_ref, x_ref, comb_ref, combt_ref, w1_ref, w3_ref, w2_ref, o_ref,
                rank_ref, rankt_ref):
    i = pl.program_id(0)
    e = pl.program_id(1)
    tm = h_ref.shape[0]

    @pl.when(e == 0)
    def _():
        o_ref[...] = x_ref[...]
        r = lax.broadcasted_iota(jnp.int32, (tm, tm), 0)
        c = lax.broadcasted_iota(jnp.int32, (tm, tm), 1)
        before = jnp.where(c < r, 1.0, 0.0).astype(BF16)
        sel = jnp.where(comb_ref[...] > 0.0, 1.0, 0.0).astype(BF16)
        rank_ref[...] = jnp.dot(before, sel, preferred_element_type=F32)
        selt = jnp.where(combt_ref[...] > 0.0, 1.0, 0.0).astype(BF16)
        rankt_ref[...] = lax.dot_general(selt, before, (((1,), (1,)), ((), ())),
                                         preferred_element_type=F32)

    lane = lax.broadcasted_iota(jnp.int32, (tm, LANES), 1)
    comb = comb_ref[...]
    c_col = jnp.sum(jnp.where(lane == e, comb, 0.0), axis=1, keepdims=True)
    rank_col = jnp.sum(jnp.where(lane == e, rank_ref[...], 0.0), axis=1, keepdims=True)
    rank_col = jnp.where(c_col > 0.0, rank_col, -1.0)
    c_row = combt_ref[pl.ds(e, 1), :]
    rank_row = jnp.where(c_row > 0.0, rankt_ref[pl.ds(e, 1), :], -1.0)
    cnt = cnt_ref[i * N_EXPERTS + e]

    def run_chunk(base, size):
        pos_r = (lax.broadcasted_iota(jnp.int32, (size, tm), 0) + base).astype(F32)
        gather = jnp.where(rank_row == pos_r, 1.0, 0.0).astype(BF16)
        xs = jnp.dot(gather, h_ref[...], preferred_element_type=F32).astype(BF16)
        y = _swiglu_acc(xs, w1_ref, w3_ref, w2_ref).astype(BF16)
        pos_c = (lax.broadcasted_iota(jnp.int32, (tm, size), 1) + base).astype(F32)
        scatter = jnp.where(rank_col == pos_c, c_col, 0.0).astype(BF16)
        o_ref[...] += jnp.dot(scatter, y, preferred_element_type=F32)

    n_full = cnt // MOE_CH
    rem = cnt - n_full * MOE_CH

    def body(j, carry):
        run_chunk(j * MOE_CH, MOE_CH)
        return carry

    lax.fori_loop(0, n_full, body, 0)

    @pl.when(rem > MOE_CH // 2)
    def _():
        run_chunk(n_full * MOE_CH, MOE_CH)

    @pl.when((rem > 0) & (rem <= MOE_CH // 2))
    def _():
        run_chunk(n_full * MOE_CH, MOE_CH // 2)


def _moe(hn, x, comb, combt, w1, w3, w2, tm=MOE_TM):
    T = x.shape[0]
    dfe = w1.shape[2]
    nt = T // tm
    cnt = jnp.sum((comb[:, :N_EXPERTS] > 0.0).reshape(nt, tm, N_EXPERTS), axis=1, dtype=jnp.int32).reshape(-1)
    grid_spec = pltpu.PrefetchScalarGridSpec(
        num_scalar_prefetch=1,
        grid=(nt, N_EXPERTS),
        in_specs=[pl.BlockSpec((tm, D_MODEL), lambda i, e, c: (i, 0)),
                  pl.BlockSpec((tm, D_MODEL), lambda i, e, c: (i, 0)),
                  pl.BlockSpec((tm, LANES), lambda i, e, c: (i, 0)),
                  pl.BlockSpec((LANES, tm), lambda i, e, c: (0, i)),
                  pl.BlockSpec((None, D_MODEL, dfe), lambda i, e, c: (e, 0, 0)),
                  pl.BlockSpec((None, D_MODEL, dfe), lambda i, e, c: (e, 0, 0)),
                  pl.BlockSpec((None, dfe, D_MODEL), lambda i, e, c: (e, 0, 0))],
        out_specs=pl.BlockSpec((tm, D_MODEL), lambda i, e, c: (i, 0)),
        scratch_shapes=[pltpu.VMEM((tm, LANES), F32), pltpu.VMEM((LANES, tm), F32)])
    return pl.pallas_call(
        _moe_kernel,
        out_shape=jax.ShapeDtypeStruct((T, D_MODEL), F32),
        grid_spec=grid_spec,
        compiler_params=_params(("parallel", "arbitrary"), 56),
        name="moe",
    )(cnt, hn, x, comb, combt, w1, w3, w2)


def _final_norm_kernel(x_ref, g_ref, o_ref):
    o_ref[...] = _rms(x_ref[...], g_ref[...])


def _final_norm(x, g, row0, rows, tm=1024):
    first = row0 // tm
    return pl.pallas_call(
        _final_norm_kernel,
        out_shape=jax.ShapeDtypeStruct((rows, D_MODEL), F32),
        grid=(rows // tm,),
        in_specs=[pl.BlockSpec((tm, D_MODEL), lambda i: (i + first, 0)),
                  pl.BlockSpec((1, D_MODEL), lambda i: (0, 0))],
        out_specs=pl.BlockSpec((tm, D_MODEL), lambda i: (i, 0)),
        compiler_params=_params(("parallel",), 48),
        name="final_norm",
    )(x, g)


def _split_w_in(w_in_l):
    n_hy, n_sgu, n_rg = 3 * W_BR, 2 * W_BR, 2 * W_BR
    o1 = n_hy
    o2 = o1 + n_sgu
    o3 = o2 + n_rg
    o4 = o3 + 4 * W_BR
    o5 = o4 + 4 * ML_HEADS
    w = jnp.concatenate([w_in_l[:, o5:], w_in_l[:, :o3], w_in_l[:, o3 + W_BR:o3 + 2 * W_BR]],
                        axis=1).astype(BF16)
    wgt = jnp.pad(w_in_l[:, o4:o5], ((0, 0), (0, LANES - 4 * ML_HEADS))).T.astype(BF16)
    wft = jnp.concatenate([w_in_l[:, o3:o3 + W_BR], w_in_l[:, o3 + 2 * W_BR:o4]], axis=1).T.astype(BF16)
    return w, wgt, wft


def _router_operands(router_w, router_b):
    w = jnp.pad(router_w, ((0, 0), (0, LANES - N_EXPERTS)))
    w_hi = w.astype(BF16)
    w_lo = (w - w_hi.astype(F32)).astype(BF16)
    return (jnp.concatenate([w_hi, w_lo], axis=1),
            jnp.pad(router_b, (0, LANES - N_EXPERTS))[None, :])


def _rg_block_diag(wa, ba, wx, bx):
    hpg = LANES // RG_HD
    ng = RG_HEADS // hpg
    eye = jnp.eye(hpg, dtype=F32)

    def bd(w):
        w = w.reshape(ng, hpg, RG_HD, RG_HD)
        return jnp.einsum('gaij,ab->gaibj', w, eye).reshape(ng, LANES, LANES)

    w = jnp.concatenate([bd(wa[0]), bd(wx[0]), bd(wa[1]), bd(wx[1])], axis=2).astype(BF16)
    fl = lambda b: b.reshape(ng, 1, LANES)
    bias = jnp.concatenate([fl(ba[0]), fl(bx[0]), fl(ba[1]), fl(bx[1])], axis=2)
    return w, bias


def _token_mixer(x, B, L, l, norm_g, w_in, kk, hy_conv_w, hy_conv_b, sgu_ln_g, sgu_ln_b, sgu_ws, sgu_bs,
                 rg_conv_w, rg_conv_b, rg_wa, rg_ba, rg_wx, rg_bx, rg_lambda, ml_i_bias, ml_f_bias,
                 ml_norm_g, w_branch, w_out, next_g, router):
    T = B * L
    w, wgt, wft = _split_w_in(w_in[l])
    proj, mlgt, qvot = _inproj(x, norm_g[l][None, :], w, wgt, wft)
    proj3 = proj.reshape(B, L, N_PROJ)

    zc = _hy_pre(proj3, hy_conv_w[l], hy_conv_b[l][None, :])
    ya_c = _hy_conv(zc.reshape(3 * W_BR, B, L), kk[l])
    ya = _hy_post(ya_c.reshape(W_BR, T))

    yb = _sgu(proj3, sgu_ln_g[l][None, :], sgu_ln_b[l][None, :], sgu_ws[l].astype(BF16),
              sgu_bs[l][:, :, None]).reshape(T, W_BR)

    w_bd, bias_bd = _rg_block_diag(rg_wa[l], rg_ba[l], rg_wx[l], rg_bx[l])
    yc = _rglru(proj3, rg_conv_w[l], rg_conv_b[l][None, :], w_bd, bias_bd, rg_lambda[l]).reshape(T, W_BR)

    gbias = jnp.concatenate([ml_i_bias[l].reshape(-1), ml_f_bias[l].reshape(-1)])[:, None]
    yd = _hy_post(_mlstm(qvot, proj3, mlgt, gbias, ml_norm_g[l]))

    return _merge(proj, ya, yb, yc, yd, w_branch[l].astype(BF16), w_out[l].astype(BF16), x,
                  next_g[None, :], router)


def kernel(x_prompt, x_sample, norm_mix_g, w_in, hy_conv_w, hy_conv_b, hy_ffn_w1, hy_ffn_b1, hy_ffn_w2, hy_ffn_b2, hy_ffn_w3, hy_bias, sgu_ln_g, sgu_ln_b, sgu_ws, sgu_bs, rg_conv_w, rg_conv_b, rg_wa, rg_ba, rg_wx, rg_bx, rg_lambda, ml_i_bias, ml_f_bias, ml_norm_g, w_branch, w_out, norm_ffn_g, ffn_w1, ffn_w3, ffn_w2, router_w, router_b, moe_w1, moe_w3, moe_w2, norm_final_g):
    bp, L, _ = x_prompt.shape
    bs = x_sample.shape[0]
    B = bp + bs
    depth = w_in.shape[0]
    x = jnp.concatenate([x_prompt, x_sample], axis=0).reshape(B * L, D_MODEL)
    kk = _hy_filters(hy_ffn_w1, hy_ffn_b1, hy_ffn_w2, hy_ffn_b2, hy_ffn_w3, hy_bias, L)
    for l in range(depth):
        j = l // 2
        router = None
        if l % 2 == 1:
            router = _router_operands(router_w[j], router_b[j])
        outs = _token_mixer(x, B, L, l, norm_mix_g, w_in, kk, hy_conv_w, hy_conv_b, sgu_ln_g, sgu_ln_b,
                            sgu_ws, sgu_bs, rg_conv_w, rg_conv_b, rg_wa, rg_ba, rg_wx, rg_bx, rg_lambda,
                            ml_i_bias, ml_f_bias, ml_norm_g, w_branch, w_out, norm_ffn_g[l], router)
        if router is None:
            x, hn = outs
            x = _ffn(hn, x, ffn_w1[j].astype(BF16), ffn_w3[j].astype(BF16), ffn_w2[j].astype(BF16))
        else:
            x, hn, comb, combt = outs
            x = _moe(hn, x, comb, combt, moe_w1[j].astype(BF16), moe_w3[j].astype(BF16),
                     moe_w2[j].astype(BF16))
    g = norm_final_g[None, :]
    return (_final_norm(x, g, 0, bp * L).reshape(bp, L, D_MODEL),
            _final_norm(x, g, bp * L, bs * L).reshape(bs, L, D_MODEL))
```

```python
import functools
import math

import jax
import jax.numpy as jnp
import numpy as np
from jax import lax
from jax.experimental import pallas as pl
from jax.experimental.pallas import tpu as pltpu

F32 = jnp.float32
BF16 = jnp.bfloat16
EPS = 1e-6
HIGHEST = lax.Precision.HIGHEST

LANES = 128
SUBLANES = 8
VMEM_BYTES_V7X = 64 * 1024 * 1024

D_MODEL = 1024
W_BR = 512
N_BRANCH = 4
HY_BANDS = 16
HY_EMB = 1 + 2 * HY_BANDS
HY_FFN = 64
HY_TARGET = 1e-2
HY_MIN_DECAY = -math.log(HY_TARGET) / 1.5
HY_MAX_DECAY = -math.log(HY_TARGET) / 0.3
SGU_CHUNK = 128
SGU_GROUPS = 4
RG_HEADS = 8
RG_HD = W_BR // RG_HEADS
RG_C = 8.0
ML_HEADS = 4
ML_HD = W_BR // ML_HEADS
ML_CHUNK = 128
N_EXPERTS = 8

N_PROJ = N_BRANCH * D_MODEL + 3 * W_BR + 2 * W_BR + 2 * W_BR + W_BR
COL_GATE = 0
COL_HY = N_BRANCH * D_MODEL
COL_SGU = COL_HY + 3 * W_BR
COL_RG = COL_SGU + 2 * W_BR
COL_ML = COL_RG + 2 * W_BR


def _params(semantics, vmem_mb):
    return pltpu.CompilerParams(dimension_semantics=semantics,
                                vmem_limit_bytes=vmem_mb * 1024 * 1024)


def _sigmoid(x):
    return 0.5 * jnp.tanh(0.5 * x) + 0.5


def _rms(x, g):
    return x * lax.rsqrt(jnp.mean(x * x, axis=-1, keepdims=True) + EPS) * g


_NT = (((1,), (1,)), ((), ()))


def _inproj_kernel(x_ref, g_ref, w_ref, wgt_ref, wft_ref, proj_ref, mlgt_ref, ft_ref, h_ref):
    @pl.when(pl.program_id(1) == 0)
    def _():
        h = _rms(x_ref[...], g_ref[...]).astype(BF16)
        h_ref[...] = h
        mlgt_ref[...] = lax.dot_general(wgt_ref[...], h, _NT, preferred_element_type=F32)
        ft_ref[...] = lax.dot_general(wft_ref[...], h, _NT, preferred_element_type=F32).astype(BF16)

    proj_ref[...] = jnp.dot(h_ref[...], w_ref[...], preferred_element_type=F32).astype(BF16)


def _inproj(x, g, w, wgt, wft, tm=1024, n_split=4):
    T = x.shape[0]
    tn = N_PROJ // n_split
    nf = wft.shape[0]
    return pl.pallas_call(
        _inproj_kernel,
        out_shape=(jax.ShapeDtypeStruct((T, N_PROJ), BF16),
                   jax.ShapeDtypeStruct((LANES, T), F32),
                   jax.ShapeDtypeStruct((nf, T), BF16)),
        grid=(T // tm, n_split),
        in_specs=[pl.BlockSpec((tm, D_MODEL), lambda i, j: (i, 0)),
                  pl.BlockSpec((1, D_MODEL), lambda i, j: (0, 0)),
                  pl.BlockSpec((D_MODEL, tn), lambda i, j: (0, j)),
                  pl.BlockSpec((LANES, D_MODEL), lambda i, j: (0, 0)),
                  pl.BlockSpec((nf, D_MODEL), lambda i, j: (0, 0))],
        out_specs=(pl.BlockSpec((tm, tn), lambda i, j: (i, j)),
                   pl.BlockSpec((LANES, tm), lambda i, j: (0, i)),
                   pl.BlockSpec((nf, tm), lambda i, j: (0, i))),
        scratch_shapes=[pltpu.VMEM((tm, D_MODEL), BF16)],
        compiler_params=_params(("parallel", "arbitrary"), 56),
        name="inproj",
    )(x, g, w, wgt, wft)


def _hy_filter_kernel(z_ref, w1_ref, b1_ref, w2_ref, b2_ref, w3_ref, delta_ref, bias_ref, kk_ref):
    half = pl.program_id(2)
    z = z_ref[...]
    h = jnp.sin(jnp.dot(w1_ref[...], z, precision=HIGHEST, preferred_element_type=F32) + b1_ref[...])
    h = jnp.sin(jnp.dot(w2_ref[...], h, precision=HIGHEST, preferred_element_type=F32) + b2_ref[...])
    f = jnp.dot(w3_ref[...], h, precision=HIGHEST, preferred_element_type=F32)
    t_norm = z[0:1, :]
    f = f * jnp.exp(-t_norm * delta_ref[...])
    lane = lax.broadcasted_iota(jnp.int32, f.shape, 1)
    first = lane == 0
    f = jnp.where(first, jnp.where(half == 0, 0.0, f + bias_ref[...]), f)
    kk_ref[...] = f


def _hy_filters(hy_ffn_w1, hy_ffn_b1, hy_ffn_w2, hy_ffn_b2, hy_ffn_w3, hy_bias, L):
    depth = hy_ffn_w1.shape[0]
    lag = np.stack([L - np.arange(L), np.arange(L)]).astype(np.float64)
    bands = np.arange(1, HY_BANDS + 1, dtype=np.float64)
    ang = (2.0 * math.pi / L) * lag[:, None, :] * bands[None, :, None]
    z = np.concatenate([lag[:, None, :] / L, np.cos(ang), np.sin(ang)], axis=1)
    z = np.pad(z, ((0, 0), (0, LANES - HY_EMB), (0, 0))).astype(np.float32)
    z = jnp.asarray(z)
    w1t = jnp.pad(jnp.swapaxes(hy_ffn_w1, 1, 2), ((0, 0), (0, 0), (0, LANES - HY_EMB)))
    w2t = jnp.swapaxes(hy_ffn_w2, 1, 2)
    w3t = jnp.swapaxes(hy_ffn_w3, 1, 2).reshape(depth, 4, W_BR, HY_FFN)
    b1 = hy_ffn_b1[:, :, None]
    b2 = hy_ffn_b2[:, :, None]
    delta = jnp.linspace(HY_MIN_DECAY, HY_MAX_DECAY, W_BR, dtype=F32)[:, None]
    bias = hy_bias[:, :, :, None]
    return pl.pallas_call(
        _hy_filter_kernel,
        out_shape=jax.ShapeDtypeStruct((depth, 2, W_BR, 2 * L), F32),
        grid=(depth, 2, 2),
        in_specs=[pl.BlockSpec((None, LANES, L), lambda l, o, s: (s, 0, 0)),
                  pl.BlockSpec((None, HY_FFN, LANES), lambda l, o, s: (l, 0, 0)),
                  pl.BlockSpec((None, HY_FFN, 1), lambda l, o, s: (l, 0, 0)),
                  pl.BlockSpec((None, HY_FFN, HY_FFN), lambda l, o, s: (l, 0, 0)),
                  pl.BlockSpec((None, HY_FFN, 1), lambda l, o, s: (l, 0, 0)),
                  pl.BlockSpec((None, None, W_BR, HY_FFN), lambda l, o, s: (l, 2 * o + 1 - s, 0, 0)),
                  pl.BlockSpec((W_BR, 1), lambda l, o, s: (0, 0)),
                  pl.BlockSpec((None, None, W_BR, 1), lambda l, o, s: (l, o, 0, 0))],
        out_specs=pl.BlockSpec((None, None, W_BR, L), lambda l, o, s: (l, o, 0, s)),
        compiler_params=_params(("parallel", "parallel", "parallel"), 48),
        name="hy_filters",
    )(z, w1t, b1, w2t, b2, w3t, delta, bias)


HY_PRE_NB = 8


def _hy_pre_kernel(p_ref, w_ref, b_ref, o_ref):
    L = p_ref.shape[1]
    w = w_ref[...]
    row = lax.broadcasted_iota(jnp.int32, (L, LANES), 0)

    def body(bi, carry):
        x = p_ref[bi].astype(F32)
        xm = jnp.where(row == 0, 0.0, pltpu.roll(x, 1, 0))
        xp = jnp.where(row == L - 1, 0.0, pltpu.roll(x, L - 1, 0))
        u = b_ref[...] + w[0:1] * xm + w[1:2] * x + w[2:3] * xp
        o_ref[:, pl.ds(pl.multiple_of(bi * L, LANES), L)] = u.T.astype(BF16)
        return carry

    lax.fori_loop(0, HY_PRE_NB, body, 0, unroll=2)


def _hy_pre(proj3, conv_w, conv_b):
    B, L, _ = proj3.shape
    nc = 3 * W_BR // LANES
    return pl.pallas_call(
        _hy_pre_kernel,
        out_shape=jax.ShapeDtypeStruct((3 * W_BR, B * L), BF16),
        grid=(B // HY_PRE_NB, nc),
        in_specs=[pl.BlockSpec((HY_PRE_NB, L, LANES), lambda b, c: (b, 0, COL_HY // LANES + c)),
                  pl.BlockSpec((3, LANES), lambda b, c: (0, c)),
                  pl.BlockSpec((1, LANES), lambda b, c: (0, c))],
        out_specs=pl.BlockSpec((LANES, HY_PRE_NB * L), lambda b, c: (c, b)),
        compiler_params=_params(("parallel", "parallel"), 48),
        name="hy_pre",
    )(proj3, conv_w, conv_b)


HY_CB = 16
HY_KT = 2 * LANES
BF16_ROWS = 16


def _hy_build_tables(kk_ref, ci, g0_ref, g1_ref, L):
    nchunk = 2 * L // LANES
    upper = (lax.broadcasted_iota(jnp.int32, (LANES, LANES), 1)
             >= lax.broadcasted_iota(jnp.int32, (LANES, LANES), 0))

    def rolled(q):
        lo = 2 * LANES * (q // 2)
        half = slice(LANES * (q % 2), LANES * (q % 2 + 1))
        s0 = kk_ref[0, pl.ds(ci, 1), lo:lo + 2 * LANES][:, half]
        s1 = kk_ref[1, pl.ds(ci, 1), lo:lo + 2 * LANES][:, half]
        b0 = lax.bitcast_convert_type(s0.astype(BF16).astype(F32), jnp.uint32)
        b1 = lax.bitcast_convert_type(s1.astype(BF16).astype(F32), jnp.uint32)
        return pltpu.roll(jnp.broadcast_to(b0 | (b1 >> 16), (LANES, LANES)), 0, 1, stride=1, stride_axis=0)

    prev = rolled(nchunk - 1)
    for m in range(nchunk - 1):
        cur = rolled(nchunk - 2 - m)
        r = jnp.where(upper, prev, cur)
        rows = slice(LANES * m, LANES * (m + 1))
        g0_ref[rows, :] = lax.bitcast_convert_type(r & jnp.uint32(0xFFFF0000), F32).astype(BF16)
        g1_ref[rows, :] = lax.bitcast_convert_type(r << 16, F32).astype(BF16)
        prev = cur


def _hy_toeplitz(u, g_ref, u_ref, acc_ref, L):
    B = u.shape[0]
    nsb = L // HY_KT
    bp = u_ref.shape[0] // nsb
    pad = jnp.zeros((bp - B, HY_KT), F32)
    for sb in range(nsb):
        blk = u[:, HY_KT * sb:HY_KT * (sb + 1)]
        if bp > B:
            blk = jnp.concatenate([blk, pad], axis=0)
        u_ref[bp * sb:bp * (sb + 1), :] = blk.astype(BF16)
    acc_ref[...] = jnp.zeros(acc_ref.shape, F32)
    for delta in range(-(nsb - 1), nsb):
        d0 = HY_KT * delta + L - LANES
        tile = jnp.concatenate([g_ref[d0:d0 + HY_KT, :], g_ref[d0 - LANES:d0 - LANES + HY_KT, :]], axis=1)
        lo, hi = max(0, delta), min(nsb - 1, nsb - 1 + delta)
        part = jnp.dot(u_ref[bp * lo:bp * (hi + 1), :], tile, preferred_element_type=F32)
        acc_ref[bp * (lo - delta):bp * (hi - delta + 1), :] += part
    return jnp.concatenate([acc_ref[bp * tb:bp * tb + B, :] for tb in range(nsb)], axis=1)


def _hy_conv_kernel(z_ref, x1_ref, x2_ref, kk_ref, o_ref, g_ref, u_ref, acc_ref):
    L = z_ref.shape[2]

    def compute(ci, slot):
        y0 = _hy_toeplitz(z_ref[ci].astype(F32), g_ref.at[slot, 0], u_ref, acc_ref, L)
        z1 = x1_ref[ci].astype(F32) * y0
        y1 = _hy_toeplitz(z1, g_ref.at[slot, 1], u_ref, acc_ref, L)
        o_ref[ci] = (x2_ref[ci].astype(F32) * y1).astype(BF16)

    def build(ci, slot):
        _hy_build_tables(kk_ref, ci, g_ref.at[slot, 0], g_ref.at[slot, 1], L)

    build(0, 0)

    def body(k, carry):
        build(2 * k + 1, 1)
        compute(2 * k, 0)
        build(jnp.minimum(2 * k + 2, HY_CB - 1), 0)
        compute(2 * k + 1, 1)
        return carry

    lax.fori_loop(0, HY_CB // 2, body, 0, unroll=4)


def _hy_conv(zc3, kk):
    _, B, L = zc3.shape
    nblk = W_BR // HY_CB
    bp = -(-B // BF16_ROWS) * BF16_ROWS
    nsb = L // HY_KT
    act = lambda off: pl.BlockSpec((HY_CB, B, L), lambda c: (c + off * nblk, 0, 0))
    return pl.pallas_call(
        _hy_conv_kernel,
        out_shape=jax.ShapeDtypeStruct((W_BR, B, L), BF16),
        grid=(nblk,),
        in_specs=[act(0), act(1), act(2),
                  pl.BlockSpec((2, HY_CB, 2 * L), lambda c: (0, c, 0))],
        out_specs=pl.BlockSpec((HY_CB, B, L), lambda c: (c, 0, 0)),
        scratch_shapes=[pltpu.VMEM((2, 2, 2 * L - LANES, LANES), BF16),
                        pltpu.VMEM((nsb * bp, HY_KT), BF16),
                        pltpu.VMEM((nsb * bp, HY_KT), F32)],
        compiler_params=_params(("parallel",), 48),
        name="hy_conv",
    )(zc3, zc3, zc3, kk)


HY_POST_TT = 8192


def _hy_post_kernel(y_ref, o_ref):
    o_ref[...] = y_ref[...].astype(F32).T.astype(BF16)


def _hy_post(yc):
    C, T = yc.shape
    return pl.pallas_call(
        _hy_post_kernel,
        out_shape=jax.ShapeDtypeStruct((T, C), BF16),
        grid=(T // HY_POST_TT, C // LANES),
        in_specs=[pl.BlockSpec((LANES, HY_POST_TT), lambda t, c: (c, t))],
        out_specs=pl.BlockSpec((HY_POST_TT, LANES), lambda t, c: (t, c)),
        compiler_params=_params(("parallel", "parallel"), 48),
        name="hy_post",
    )(yc)


def _sgu_kernel(u_ref, v_ref, g_ref, b_ref, ws_ref, bs_ref, o_ref):
    L = u_ref.shape[0]
    gw = W_BR // SGU_GROUPS

    def body(n, carry):
        rows = pl.ds(pl.multiple_of(n * SGU_CHUNK, SGU_CHUNK), SGU_CHUNK)
        v = v_ref[rows, :].astype(F32)
        mu = jnp.mean(v, axis=-1, keepdims=True)
        d = v - mu
        var = jnp.mean(d * d, axis=-1, keepdims=True)
        vn = (d * lax.rsqrt(var + EPS) * g_ref[...] + b_ref[...]).astype(BF16)
        mixed = jnp.concatenate(
            [jnp.dot(ws_ref[k], vn[:, gw * k:gw * (k + 1)], preferred_element_type=F32) + bs_ref[k]
             for k in range(SGU_GROUPS)], axis=1)
        o_ref[rows, :] = (u_ref[rows, :].astype(F32) * mixed).astype(BF16)
        return carry

    lax.fori_loop(0, L // SGU_CHUNK, body, 0, unroll=4)


def _sgu(proj3, ln_g, ln_b, ws, bs):
    B, L, _ = proj3.shape
    cb = COL_SGU // W_BR
    return pl.pallas_call(
        _sgu_kernel,
        out_shape=jax.ShapeDtypeStruct((B, L, W_BR), BF16),
        grid=(B,),
        in_specs=[pl.BlockSpec((None, L, W_BR), lambda b: (b, 0, cb)),
                  pl.BlockSpec((None, L, W_BR), lambda b: (b, 0, cb + 1)),
                  pl.BlockSpec((1, W_BR), lambda b: (0, 0)),
                  pl.BlockSpec((1, W_BR), lambda b: (0, 0)),
                  pl.BlockSpec((SGU_GROUPS, SGU_CHUNK, SGU_CHUNK), lambda b: (0, 0, 0)),
                  pl.BlockSpec((SGU_GROUPS, SGU_CHUNK, 1), lambda b: (0, 0, 0))],
        out_specs=pl.BlockSpec((None, L, W_BR), lambda b: (b, 0, 0)),
        compiler_params=_params(("parallel",), 48),
        name="sgu",
    )(proj3, proj3, ln_g, ln_b, ws, bs)


RG_SLABS = 2
RG_TL = 256
RG_PAD = 8
RG_SKEW = 8
RG_UNROLL = 8


def _rg_kernel(xb_ref, gb_ref, cw_ref, cb_ref, w_ref, bias_ref, lam_ref, o_ref,
               xpad_ref, nat_ref, xi_ref, af_ref, bf_ref, ab_ref, bb_ref):
    L = xb_ref.shape[0]
    seg = L // SUBLANES
    pitch = seg + RG_SKEW
    zeros = jnp.zeros((RG_PAD, LANES), F32)
    sp_all = []
    for s in range(RG_SLABS):
        cols = slice(s * LANES, (s + 1) * LANES)
        xpad_ref[s, 0:RG_PAD, :] = zeros
        xpad_ref[s, RG_PAD + L:RG_PAD + L + RG_PAD, :] = zeros
        xpad_ref[s, RG_PAD:RG_PAD + L, :] = xb_ref[:, cols].astype(F32)
        cw = cw_ref[:, cols]
        for j in range(SUBLANES):
            t0 = j * seg
            xc = cb_ref[:, cols]
            for k in range(4):
                xc = xc + cw[k:k + 1] * xpad_ref[s, RG_PAD + t0 + k - 2:RG_PAD + t0 + k - 2 + seg, :]
            nat_ref[s, pitch * j:pitch * j + seg, :] = xc
        lam = lam_ref[:, cols]
        sp_all.append(jnp.maximum(-lam, 0.0) + jnp.log(1.0 + jnp.exp(-jnp.abs(lam))))

    def interleave(r, carry):
        dst = pl.ds(pl.multiple_of(r * SUBLANES, SUBLANES), SUBLANES)
        for s in range(RG_SLABS):
            xi_ref[s, dst, :] = nat_ref[s, pl.ds(r, SUBLANES, stride=pitch), :]
        return carry

    lax.fori_loop(0, seg, interleave, 0, unroll=RG_UNROLL)

    for s in range(RG_SLABS):
        for ti in range(L // RG_TL):
            rows = slice(ti * RG_TL, (ti + 1) * RG_TL)
            xc = xi_ref[s, rows, :]
            g = jnp.dot(xc.astype(BF16), w_ref[s], preferred_element_type=F32) + bias_ref[s]
            g = 0.5 * jnp.tanh(0.5 * g) + 0.5
            for d, (a_ref, b_ref) in enumerate(((af_ref, bf_ref), (ab_ref, bb_ref))):
                r = g[:, 2 * d * LANES:(2 * d + 1) * LANES]
                i = g[:, (2 * d + 1) * LANES:(2 * d + 2) * LANES]
                a = jnp.exp((-RG_C) * r * sp_all[s][d:d + 1])
                om = 1.0 - a * a
                a_ref[s, rows, :] = a
                b_ref[s, rows, :] = (om * lax.rsqrt(jnp.maximum(om, 1e-30))) * (i * xc)

    def rows_f(r):
        return pl.ds(pl.multiple_of(r * SUBLANES, SUBLANES), SUBLANES)

    def rows_b(r):
        return pl.ds(pl.multiple_of((seg - 1 - r) * SUBLANES, SUBLANES), SUBLANES)

    def pass1(r, carry):
        out = []
        for s in range(RG_SLABS):
            hf, pf, hb, pb = carry[4 * s:4 * s + 4]
            a = af_ref[s, rows_f(r), :]
            hf = a * hf + bf_ref[s, rows_f(r), :]
            pf = a * pf
            a = ab_ref[s, rows_b(r), :]
            hb = a * hb + bb_ref[s, rows_b(r), :]
            pb = a * pb
            out += [hf, pf, hb, pb]
        return tuple(out)

    z = jnp.zeros((SUBLANES, LANES), F32)
    one = jnp.ones((SUBLANES, LANES), F32)
    ends = lax.fori_loop(0, seg, pass1, (z, one, z, one) * RG_SLABS, unroll=RG_UNROLL)

    init = []
    for s in range(RG_SLABS):
        hf, pf, hb, pb = ends[4 * s:4 * s + 4]
        c = jnp.zeros((1, LANES), F32)
        rows = []
        for j in range(SUBLANES):
            rows.append(c)
            c = pf[j:j + 1] * c + hf[j:j + 1]
        init.append(jnp.concatenate(rows, axis=0))
        c = jnp.zeros((1, LANES), F32)
        rows = []
        for j in range(SUBLANES - 1, -1, -1):
            rows.append(c)
            c = pb[j:j + 1] * c + hb[j:j + 1]
        init.append(jnp.concatenate(rows[::-1], axis=0))

    def pass2(r, carry):
        out = []
        for s in range(RG_SLABS):
            hf, hb = carry[2 * s:2 * s + 2]
            hf = af_ref[s, rows_f(r), :] * hf + bf_ref[s, rows_f(r), :]
            xi_ref[s, rows_f(r), :] = hf
            hb = ab_ref[s, rows_b(r), :] * hb + bb_ref[s, rows_b(r), :]
            xpad_ref[s, rows_b(r), :] = hb
            out += [hf, hb]
        return tuple(out)

    lax.fori_loop(0, seg, pass2, tuple(init), unroll=RG_UNROLL)

    def deinterleave(r, carry):
        src = rows_f(r)
        for s in range(RG_SLABS):
            nat_ref[s, pl.ds(r, SUBLANES, stride=pitch), :] = xi_ref[s, src, :] + xpad_ref[s, src, :]
        return carry

    lax.fori_loop(0, seg, deinterleave, 0, unroll=RG_UNROLL)

    for s in range(RG_SLABS):
        cols = slice(s * LANES, (s + 1) * LANES)
        for j in range(SUBLANES):
            gb = gb_ref[j * seg:(j + 1) * seg, cols].astype(F32)
            gelu = 0.5 * gb * (1.0 + jnp.tanh(math.sqrt(2.0 / math.pi) * (gb + 0.044715 * (gb * gb * gb))))
            o_ref[j * seg:(j + 1) * seg, cols] = (nat_ref[s, pitch * j:pitch * j + seg, :] * gelu).astype(BF16)


def _rglru(proj3, conv_w, conv_b, w_bd, bias_bd, lam):
    B, L, _ = proj3.shape
    gw = RG_SLABS * LANES
    ng = W_BR // gw
    cx = COL_RG // gw
    seg = L // SUBLANES
    slab = lambda rows: pltpu.VMEM((RG_SLABS, rows, LANES), F32)
    return pl.pallas_call(
        _rg_kernel,
        out_shape=jax.ShapeDtypeStruct((B, L, W_BR), BF16),
        grid=(B, ng),
        in_specs=[pl.BlockSpec((None, L, gw), lambda b, c: (b, 0, cx + c)),
                  pl.BlockSpec((None, L, gw), lambda b, c: (b, 0, cx + ng + c)),
                  pl.BlockSpec((4, gw), lambda b, c: (0, c)),
                  pl.BlockSpec((1, gw), lambda b, c: (0, c)),
                  pl.BlockSpec((RG_SLABS, LANES, 4 * LANES), lambda b, c: (c, 0, 0)),
                  pl.BlockSpec((RG_SLABS, 1, 4 * LANES), lambda b, c: (c, 0, 0)),
                  pl.BlockSpec((2, gw), lambda b, c: (0, c))],
        out_specs=pl.BlockSpec((None, L, gw), lambda b, c: (b, 0, c)),
        scratch_shapes=[slab(L + 2 * RG_PAD), slab(SUBLANES * (seg + RG_SKEW))] + [slab(L)] * 5,
        compiler_params=_params(("parallel", "parallel"), 48),
        name="rglru",
    )(proj3, proj3, conv_w, conv_b, w_bd, bias_bd, lam)


def _log_sigmoid(x):
    return jnp.minimum(x, 0.0) - jnp.log(1.0 + jnp.exp(-jnp.abs(x)))


ML_AUG = 16


def _ml_kernel(qt_ref, vt_ref, ot_ref, k_ref, gtt_ref, gbias_ref, ng_ref, y_ref,
               hf_ref, hb_ref, li_ref, b_ref, cc_ref, *state_refs):
    L = k_ref.shape[0]
    nc = L // ML_CHUNK
    ng2 = 2 * ML_HEADS
    row = lax.broadcasted_iota(jnp.int32, (ML_CHUNK, ML_CHUNK), 0)
    col = lax.broadcasted_iota(jnp.int32, (ML_CHUNK, ML_CHUNK), 1)
    scale = ML_HD ** -0.5
    masks = (col <= row, col >= row)
    for ref in state_refs:
        ref[...] = jnp.zeros(ref.shape, F32)

    li = gtt_ref[0:ng2, :] + gbias_ref[0:ng2, :]
    lf = _log_sigmoid(gtt_ref[ng2:2 * ng2, :] + gbias_ref[ng2:2 * ng2, :])
    pos = lax.broadcasted_iota(jnp.int32, (ng2, L), 1) & (ML_CHUNK - 1)
    pre_sum = lf
    suf_sum = lf
    k = 1
    while k < ML_CHUNK:
        pre_sum = pre_sum + jnp.where(pos >= k, pltpu.roll(pre_sum, k, 1), 0.0)
        suf_sum = suf_sum + jnp.where(pos < ML_CHUNK - k, pltpu.roll(suf_sum, L - k, 1), 0.0)
        k *= 2
    causal_row = lax.broadcasted_iota(jnp.int32, (ng2, L), 0) < ML_HEADS
    b_all = jnp.where(causal_row, pre_sum, suf_sum)
    li_ref[...] = li
    b_ref[...] = b_all
    diff = jnp.concatenate([li - b_all, jnp.zeros((LANES - ng2, L), F32)], axis=0)
    for c in range(nc):
        cc_ref[c * ML_CHUNK:(c + 1) * ML_CHUNK, :] = diff[:, c * ML_CHUNK:(c + 1) * ML_CHUNK].T

    def body(step, carry):
        for d in range(2):
            c = step if d == 0 else nc - 1 - step
            tsl = pl.ds(pl.multiple_of(c * ML_CHUNK, ML_CHUNK), ML_CHUNK)
            mask_st = masks[1 - d]
            last = ML_CHUNK - 1 if d == 0 else 0
            for hd in range(ML_HEADS):
                r = d * ML_HEADS + hd
                c_ref, m_ref = state_refs[2 * r], state_refs[2 * r + 1]
                b_row = b_ref[r:r + 1, tsl]
                li_row = li_ref[r:r + 1, tsl]
                c_col = cc_ref[tsl, r:r + 1]
                g_tot = b_row[:, last:last + 1]
                m_prev = m_ref[...]
                dlog = jnp.where(mask_st, b_row + c_col, -jnp.inf)
                inter = b_row + m_prev
                m_t = jnp.maximum(inter, jnp.max(dlog, axis=0, keepdims=True))
                w_intra = jnp.exp(dlog - m_t)
                w_inter = jnp.exp(inter - m_t)
                hs = slice(hd * ML_HD, (hd + 1) * ML_HD)
                qt = qt_ref[hs, tsl]
                kh = k_ref[tsl, hs]
                vt = vt_ref[hs, tsl]
                s = jnp.dot(kh, qt, preferred_element_type=F32) * (scale * w_intra)
                ca = c_ref[...]
                qc = jnp.dot(ca.astype(BF16), qt, preferred_element_type=F32) * scale
                num = w_inter * qc[0:ML_HD] + jnp.dot(vt, s.astype(BF16), preferred_element_type=F32)
                den = w_inter * qc[ML_HD:ML_HD + 1] + jnp.sum(s, axis=0, keepdims=True)
                hout = num * (1.0 / jnp.maximum(jnp.abs(den), jnp.exp(-m_t)))
                if d == 0:
                    hf_ref[hs, tsl] = hout
                else:
                    hb_ref[hs, tsl] = hout
                wlog = g_tot - b_row + li_row
                m_new = jnp.maximum(g_tot + m_prev, jnp.max(wlog, axis=1, keepdims=True))
                decay = jnp.exp(g_tot + m_prev - m_new)
                ws = jnp.exp(wlog - m_new)
                vw = jnp.concatenate([vt.astype(F32) * ws, jnp.broadcast_to(ws, (ML_AUG, ML_CHUNK))],
                                     axis=0).astype(BF16)
                c_ref[...] = decay * ca + jnp.dot(vw, kh, preferred_element_type=F32)
                m_ref[...] = m_new
        return carry

    lax.fori_loop(0, nc, body, 0, unroll=4)

    def finish(c, carry):
        tsl = pl.ds(pl.multiple_of(c * ML_CHUNK, ML_CHUNK), ML_CHUNK)
        for hd in range(ML_HEADS):
            hs = slice(hd * ML_HD, (hd + 1) * ML_HD)
            h = hf_ref[hs, tsl] + hb_ref[hs, tsl]
            hn = h * lax.rsqrt(jnp.mean(h * h, axis=0, keepdims=True) + EPS) * ng_ref[hs, :]
            y = _sigmoid(ot_ref[hs, tsl].astype(F32)) * hn
            y_ref[tsl, hs] = y.T.astype(BF16)
        return carry

    lax.fori_loop(0, nc, finish, 0, unroll=2)


def _mlstm(qvot, proj3, mlgt, gbias, norm_g):
    B, L, _ = proj3.shape
    cb = COL_ML // W_BR
    ng4 = 4 * ML_HEADS
    fm = lambda j: pl.BlockSpec((W_BR, L), lambda b: (j, b))
    return pl.pallas_call(
        _ml_kernel,
        out_shape=jax.ShapeDtypeStruct((B, L, W_BR), BF16),
        grid=(B,),
        in_specs=[fm(0), fm(1), fm(2),
                  pl.BlockSpec((None, L, W_BR), lambda b: (b, 0, cb)),
                  pl.BlockSpec((LANES, L), lambda b: (0, b)),
                  pl.BlockSpec((ng4, 1), lambda b: (0, 0)),
                  pl.BlockSpec((W_BR, LANES), lambda b: (0, 0))],
        out_specs=pl.BlockSpec((None, L, W_BR), lambda b: (b, 0, 0)),
        scratch_shapes=[pltpu.VMEM((W_BR, L), F32),
                        pltpu.VMEM((W_BR, L), F32),
                        pltpu.VMEM((2 * ML_HEADS, L), F32),
                        pltpu.VMEM((2 * ML_HEADS, L), F32),
                        pltpu.VMEM((L, LANES), F32)]
        + [pltpu.VMEM((ML_HD + ML_AUG, ML_HD), F32), pltpu.VMEM((1, 1), F32)] * (2 * ML_HEADS),
        compiler_params=_params(("parallel",), 48),
        name="mlstm",
    )(qvot, qvot, qvot, proj3, mlgt, gbias, jnp.broadcast_to(norm_g.reshape(W_BR, 1), (W_BR, LANES)))


def _merge_kernel(route, gate_ref, ya_ref, yb_ref, yc_ref, yd_ref, wb_ref, wo_ref, x_ref, g_ref, *rest):
    if route:
        rw_ref, rb_ref, xo_ref, hn_ref, comb_ref, combt_ref = rest
    else:
        xo_ref, hn_ref = rest
    merged = None
    for k, y_ref in enumerate((ya_ref, yb_ref, yc_ref, yd_ref)):
        t = jnp.dot(y_ref[...], wb_ref[k], preferred_element_type=F32)
        gk = _sigmoid(gate_ref[:, k * D_MODEL:(k + 1) * D_MODEL].astype(F32))
        merged = gk * t if merged is None else merged + gk * t
    xn = x_ref[...] + jnp.dot(merged.astype(BF16), wo_ref[...], preferred_element_type=F32)
    xo_ref[...] = xn
    h = _rms(xn, g_ref[...])
    h_hi = h.astype(BF16)
    hn_ref[...] = h_hi
    if route:
        h_lo = (h - h_hi.astype(F32)).astype(BF16)
        r_hi = jnp.dot(h_hi, rw_ref[...], preferred_element_type=F32)
        r_lo = jnp.dot(h_lo, rw_ref[:, 0:LANES], preferred_element_type=F32)
        logits = r_hi[:, 0:LANES] + r_hi[:, LANES:2 * LANES] + r_lo + rb_ref[...]
        lane = lax.broadcasted_iota(jnp.int32, logits.shape, 1)
        logits = jnp.where(lane < N_EXPERTS, logits, -jnp.inf)
        v1 = jnp.max(logits, axis=1, keepdims=True)
        i1 = jnp.min(jnp.where(logits == v1, lane, LANES), axis=1, keepdims=True)
        rest_l = jnp.where(lane == i1, -jnp.inf, logits)
        v2 = jnp.max(rest_l, axis=1, keepdims=True)
        i2 = jnp.min(jnp.where(rest_l == v2, lane, LANES), axis=1, keepdims=True)
        e2 = jnp.exp(v2 - v1)
        p1 = 1.0 / (1.0 + e2)
        comb = jnp.where(lane == i1, p1, jnp.where(lane == i2, e2 * p1, 0.0))
        comb_ref[...] = comb
        combt_ref[...] = comb.T


def _merge(proj, ya, yb, yc, yd, wb, wo, x, g, router=None, tm=512):
    T = x.shape[0]
    row = lambda w: pl.BlockSpec((tm, w), lambda i: (i, 0))
    in_specs = [row(N_BRANCH * D_MODEL), row(W_BR), row(W_BR), row(W_BR), row(W_BR),
                pl.BlockSpec((N_BRANCH, W_BR, D_MODEL), lambda i: (0, 0, 0)),
                pl.BlockSpec((D_MODEL, D_MODEL), lambda i: (0, 0)),
                row(D_MODEL),
                pl.BlockSpec((1, D_MODEL), lambda i: (0, 0))]
    out_shape = [jax.ShapeDtypeStruct((T, D_MODEL), F32), jax.ShapeDtypeStruct((T, D_MODEL), BF16)]
    out_specs = [row(D_MODEL), row(D_MODEL)]
    args = [proj, ya, yb, yc, yd, wb, wo, x, g]
    if router is not None:
        in_specs += [pl.BlockSpec((D_MODEL, 2 * LANES), lambda i: (0, 0)),
                     pl.BlockSpec((1, LANES), lambda i: (0, 0))]
        out_shape += [jax.ShapeDtypeStruct((T, LANES), F32), jax.ShapeDtypeStruct((LANES, T), F32)]
        out_specs += [row(LANES), pl.BlockSpec((LANES, tm), lambda i: (0, i))]
        args += list(router)
    return pl.pallas_call(
        functools.partial(_merge_kernel, router is not None),
        out_shape=tuple(out_shape),
        grid=(T // tm,),
        in_specs=in_specs,
        out_specs=tuple(out_specs),
        compiler_params=_params(("parallel",), 56),
        name="merge_route" if router is not None else "merge",
    )(*args)


def _swiglu_acc(h, w1_ref, w3_ref, w2_ref):
    a = jnp.dot(h, w1_ref[...], preferred_element_type=F32)
    b = jnp.dot(h, w3_ref[...], preferred_element_type=F32)
    act = (a * _sigmoid(a) * b).astype(BF16)
    return jnp.dot(act, w2_ref[...], preferred_element_type=F32)


def _ffn_kernel(h_ref, x_ref, w1_ref, w3_ref, w2_ref, o_ref, acc_ref):
    f = pl.program_id(1)

    @pl.when(f == 0)
    def _():
        acc_ref[...] = x_ref[...]

    acc_ref[...] += _swiglu_acc(h_ref[...], w1_ref, w3_ref, w2_ref)

    @pl.when(f == pl.num_programs(1) - 1)
    def _():
        o_ref[...] = acc_ref[...]


def _ffn(hn, x, w1, w3, w2, tm=512, n_split=2):
    T = x.shape[0]
    dff = w1.shape[1]
    tf = dff // n_split
    return pl.pallas_call(
        _ffn_kernel,
        out_shape=jax.ShapeDtypeStruct((T, D_MODEL), F32),
        grid=(T // tm, n_split),
        in_specs=[pl.BlockSpec((tm, D_MODEL), lambda i, f: (i, 0)),
                  pl.BlockSpec((tm, D_MODEL), lambda i, f: (i, 0)),
                  pl.BlockSpec((D_MODEL, tf), lambda i, f: (0, f)),
                  pl.BlockSpec((D_MODEL, tf), lambda i, f: (0, f)),
                  pl.BlockSpec((tf, D_MODEL), lambda i, f: (f, 0))],
        out_specs=pl.BlockSpec((tm, D_MODEL), lambda i, f: (i, 0)),
        scratch_shapes=[pltpu.VMEM((tm, D_MODEL), F32)],
        compiler_params=_params(("parallel", "arbitrary"), 56),
        name="ffn",
    )(hn, x, w1, w3, w2)


MOE_TM = 1024
MOE_CH = 256


def _moe_kernel(cnt_ref, h_ref, x_ref, comb_ref, combt_ref, w1_ref, w3_ref, w2_ref, o_ref,
                rank_ref, rankt_ref):
    i = pl.program_id(0)
    e = pl.program_id(1)
    tm = h_ref.shape[0]

    @pl.when(e == 0)
    def _():
        o_ref[...] = x_ref[...]
        r = lax.broadcasted_iota(jnp.int32, (tm, tm), 0)
        c = lax.broadcasted_iota(jnp.int32, (tm, tm), 1)
        before = jnp.where(c < r, 1.0, 0.0).astype(BF16)
        sel = jnp.where(comb_ref[...] > 0.0, 1.0, 0.0).astype(BF16)
        rank_ref[...] = jnp.dot(before, sel, preferred_element_type=F32)
        selt = jnp.where(combt_ref[...] > 0.0, 1.0, 0.0).astype(BF16)
        rankt_ref[...] = lax.dot_general(selt, before, (((1,), (1,)), ((), ())),
                                         preferred_element_type=F32)

    lane = lax.broadcasted_iota(jnp.int32, (tm, LANES), 1)
    comb = comb_ref[...]
    c_col = jnp.sum(jnp.where(lane == e, comb, 0.0), axis=1, keepdims=True)
    rank_col = jnp.sum(jnp.where(lane == e, rank_ref[...], 0.0), axis=1, keepdims=True)
    rank_col = jnp.where(c_col > 0.0, rank_col, -1.0)
    c_row = combt_ref[pl.ds(e, 1), :]
    rank_row = jnp.where(c_row > 0.0, rankt_ref[pl.ds(e, 1), :], -1.0)
    cnt = cnt_ref[i * N_EXPERTS + e]

    def run_chunk(base, size):
        pos_r = (lax.broadcasted_iota(jnp.int32, (size, tm), 0) + base).astype(F32)
        gather = jnp.where(rank_row == pos_r, 1.0, 0.0).astype(BF16)
        xs = jnp.dot(gather, h_ref[...], preferred_element_type=F32).astype(BF16)
        y = _swiglu_acc(xs, w1_ref, w3_ref, w2_ref).astype(BF16)
        pos_c = (lax.broadcasted_iota(jnp.int32, (tm, size), 1) + base).astype(F32)
        scatter = jnp.where(rank_col == pos_c, c_col, 0.0).astype(BF16)
        o_ref[...] += jnp.dot(scatter, y, preferred_element_type=F32)

    n_full = cnt // MOE_CH
    rem = cnt - n_full * MOE_CH

    def body(j, carry):
        run_chunk(j * MOE_CH, MOE_CH)
        return carry

    lax.fori_loop(0, n_full, body, 0)

    @pl.when(rem > 3 * MOE_CH // 4)
    def _():
        run_chunk(n_full * MOE_CH, MOE_CH)

    @pl.when((rem > MOE_CH // 2) & (rem <= 3 * MOE_CH // 4))
    def _():
        run_chunk(n_full * MOE_CH, 3 * MOE_CH // 4)

    @pl.when((rem > 0) & (rem <= MOE_CH // 2))
    def _():
        run_chunk(n_full * MOE_CH, MOE_CH // 2)


def _moe(hn, x, comb, combt, w1, w3, w2, tm=MOE_TM):
    T = x.shape[0]
    dfe = w1.shape[2]
    nt = T // tm
    cnt = jnp.sum((comb[:, :N_EXPERTS] > 0.0).reshape(nt, tm, N_EXPERTS), axis=1, dtype=jnp.int32).reshape(-1)
    grid_spec = pltpu.PrefetchScalarGridSpec(
        num_scalar_prefetch=1,
        grid=(nt, N_EXPERTS),
        in_specs=[pl.BlockSpec((tm, D_MODEL), lambda i, e, c: (i, 0)),
                  pl.BlockSpec((tm, D_MODEL), lambda i, e, c: (i, 0)),
                  pl.BlockSpec((tm, LANES), lambda i, e, c: (i, 0)),
                  pl.BlockSpec((LANES, tm), lambda i, e, c: (0, i)),
                  pl.BlockSpec((None, D_MODEL, dfe), lambda i, e, c: (e, 0, 0)),
                  pl.BlockSpec((None, D_MODEL, dfe), lambda i, e, c: (e, 0, 0)),
                  pl.BlockSpec((None, dfe, D_MODEL), lambda i, e, c: (e, 0, 0))],
        out_specs=pl.BlockSpec((tm, D_MODEL), lambda i, e, c: (i, 0)),
        scratch_shapes=[pltpu.VMEM((tm, LANES), F32), pltpu.VMEM((LANES, tm), F32)])
    return pl.pallas_call(
        _moe_kernel,
        out_shape=jax.ShapeDtypeStruct((T, D_MODEL), F32),
        grid_spec=grid_spec,
        compiler_params=_params(("parallel", "arbitrary"), 56),
        name="moe",
    )(cnt, hn, x, comb, combt, w1, w3, w2)


def _final_norm_kernel(x_ref, g_ref, o_ref):
    o_ref[...] = _rms(x_ref[...], g_ref[...])


def _final_norm(x, g, row0, rows, tm=1024):
    first = row0 // tm
    return pl.pallas_call(
        _final_norm_kernel,
        out_shape=jax.ShapeDtypeStruct((rows, D_MODEL), F32),
        grid=(rows // tm,),
        in_specs=[pl.BlockSpec((tm, D_MODEL), lambda i: (i + first, 0)),
                  pl.BlockSpec((1, D_MODEL), lambda i: (0, 0))],
        out_specs=pl.BlockSpec((tm, D_MODEL), lambda i: (i, 0)),
        compiler_params=_params(("parallel",), 48),
        name="final_norm",
    )(x, g)


def _split_w_in(w_in_l):
    n_hy, n_sgu, n_rg = 3 * W_BR, 2 * W_BR, 2 * W_BR
    o1 = n_hy
    o2 = o1 + n_sgu
    o3 = o2 + n_rg
    o4 = o3 + 4 * W_BR
    o5 = o4 + 4 * ML_HEADS
    w = jnp.concatenate([w_in_l[:, o5:], w_in_l[:, :o3], w_in_l[:, o3 + W_BR:o3 + 2 * W_BR]],
                        axis=1).astype(BF16)
    wgt = jnp.pad(w_in_l[:, o4:o5], ((0, 0), (0, LANES - 4 * ML_HEADS))).T.astype(BF16)
    wft = jnp.concatenate([w_in_l[:, o3:o3 + W_BR], w_in_l[:, o3 + 2 * W_BR:o4]], axis=1).T.astype(BF16)
    return w, wgt, wft


def _router_operands(router_w, router_b):
    w = jnp.pad(router_w, ((0, 0), (0, LANES - N_EXPERTS)))
    w_hi = w.astype(BF16)
    w_lo = (w - w_hi.astype(F32)).astype(BF16)
    return (jnp.concatenate([w_hi, w_lo], axis=1),
            jnp.pad(router_b, (0, LANES - N_EXPERTS))[None, :])


def _rg_block_diag(wa, ba, wx, bx):
    hpg = LANES // RG_HD
    ng = RG_HEADS // hpg
    eye = jnp.eye(hpg, dtype=F32)

    def bd(w):
        w = w.reshape(ng, hpg, RG_HD, RG_HD)
        return jnp.einsum('gaij,ab->gaibj', w, eye).reshape(ng, LANES, LANES)

    w = jnp.concatenate([bd(wa[0]), bd(wx[0]), bd(wa[1]), bd(wx[1])], axis=2).astype(BF16)
    fl = lambda b: b.reshape(ng, 1, LANES)
    bias = jnp.concatenate([fl(ba[0]), fl(bx[0]), fl(ba[1]), fl(bx[1])], axis=2)
    return w, bias


def _token_mixer(x, B, L, l, norm_g, w_in, kk, hy_conv_w, hy_conv_b, sgu_ln_g, sgu_ln_b, sgu_ws, sgu_bs,
                 rg_conv_w, rg_conv_b, rg_wa, rg_ba, rg_wx, rg_bx, rg_lambda, ml_i_bias, ml_f_bias,
                 ml_norm_g, w_branch, w_out, next_g, router):
    T = B * L
    w, wgt, wft = _split_w_in(w_in[l])
    proj, mlgt, qvot = _inproj(x, norm_g[l][None, :], w, wgt, wft)
    proj3 = proj.reshape(B, L, N_PROJ)

    zc = _hy_pre(proj3, hy_conv_w[l], hy_conv_b[l][None, :])
    ya_c = _hy_conv(zc.reshape(3 * W_BR, B, L), kk[l])
    ya = _hy_post(ya_c.reshape(W_BR, T))

    yb = _sgu(proj3, sgu_ln_g[l][None, :], sgu_ln_b[l][None, :], sgu_ws[l].astype(BF16),
              sgu_bs[l][:, :, None]).reshape(T, W_BR)

    w_bd, bias_bd = _rg_block_diag(rg_wa[l], rg_ba[l], rg_wx[l], rg_bx[l])
    yc = _rglru(proj3, rg_conv_w[l], rg_conv_b[l][None, :], w_bd, bias_bd, rg_lambda[l]).reshape(T, W_BR)

    gbias = jnp.concatenate([ml_i_bias[l].reshape(-1), ml_f_bias[l].reshape(-1)])[:, None]
    yd = _mlstm(qvot, proj3, mlgt, gbias, ml_norm_g[l]).reshape(T, W_BR)

    return _merge(proj, ya, yb, yc, yd, w_branch[l].astype(BF16), w_out[l].astype(BF16), x,
                  next_g[None, :], router)


def kernel(x_prompt, x_sample, norm_mix_g, w_in, hy_conv_w, hy_conv_b, hy_ffn_w1, hy_ffn_b1, hy_ffn_w2, hy_ffn_b2, hy_ffn_w3, hy_bias, sgu_ln_g, sgu_ln_b, sgu_ws, sgu_bs, rg_conv_w, rg_conv_b, rg_wa, rg_ba, rg_wx, rg_bx, rg_lambda, ml_i_bias, ml_f_bias, ml_norm_g, w_branch, w_out, norm_ffn_g, ffn_w1, ffn_w3, ffn_w2, router_w, router_b, moe_w1, moe_w3, moe_w2, norm_final_g):
    bp, L, _ = x_prompt.shape
    bs = x_sample.shape[0]
    B = bp + bs
    depth = w_in.shape[0]
    x = jnp.concatenate([x_prompt, x_sample], axis=0).reshape(B * L, D_MODEL)
    kk = _hy_filters(hy_ffn_w1, hy_ffn_b1, hy_ffn_w2, hy_ffn_b2, hy_ffn_w3, hy_bias, L)
    for l in range(depth):
        j = l // 2
        router = None
        if l % 2 == 1:
            router = _router_operands(router_w[j], router_b[j])
        outs = _token_mixer(x, B, L, l, norm_mix_g, w_in, kk, hy_conv_w, hy_conv_b, sgu_ln_g, sgu_ln_b,
                            sgu_ws, sgu_bs, rg_conv_w, rg_conv_b, rg_wa, rg_ba, rg_wx, rg_bx, rg_lambda,
                            ml_i_bias, ml_f_bias, ml_norm_g, w_branch, w_out, norm_ffn_g[l], router)
        if router is None:
            x, hn = outs
            x = _ffn(hn, x, ffn_w1[j].astype(BF16), ffn_w3[j].astype(BF16), ffn_w2[j].astype(BF16))
        else:
            x, hn, comb, combt = outs
            x = _moe(hn, x, comb, combt, moe_w1[j].astype(BF16), moe_w3[j].astype(BF16),
                     moe_w2[j].astype(BF16))
    g = norm_final_g[None, :]
    return (_final_norm(x, g, 0, bp * L).reshape(bp, L, D_MODEL),
            _final_norm(x, g, bp * L, bs * L).reshape(bs, L, D_MODEL))
```

```python
import functools
import math

import jax
import jax.numpy as jnp
import numpy as np
from jax import lax
from jax.experimental import pallas as pl
from jax.experimental.pallas import tpu as pltpu

F32 = jnp.float32
BF16 = jnp.bfloat16
EPS = 1e-6
HIGHEST = lax.Precision.HIGHEST

LANES = 128
SUBLANES = 8
VMEM_BYTES_V7X = 64 * 1024 * 1024

D_MODEL = 1024
W_BR = 512
N_BRANCH = 4
HY_BANDS = 16
HY_EMB = 1 + 2 * HY_BANDS
HY_FFN = 64
HY_TARGET = 1e-2
HY_MIN_DECAY = -math.log(HY_TARGET) / 1.5
HY_MAX_DECAY = -math.log(HY_TARGET) / 0.3
SGU_CHUNK = 128
SGU_GROUPS = 4
RG_HEADS = 8
RG_HD = W_BR // RG_HEADS
RG_C = 8.0
ML_HEADS = 4
ML_HD = W_BR // ML_HEADS
ML_CHUNK = 128
N_EXPERTS = 8

N_PROJ = N_BRANCH * D_MODEL + 3 * W_BR + 2 * W_BR + 2 * W_BR + W_BR
COL_GATE = 0
COL_HY = N_BRANCH * D_MODEL
COL_SGU = COL_HY + 3 * W_BR
COL_RG = COL_SGU + 2 * W_BR
COL_ML = COL_RG + 2 * W_BR


def _params(semantics, vmem_mb):
    return pltpu.CompilerParams(dimension_semantics=semantics,
                                vmem_limit_bytes=vmem_mb * 1024 * 1024)


def _sigmoid(x):
    return 0.5 * jnp.tanh(0.5 * x) + 0.5


def _rms(x, g):
    return x * lax.rsqrt(jnp.mean(x * x, axis=-1, keepdims=True) + EPS) * g


_NT = (((1,), (1,)), ((), ()))


def _inproj_kernel(x_ref, g_ref, w_ref, wgt_ref, wft_ref, proj_ref, mlgt_ref, ft_ref, h_ref):
    @pl.when(pl.program_id(1) == 0)
    def _():
        h = _rms(x_ref[...], g_ref[...]).astype(BF16)
        h_ref[...] = h
        mlgt_ref[...] = lax.dot_general(wgt_ref[...], h, _NT, preferred_element_type=F32)
        ft_ref[...] = lax.dot_general(wft_ref[...], h, _NT, preferred_element_type=F32).astype(BF16)

    proj_ref[...] = jnp.dot(h_ref[...], w_ref[...], preferred_element_type=F32).astype(BF16)


def _inproj(x, g, w, wgt, wft, tm=1024, n_split=4):
    T = x.shape[0]
    tn = N_PROJ // n_split
    nf = wft.shape[0]
    return pl.pallas_call(
        _inproj_kernel,
        out_shape=(jax.ShapeDtypeStruct((T, N_PROJ), BF16),
                   jax.ShapeDtypeStruct((LANES, T), F32),
                   jax.ShapeDtypeStruct((nf, T), BF16)),
        grid=(T // tm, n_split),
        in_specs=[pl.BlockSpec((tm, D_MODEL), lambda i, j: (i, 0)),
                  pl.BlockSpec((1, D_MODEL), lambda i, j: (0, 0)),
                  pl.BlockSpec((D_MODEL, tn), lambda i, j: (0, j)),
                  pl.BlockSpec((LANES, D_MODEL), lambda i, j: (0, 0)),
                  pl.BlockSpec((nf, D_MODEL), lambda i, j: (0, 0))],
        out_specs=(pl.BlockSpec((tm, tn), lambda i, j: (i, j)),
                   pl.BlockSpec((LANES, tm), lambda i, j: (0, i)),
                   pl.BlockSpec((nf, tm), lambda i, j: (0, i))),
        scratch_shapes=[pltpu.VMEM((tm, D_MODEL), BF16)],
        compiler_params=_params(("parallel", "arbitrary"), 56),
        name="inproj",
    )(x, g, w, wgt, wft)


def _hy_filter_kernel(z_ref, w1_ref, b1_ref, w2_ref, b2_ref, w3_ref, delta_ref, bias_ref, kk_ref):
    half = pl.program_id(2)
    z = z_ref[...]
    h = jnp.sin(jnp.dot(w1_ref[...], z, precision=HIGHEST, preferred_element_type=F32) + b1_ref[...])
    h = jnp.sin(jnp.dot(w2_ref[...], h, precision=HIGHEST, preferred_element_type=F32) + b2_ref[...])
    f = jnp.dot(w3_ref[...], h, precision=HIGHEST, preferred_element_type=F32)
    t_norm = z[0:1, :]
    f = f * jnp.exp(-t_norm * delta_ref[...])
    lane = lax.broadcasted_iota(jnp.int32, f.shape, 1)
    first = lane == 0
    f = jnp.where(first, jnp.where(half == 0, 0.0, f + bias_ref[...]), f)
    kk_ref[...] = f


def _hy_filters(hy_ffn_w1, hy_ffn_b1, hy_ffn_w2, hy_ffn_b2, hy_ffn_w3, hy_bias, L):
    depth = hy_ffn_w1.shape[0]
    lag = np.stack([L - np.arange(L), np.arange(L)]).astype(np.float64)
    bands = np.arange(1, HY_BANDS + 1, dtype=np.float64)
    ang = (2.0 * math.pi / L) * lag[:, None, :] * bands[None, :, None]
    z = np.concatenate([lag[:, None, :] / L, np.cos(ang), np.sin(ang)], axis=1)
    z = np.pad(z, ((0, 0), (0, LANES - HY_EMB), (0, 0))).astype(np.float32)
    z = jnp.asarray(z)
    w1t = jnp.pad(jnp.swapaxes(hy_ffn_w1, 1, 2), ((0, 0), (0, 0), (0, LANES - HY_EMB)))
    w2t = jnp.swapaxes(hy_ffn_w2, 1, 2)
    w3t = jnp.swapaxes(hy_ffn_w3, 1, 2).reshape(depth, 4, W_BR, HY_FFN)
    b1 = hy_ffn_b1[:, :, None]
    b2 = hy_ffn_b2[:, :, None]
    delta = jnp.linspace(HY_MIN_DECAY, HY_MAX_DECAY, W_BR, dtype=F32)[:, None]
    bias = hy_bias[:, :, :, None]
    return pl.pallas_call(
        _hy_filter_kernel,
        out_shape=jax.ShapeDtypeStruct((depth, 2, W_BR, 2 * L), F32),
        grid=(depth, 2, 2),
        in_specs=[pl.BlockSpec((None, LANES, L), lambda l, o, s: (s, 0, 0)),
                  pl.BlockSpec((None, HY_FFN, LANES), lambda l, o, s: (l, 0, 0)),
                  pl.BlockSpec((None, HY_FFN, 1), lambda l, o, s: (l, 0, 0)),
                  pl.BlockSpec((None, HY_FFN, HY_FFN), lambda l, o, s: (l, 0, 0)),
                  pl.BlockSpec((None, HY_FFN, 1), lambda l, o, s: (l, 0, 0)),
                  pl.BlockSpec((None, None, W_BR, HY_FFN), lambda l, o, s: (l, 2 * o + 1 - s, 0, 0)),
                  pl.BlockSpec((W_BR, 1), lambda l, o, s: (0, 0)),
                  pl.BlockSpec((None, None, W_BR, 1), lambda l, o, s: (l, o, 0, 0))],
        out_specs=pl.BlockSpec((None, None, W_BR, L), lambda l, o, s: (l, o, 0, s)),
        compiler_params=_params(("parallel", "parallel", "parallel"), 48),
        name="hy_filters",
    )(z, w1t, b1, w2t, b2, w3t, delta, bias)


HY_PRE_NB = 8


def _hy_pre_kernel(p_ref, w_ref, b_ref, o_ref):
    L = p_ref.shape[1]
    w = w_ref[...]
    row = lax.broadcasted_iota(jnp.int32, (L, LANES), 0)

    def body(bi, carry):
        x = p_ref[bi].astype(F32)
        xm = jnp.where(row == 0, 0.0, pltpu.roll(x, 1, 0))
        xp = jnp.where(row == L - 1, 0.0, pltpu.roll(x, L - 1, 0))
        u = b_ref[...] + w[0:1] * xm + w[1:2] * x + w[2:3] * xp
        o_ref[:, pl.ds(pl.multiple_of(bi * L, LANES), L)] = u.T.astype(BF16)
        return carry

    lax.fori_loop(0, HY_PRE_NB, body, 0, unroll=2)


def _hy_pre(proj3, conv_w, conv_b):
    B, L, _ = proj3.shape
    nc = 3 * W_BR // LANES
    return pl.pallas_call(
        _hy_pre_kernel,
        out_shape=jax.ShapeDtypeStruct((3 * W_BR, B * L), BF16),
        grid=(B // HY_PRE_NB, nc),
        in_specs=[pl.BlockSpec((HY_PRE_NB, L, LANES), lambda b, c: (b, 0, COL_HY // LANES + c)),
                  pl.BlockSpec((3, LANES), lambda b, c: (0, c)),
                  pl.BlockSpec((1, LANES), lambda b, c: (0, c))],
        out_specs=pl.BlockSpec((LANES, HY_PRE_NB * L), lambda b, c: (c, b)),
        compiler_params=_params(("parallel", "parallel"), 48),
        name="hy_pre",
    )(proj3, conv_w, conv_b)


HY_CB = 16
HY_KT = 2 * LANES
BF16_ROWS = 16


def _hy_build_tables(kk_ref, ci, g0_ref, g1_ref, L):
    nchunk = 2 * L // LANES
    upper = (lax.broadcasted_iota(jnp.int32, (LANES, LANES), 1)
             >= lax.broadcasted_iota(jnp.int32, (LANES, LANES), 0))

    def rolled(q):
        lo = 2 * LANES * (q // 2)
        half = slice(LANES * (q % 2), LANES * (q % 2 + 1))
        s0 = kk_ref[0, pl.ds(ci, 1), lo:lo + 2 * LANES][:, half]
        s1 = kk_ref[1, pl.ds(ci, 1), lo:lo + 2 * LANES][:, half]
        b0 = lax.bitcast_convert_type(s0.astype(BF16).astype(F32), jnp.uint32)
        b1 = lax.bitcast_convert_type(s1.astype(BF16).astype(F32), jnp.uint32)
        return pltpu.roll(jnp.broadcast_to(b0 | (b1 >> 16), (LANES, LANES)), 0, 1, stride=1, stride_axis=0)

    prev = rolled(nchunk - 1)
    for m in range(nchunk - 1):
        cur = rolled(nchunk - 2 - m)
        r = jnp.where(upper, prev, cur)
        rows = slice(LANES * m, LANES * (m + 1))
        g0_ref[rows, :] = lax.bitcast_convert_type(r & jnp.uint32(0xFFFF0000), F32).astype(BF16)
        g1_ref[rows, :] = lax.bitcast_convert_type(r << 16, F32).astype(BF16)
        prev = cur


def _hy_toeplitz(u, g_ref, u_ref, acc_ref, L):
    B = u.shape[0]
    nsb = L // HY_KT
    bp = u_ref.shape[0] // nsb
    pad = jnp.zeros((bp - B, HY_KT), F32)
    for sb in range(nsb):
        blk = u[:, HY_KT * sb:HY_KT * (sb + 1)]
        if bp > B:
            blk = jnp.concatenate([blk, pad], axis=0)
        u_ref[bp * sb:bp * (sb + 1), :] = blk.astype(BF16)
    acc_ref[...] = jnp.zeros(acc_ref.shape, F32)
    for delta in range(-(nsb - 1), nsb):
        d0 = HY_KT * delta + L - LANES
        tile = jnp.concatenate([g_ref[d0:d0 + HY_KT, :], g_ref[d0 - LANES:d0 - LANES + HY_KT, :]], axis=1)
        lo, hi = max(0, delta), min(nsb - 1, nsb - 1 + delta)
        part = jnp.dot(u_ref[bp * lo:bp * (hi + 1), :], tile, preferred_element_type=F32)
        acc_ref[bp * (lo - delta):bp * (hi - delta + 1), :] += part
    return jnp.concatenate([acc_ref[bp * tb:bp * tb + B, :] for tb in range(nsb)], axis=1)


def _hy_conv_kernel(z_ref, x1_ref, x2_ref, kk_ref, o_ref, g_ref, u_ref, acc_ref):
    L = z_ref.shape[2]

    def compute(ci, slot):
        y0 = _hy_toeplitz(z_ref[ci].astype(F32), g_ref.at[slot, 0], u_ref, acc_ref, L)
        z1 = x1_ref[ci].astype(F32) * y0
        y1 = _hy_toeplitz(z1, g_ref.at[slot, 1], u_ref, acc_ref, L)
        o_ref[ci] = (x2_ref[ci].astype(F32) * y1).astype(BF16)

    def build(ci, slot):
        _hy_build_tables(kk_ref, ci, g_ref.at[slot, 0], g_ref.at[slot, 1], L)

    build(0, 0)

    def body(k, carry):
        build(2 * k + 1, 1)
        compute(2 * k, 0)
        build(jnp.minimum(2 * k + 2, HY_CB - 1), 0)
        compute(2 * k + 1, 1)
        return carry

    lax.fori_loop(0, HY_CB // 2, body, 0, unroll=4)


def _hy_conv(zc3, kk):
    _, B, L = zc3.shape
    nblk = W_BR // HY_CB
    bp = -(-B // BF16_ROWS) * BF16_ROWS
    nsb = L // HY_KT
    act = lambda off: pl.BlockSpec((HY_CB, B, L), lambda c: (c + off * nblk, 0, 0))
    return pl.pallas_call(
        _hy_conv_kernel,
        out_shape=jax.ShapeDtypeStruct((W_BR, B, L), BF16),
        grid=(nblk,),
        in_specs=[act(0), act(1), act(2),
                  pl.BlockSpec((2, HY_CB, 2 * L), lambda c: (0, c, 0))],
        out_specs=pl.BlockSpec((HY_CB, B, L), lambda c: (c, 0, 0)),
        scratch_shapes=[pltpu.VMEM((2, 2, 2 * L - LANES, LANES), BF16),
                        pltpu.VMEM((nsb * bp, HY_KT), BF16),
                        pltpu.VMEM((nsb * bp, HY_KT), F32)],
        compiler_params=_params(("parallel",), 48),
        name="hy_conv",
    )(zc3, zc3, zc3, kk)


HY_POST_TT = 8192


def _hy_post_kernel(y_ref, o_ref):
    o_ref[...] = y_ref[...].astype(F32).T.astype(BF16)


def _hy_post(yc):
    C, T = yc.shape
    return pl.pallas_call(
        _hy_post_kernel,
        out_shape=jax.ShapeDtypeStruct((T, C), BF16),
        grid=(T // HY_POST_TT, C // LANES),
        in_specs=[pl.BlockSpec((LANES, HY_POST_TT), lambda t, c: (c, t))],
        out_specs=pl.BlockSpec((HY_POST_TT, LANES), lambda t, c: (t, c)),
        compiler_params=_params(("parallel", "parallel"), 48),
        name="hy_post",
    )(yc)


def _sgu_kernel(u_ref, v_ref, g_ref, b_ref, ws_ref, bs_ref, o_ref):
    L = u_ref.shape[0]
    gw = W_BR // SGU_GROUPS

    def body(n, carry):
        rows = pl.ds(pl.multiple_of(n * SGU_CHUNK, SGU_CHUNK), SGU_CHUNK)
        v = v_ref[rows, :].astype(F32)
        mu = jnp.mean(v, axis=-1, keepdims=True)
        d = v - mu
        var = jnp.mean(d * d, axis=-1, keepdims=True)
        vn = (d * lax.rsqrt(var + EPS) * g_ref[...] + b_ref[...]).astype(BF16)
        mixed = jnp.concatenate(
            [jnp.dot(ws_ref[k], vn[:, gw * k:gw * (k + 1)], preferred_element_type=F32) + bs_ref[k]
             for k in range(SGU_GROUPS)], axis=1)
        o_ref[rows, :] = (u_ref[rows, :].astype(F32) * mixed).astype(BF16)
        return carry

    lax.fori_loop(0, L // SGU_CHUNK, body, 0, unroll=4)


def _sgu(proj3, ln_g, ln_b, ws, bs):
    B, L, _ = proj3.shape
    cb = COL_SGU // W_BR
    return pl.pallas_call(
        _sgu_kernel,
        out_shape=jax.ShapeDtypeStruct((B, L, W_BR), BF16),
        grid=(B,),
        in_specs=[pl.BlockSpec((None, L, W_BR), lambda b: (b, 0, cb)),
                  pl.BlockSpec((None, L, W_BR), lambda b: (b, 0, cb + 1)),
                  pl.BlockSpec((1, W_BR), lambda b: (0, 0)),
                  pl.BlockSpec((1, W_BR), lambda b: (0, 0)),
                  pl.BlockSpec((SGU_GROUPS, SGU_CHUNK, SGU_CHUNK), lambda b: (0, 0, 0)),
                  pl.BlockSpec((SGU_GROUPS, SGU_CHUNK, 1), lambda b: (0, 0, 0))],
        out_specs=pl.BlockSpec((None, L, W_BR), lambda b: (b, 0, 0)),
        compiler_params=_params(("parallel",), 48),
        name="sgu",
    )(proj3, proj3, ln_g, ln_b, ws, bs)


RG_SLABS = 2
RG_TL = 256
RG_PAD = 8
RG_SKEW = 8
RG_UNROLL = 8


def _rg_kernel(xb_ref, gb_ref, cw_ref, cb_ref, w_ref, bias_ref, lam_ref, o_ref,
               xpad_ref, nat_ref, xi_ref, af_ref, bf_ref, ab_ref, bb_ref):
    L = xb_ref.shape[0]
    seg = L // SUBLANES
    pitch = seg + RG_SKEW
    zeros = jnp.zeros((RG_PAD, LANES), F32)
    sp_all = []
    for s in range(RG_SLABS):
        cols = slice(s * LANES, (s + 1) * LANES)
        xpad_ref[s, 0:RG_PAD, :] = zeros
        xpad_ref[s, RG_PAD + L:RG_PAD + L + RG_PAD, :] = zeros
        xpad_ref[s, RG_PAD:RG_PAD + L, :] = xb_ref[:, cols].astype(F32)
        cw = cw_ref[:, cols]
        for j in range(SUBLANES):
            t0 = j * seg
            xc = cb_ref[:, cols]
            for k in range(4):
                xc = xc + cw[k:k + 1] * xpad_ref[s, RG_PAD + t0 + k - 2:RG_PAD + t0 + k - 2 + seg, :]
            nat_ref[s, pitch * j:pitch * j + seg, :] = xc
        lam = lam_ref[:, cols]
        sp_all.append(jnp.maximum(-lam, 0.0) + jnp.log(1.0 + jnp.exp(-jnp.abs(lam))))

    def interleave(r, carry):
        dst = pl.ds(pl.multiple_of(r * SUBLANES, SUBLANES), SUBLANES)
        for s in range(RG_SLABS):
            xi_ref[s, dst, :] = nat_ref[s, pl.ds(r, SUBLANES, stride=pitch), :]
        return carry

    lax.fori_loop(0, seg, interleave, 0, unroll=RG_UNROLL)

    for s in range(RG_SLABS):
        for ti in range(L // RG_TL):
            rows = slice(ti * RG_TL, (ti + 1) * RG_TL)
            xc = xi_ref[s, rows, :]
            g = jnp.dot(xc.astype(BF16), w_ref[s], preferred_element_type=F32) + bias_ref[s]
            g = 0.5 * jnp.tanh(0.5 * g) + 0.5
            for d, (a_ref, b_ref) in enumerate(((af_ref, bf_ref), (ab_ref, bb_ref))):
                r = g[:, 2 * d * LANES:(2 * d + 1) * LANES]
                i = g[:, (2 * d + 1) * LANES:(2 * d + 2) * LANES]
                a = jnp.exp((-RG_C) * r * sp_all[s][d:d + 1])
                om = 1.0 - a * a
                a_ref[s, rows, :] = a
                b_ref[s, rows, :] = (om * lax.rsqrt(jnp.maximum(om, 1e-30))) * (i * xc)

    def rows_f(r):
        return pl.ds(pl.multiple_of(r * SUBLANES, SUBLANES), SUBLANES)

    def rows_b(r):
        return pl.ds(pl.multiple_of((seg - 1 - r) * SUBLANES, SUBLANES), SUBLANES)

    def pass1(r, carry):
        out = []
        for s in range(RG_SLABS):
            hf, pf, hb, pb = carry[4 * s:4 * s + 4]
            a = af_ref[s, rows_f(r), :]
            hf = a * hf + bf_ref[s, rows_f(r), :]
            pf = a * pf
            a = ab_ref[s, rows_b(r), :]
            hb = a * hb + bb_ref[s, rows_b(r), :]
            pb = a * pb
            out += [hf, pf, hb, pb]
        return tuple(out)

    z = jnp.zeros((SUBLANES, LANES), F32)
    one = jnp.ones((SUBLANES, LANES), F32)
    ends = lax.fori_loop(0, seg, pass1, (z, one, z, one) * RG_SLABS, unroll=RG_UNROLL)

    init = []
    for s in range(RG_SLABS):
        hf, pf, hb, pb = ends[4 * s:4 * s + 4]
        c = jnp.zeros((1, LANES), F32)
        rows = []
        for j in range(SUBLANES):
            rows.append(c)
            c = pf[j:j + 1] * c + hf[j:j + 1]
        init.append(jnp.concatenate(rows, axis=0))
        c = jnp.zeros((1, LANES), F32)
        rows = []
        for j in range(SUBLANES - 1, -1, -1):
            rows.append(c)
            c = pb[j:j + 1] * c + hb[j:j + 1]
        init.append(jnp.concatenate(rows[::-1], axis=0))

    def pass2(r, carry):
        out = []
        for s in range(RG_SLABS):
            hf, hb = carry[2 * s:2 * s + 2]
            hf = af_ref[s, rows_f(r), :] * hf + bf_ref[s, rows_f(r), :]
            xi_ref[s, rows_f(r), :] = hf
            hb = ab_ref[s, rows_b(r), :] * hb + bb_ref[s, rows_b(r), :]
            xpad_ref[s, rows_b(r), :] = hb
            out += [hf, hb]
        return tuple(out)

    lax.fori_loop(0, seg, pass2, tuple(init), unroll=RG_UNROLL)

    def deinterleave(r, carry):
        src = rows_f(r)
        for s in range(RG_SLABS):
            nat_ref[s, pl.ds(r, SUBLANES, stride=pitch), :] = xi_ref[s, src, :] + xpad_ref[s, src, :]
        return carry

    lax.fori_loop(0, seg, deinterleave, 0, unroll=RG_UNROLL)

    for s in range(RG_SLABS):
        cols = slice(s * LANES, (s + 1) * LANES)
        for j in range(SUBLANES):
            gb = gb_ref[j * seg:(j + 1) * seg, cols].astype(F32)
            gelu = 0.5 * gb * (1.0 + jnp.tanh(math.sqrt(2.0 / math.pi) * (gb + 0.044715 * (gb * gb * gb))))
            o_ref[j * seg:(j + 1) * seg, cols] = (nat_ref[s, pitch * j:pitch * j + seg, :] * gelu).astype(BF16)


def _rglru(proj3, conv_w, conv_b, w_bd, bias_bd, lam):
    B, L, _ = proj3.shape
    gw = RG_SLABS * LANES
    ng = W_BR // gw
    cx = COL_RG // gw
    seg = L // SUBLANES
    slab = lambda rows: pltpu.VMEM((RG_SLABS, rows, LANES), F32)
    return pl.pallas_call(
        _rg_kernel,
        out_shape=jax.ShapeDtypeStruct((B, L, W_BR), BF16),
        grid=(B, ng),
        in_specs=[pl.BlockSpec((None, L, gw), lambda b, c: (b, 0, cx + c)),
                  pl.BlockSpec((None, L, gw), lambda b, c: (b, 0, cx + ng + c)),
                  pl.BlockSpec((4, gw), lambda b, c: (0, c)),
                  pl.BlockSpec((1, gw), lambda b, c: (0, c)),
                  pl.BlockSpec((RG_SLABS, LANES, 4 * LANES), lambda b, c: (c, 0, 0)),
                  pl.BlockSpec((RG_SLABS, 1, 4 * LANES), lambda b, c: (c, 0, 0)),
                  pl.BlockSpec((2, gw), lambda b, c: (0, c))],
        out_specs=pl.BlockSpec((None, L, gw), lambda b, c: (b, 0, c)),
        scratch_shapes=[slab(L + 2 * RG_PAD), slab(SUBLANES * (seg + RG_SKEW))] + [slab(L)] * 5,
        compiler_params=_params(("parallel", "parallel"), 48),
        name="rglru",
    )(proj3, proj3, conv_w, conv_b, w_bd, bias_bd, lam)


def _log_sigmoid(x):
    return jnp.minimum(x, 0.0) - jnp.log(1.0 + jnp.exp(-jnp.abs(x)))


ML_AUG = 16


def _ml_kernel(qt_ref, vt_ref, ot_ref, k_ref, gtt_ref, gbias_ref, ng_ref, y_ref,
               hf_ref, hb_ref, li_ref, b_ref, cc_ref, *state_refs):
    L = k_ref.shape[0]
    nc = L // ML_CHUNK
    ng2 = 2 * ML_HEADS
    row = lax.broadcasted_iota(jnp.int32, (ML_CHUNK, ML_CHUNK), 0)
    col = lax.broadcasted_iota(jnp.int32, (ML_CHUNK, ML_CHUNK), 1)
    scale = ML_HD ** -0.5
    masks = (col <= row, col >= row)
    for ref in state_refs:
        ref[...] = jnp.zeros(ref.shape, F32)

    li = gtt_ref[0:ng2, :] + gbias_ref[0:ng2, :]
    lf = _log_sigmoid(gtt_ref[ng2:2 * ng2, :] + gbias_ref[ng2:2 * ng2, :])
    pos = lax.broadcasted_iota(jnp.int32, (ng2, L), 1) & (ML_CHUNK - 1)
    pre_sum = lf
    suf_sum = lf
    k = 1
    while k < ML_CHUNK:
        pre_sum = pre_sum + jnp.where(pos >= k, pltpu.roll(pre_sum, k, 1), 0.0)
        suf_sum = suf_sum + jnp.where(pos < ML_CHUNK - k, pltpu.roll(suf_sum, L - k, 1), 0.0)
        k *= 2
    causal_row = lax.broadcasted_iota(jnp.int32, (ng2, L), 0) < ML_HEADS
    b_all = jnp.where(causal_row, pre_sum, suf_sum)
    li_ref[...] = li
    b_ref[...] = b_all
    diff = jnp.concatenate([li - b_all, jnp.zeros((LANES - ng2, L), F32)], axis=0)
    for c in range(nc):
        cc_ref[c * ML_CHUNK:(c + 1) * ML_CHUNK, :] = diff[:, c * ML_CHUNK:(c + 1) * ML_CHUNK].T

    def body(step, carry):
        for d in range(2):
            c = step if d == 0 else nc - 1 - step
            tsl = pl.ds(pl.multiple_of(c * ML_CHUNK, ML_CHUNK), ML_CHUNK)
            mask_st = masks[1 - d]
            last = ML_CHUNK - 1 if d == 0 else 0
            for hd in range(ML_HEADS):
                r = d * ML_HEADS + hd
                c_ref, m_ref = state_refs[2 * r], state_refs[2 * r + 1]
                b_row = b_ref[r:r + 1, tsl]
                li_row = li_ref[r:r + 1, tsl]
                c_col = cc_ref[tsl, r:r + 1]
                g_tot = b_row[:, last:last + 1]
                m_prev = m_ref[...]
                dlog = jnp.where(mask_st, b_row + c_col, -jnp.inf)
                inter = b_row + m_prev
                m_t = jnp.maximum(inter, jnp.max(dlog, axis=0, keepdims=True))
                w_intra = jnp.exp(dlog - m_t)
                w_inter = jnp.exp(inter - m_t)
                hs = slice(hd * ML_HD, (hd + 1) * ML_HD)
                qt = qt_ref[hs, tsl]
                kh = k_ref[tsl, hs]
                vt = vt_ref[hs, tsl]
                s = jnp.dot(kh, qt, preferred_element_type=F32) * (scale * w_intra)
                ca = c_ref[...]
                qc = jnp.dot(ca.astype(BF16), qt, preferred_element_type=F32) * scale
                num = w_inter * qc[0:ML_HD] + jnp.dot(vt, s.astype(BF16), preferred_element_type=F32)
                den = w_inter * qc[ML_HD:ML_HD + 1] + jnp.sum(s, axis=0, keepdims=True)
                hout = num * (1.0 / jnp.maximum(jnp.abs(den), jnp.exp(-m_t)))
                if d == 0:
                    hf_ref[hs, tsl] = hout
                else:
                    hb_ref[hs, tsl] = hout
                wlog = g_tot - b_row + li_row
                m_new = jnp.maximum(g_tot + m_prev, jnp.max(wlog, axis=1, keepdims=True))
                decay = jnp.exp(g_tot + m_prev - m_new)
                ws = jnp.exp(wlog - m_new)
                vw = jnp.concatenate([vt.astype(F32) * ws, jnp.broadcast_to(ws, (ML_AUG, ML_CHUNK))],
                                     axis=0).astype(BF16)
                c_ref[...] = decay * ca + jnp.dot(vw, kh, preferred_element_type=F32)
                m_ref[...] = m_new
        return carry

    lax.fori_loop(0, nc, body, 0, unroll=4)

    def finish(c, carry):
        tsl = pl.ds(pl.multiple_of(c * ML_CHUNK, ML_CHUNK), ML_CHUNK)
        for hd in range(ML_HEADS):
            hs = slice(hd * ML_HD, (hd + 1) * ML_HD)
            h = hf_ref[hs, tsl] + hb_ref[hs, tsl]
            hn = h * lax.rsqrt(jnp.mean(h * h, axis=0, keepdims=True) + EPS) * ng_ref[hs, :]
            y = _sigmoid(ot_ref[hs, tsl].astype(F32)) * hn
            y_ref[tsl, hs] = y.T.astype(BF16)
        return carry

    lax.fori_loop(0, nc, finish, 0, unroll=2)


def _mlstm(qvot, proj3, mlgt, gbias, norm_g):
    B, L, _ = proj3.shape
    cb = COL_ML // W_BR
    ng4 = 4 * ML_HEADS
    fm = lambda j: pl.BlockSpec((W_BR, L), lambda b: (j, b))
    return pl.pallas_call(
        _ml_kernel,
        out_shape=jax.ShapeDtypeStruct((B, L, W_BR), BF16),
        grid=(B,),
        in_specs=[fm(0), fm(1), fm(2),
                  pl.BlockSpec((None, L, W_BR), lambda b: (b, 0, cb)),
                  pl.BlockSpec((LANES, L), lambda b: (0, b)),
                  pl.BlockSpec((ng4, 1), lambda b: (0, 0)),
                  pl.BlockSpec((W_BR, LANES), lambda b: (0, 0))],
        out_specs=pl.BlockSpec((None, L, W_BR), lambda b: (b, 0, 0)),
        scratch_shapes=[pltpu.VMEM((W_BR, L), F32),
                        pltpu.VMEM((W_BR, L), F32),
                        pltpu.VMEM((2 * ML_HEADS, L), F32),
                        pltpu.VMEM((2 * ML_HEADS, L), F32),
                        pltpu.VMEM((L, LANES), F32)]
        + [pltpu.VMEM((ML_HD + ML_AUG, ML_HD), F32), pltpu.VMEM((1, 1), F32)] * (2 * ML_HEADS),
        compiler_params=_params(("parallel",), 48),
        name="mlstm",
    )(qvot, qvot, qvot, proj3, mlgt, gbias, jnp.broadcast_to(norm_g.reshape(W_BR, 1), (W_BR, LANES)))


def _merge_kernel(route, gate_ref, ya_ref, yb_ref, yc_ref, yd_ref, wb_ref, wo_ref, x_ref, g_ref, *rest):
    if route:
        rw_ref, rb_ref, xo_ref, hn_ref, comb_ref, combt_ref = rest
    else:
        xo_ref, hn_ref = rest
    merged = None
    for k, y_ref in enumerate((ya_ref, yb_ref, yc_ref, yd_ref)):
        t = jnp.dot(y_ref[...], wb_ref[k], preferred_element_type=F32)
        gk = _sigmoid(gate_ref[:, k * D_MODEL:(k + 1) * D_MODEL].astype(F32))
        merged = gk * t if merged is None else merged + gk * t
    xn = x_ref[...] + jnp.dot(merged.astype(BF16), wo_ref[...], preferred_element_type=F32)
    xo_ref[...] = xn
    h = _rms(xn, g_ref[...])
    h_hi = h.astype(BF16)
    hn_ref[...] = h_hi
    if route:
        h_lo = (h - h_hi.astype(F32)).astype(BF16)
        r_hi = jnp.dot(h_hi, rw_ref[...], preferred_element_type=F32)
        r_lo = jnp.dot(h_lo, rw_ref[:, 0:LANES], preferred_element_type=F32)
        logits = r_hi[:, 0:LANES] + r_hi[:, LANES:2 * LANES] + r_lo + rb_ref[...]
        lane = lax.broadcasted_iota(jnp.int32, logits.shape, 1)
        logits = jnp.where(lane < N_EXPERTS, logits, -jnp.inf)
        v1 = jnp.max(logits, axis=1, keepdims=True)
        i1 = jnp.min(jnp.where(logits == v1, lane, LANES), axis=1, keepdims=True)
        rest_l = jnp.where(lane == i1, -jnp.inf, logits)
        v2 = jnp.max(rest_l, axis=1, keepdims=True)
        i2 = jnp.min(jnp.where(rest_l == v2, lane, LANES), axis=1, keepdims=True)
        e2 = jnp.exp(v2 - v1)
        p1 = 1.0 / (1.0 + e2)
        comb = jnp.where(lane == i1, p1, jnp.where(lane == i2, e2 * p1, 0.0))
        comb_ref[...] = comb
        combt_ref[...] = comb.T


def _merge(proj, ya, yb, yc, yd, wb, wo, x, g, router=None, tm=512):
    T = x.shape[0]
    row = lambda w: pl.BlockSpec((tm, w), lambda i: (i, 0))
    in_specs = [row(N_BRANCH * D_MODEL), row(W_BR), row(W_BR), row(W_BR), row(W_BR),
                pl.BlockSpec((N_BRANCH, W_BR, D_MODEL), lambda i: (0, 0, 0)),
                pl.BlockSpec((D_MODEL, D_MODEL), lambda i: (0, 0)),
                row(D_MODEL),
                pl.BlockSpec((1, D_MODEL), lambda i: (0, 0))]
    out_shape = [jax.ShapeDtypeStruct((T, D_MODEL), F32), jax.ShapeDtypeStruct((T, D_MODEL), BF16)]
    out_specs = [row(D_MODEL), row(D_MODEL)]
    args = [proj, ya, yb, yc, yd, wb, wo, x, g]
    if router is not None:
        in_specs += [pl.BlockSpec((D_MODEL, 2 * LANES), lambda i: (0, 0)),
                     pl.BlockSpec((1, LANES), lambda i: (0, 0))]
        out_shape += [jax.ShapeDtypeStruct((T, LANES), F32), jax.ShapeDtypeStruct((LANES, T), F32)]
        out_specs += [row(LANES), pl.BlockSpec((LANES, tm), lambda i: (0, i))]
        args += list(router)
    return pl.pallas_call(
        functools.partial(_merge_kernel, router is not None),
        out_shape=tuple(out_shape),
        grid=(T // tm,),
        in_specs=in_specs,
        out_specs=tuple(out_specs),
        compiler_params=_params(("parallel",), 56),
        name="merge_route" if router is not None else "merge",
    )(*args)


def _swiglu_acc(h, w1_ref, w3_ref, w2_ref):
    a = jnp.dot(h, w1_ref[...], preferred_element_type=F32)
    b = jnp.dot(h, w3_ref[...], preferred_element_type=F32)
    act = (a * _sigmoid(a) * b).astype(BF16)
    return jnp.dot(act, w2_ref[...], preferred_element_type=F32)


def _ffn_kernel(h_ref, x_ref, w1_ref, w3_ref, w2_ref, o_ref, acc_ref):
    f = pl.program_id(1)

    @pl.when(f == 0)
    def _():
        acc_ref[...] = x_ref[...]

    acc_ref[...] += _swiglu_acc(h_ref[...], w1_ref, w3_ref, w2_ref)

    @pl.when(f == pl.num_programs(1) - 1)
    def _():
        o_ref[...] = acc_ref[...]


def _ffn(hn, x, w1, w3, w2, tm=512, n_split=2):
    T = x.shape[0]
    dff = w1.shape[1]
    tf = dff // n_split
    return pl.pallas_call(
        _ffn_kernel,
        out_shape=jax.ShapeDtypeStruct((T, D_MODEL), F32),
        grid=(T // tm, n_split),
        in_specs=[pl.BlockSpec((tm, D_MODEL), lambda i, f: (i, 0)),
                  pl.BlockSpec((tm, D_MODEL), lambda i, f: (i, 0)),
                  pl.BlockSpec((D_MODEL, tf), lambda i, f: (0, f)),
                  pl.BlockSpec((D_MODEL, tf), lambda i, f: (0, f)),
                  pl.BlockSpec((tf, D_MODEL), lambda i, f: (f, 0))],
        out_specs=pl.BlockSpec((tm, D_MODEL), lambda i, f: (i, 0)),
        scratch_shapes=[pltpu.VMEM((tm, D_MODEL), F32)],
        compiler_params=_params(("parallel", "arbitrary"), 56),
        name="ffn",
    )(hn, x, w1, w3, w2)


MOE_TM = 1024
MOE_CH = 256
MOE_SINGLE = (128, 192, 256, 320, 384)


def _moe_kernel(cnt_ref, h_ref, x_ref, comb_ref, combt_ref, w1_ref, w3_ref, w2_ref, o_ref,
                rank_ref, rankt_ref):
    i = pl.program_id(0)
    e = pl.program_id(1)
    tm = h_ref.shape[0]

    @pl.when(e == 0)
    def _():
        o_ref[...] = x_ref[...]
        r = lax.broadcasted_iota(jnp.int32, (tm, tm), 0)
        c = lax.broadcasted_iota(jnp.int32, (tm, tm), 1)
        before = jnp.where(c < r, 1.0, 0.0).astype(BF16)
        sel = jnp.where(comb_ref[...] > 0.0, 1.0, 0.0).astype(BF16)
        rank_ref[...] = jnp.dot(before, sel, preferred_element_type=F32)
        selt = jnp.where(combt_ref[...] > 0.0, 1.0, 0.0).astype(BF16)
        rankt_ref[...] = lax.dot_general(selt, before, (((1,), (1,)), ((), ())),
                                         preferred_element_type=F32)

    lane = lax.broadcasted_iota(jnp.int32, (tm, LANES), 1)
    comb = comb_ref[...]
    c_col = jnp.sum(jnp.where(lane == e, comb, 0.0), axis=1, keepdims=True)
    rank_col = jnp.sum(jnp.where(lane == e, rank_ref[...], 0.0), axis=1, keepdims=True)
    rank_col = jnp.where(c_col > 0.0, rank_col, -1.0)
    c_row = combt_ref[pl.ds(e, 1), :]
    rank_row = jnp.where(c_row > 0.0, rankt_ref[pl.ds(e, 1), :], -1.0)
    cnt = cnt_ref[i * N_EXPERTS + e]

    def run_chunk(base, size):
        pos_r = (lax.broadcasted_iota(jnp.int32, (size, tm), 0) + base).astype(F32)
        gather = jnp.where(rank_row == pos_r, 1.0, 0.0).astype(BF16)
        xs = jnp.dot(gather, h_ref[...], preferred_element_type=F32).astype(BF16)
        y = _swiglu_acc(xs, w1_ref, w3_ref, w2_ref).astype(BF16)
        pos_c = (lax.broadcasted_iota(jnp.int32, (tm, size), 1) + base).astype(F32)
        scatter = jnp.where(rank_col == pos_c, c_col, 0.0).astype(BF16)
        o_ref[...] += jnp.dot(scatter, y, preferred_element_type=F32)

    lo = 0
    for size in MOE_SINGLE:
        @pl.when((cnt > lo) & (cnt <= size))
        def _(size=size):
            run_chunk(0, size)
        lo = size

    @pl.when(cnt > MOE_SINGLE[-1])
    def _():
        def body(j, carry):
            run_chunk(j * MOE_CH, MOE_CH)
            return carry

        lax.fori_loop(0, (cnt + MOE_CH - 1) // MOE_CH, body, 0)


def _moe(hn, x, comb, combt, w1, w3, w2, tm=MOE_TM):
    T = x.shape[0]
    dfe = w1.shape[2]
    nt = T // tm
    cnt = jnp.sum((comb[:, :N_EXPERTS] > 0.0).reshape(nt, tm, N_EXPERTS), axis=1, dtype=jnp.int32).reshape(-1)
    grid_spec = pltpu.PrefetchScalarGridSpec(
        num_scalar_prefetch=1,
        grid=(nt, N_EXPERTS),
        in_specs=[pl.BlockSpec((tm, D_MODEL), lambda i, e, c: (i, 0)),
                  pl.BlockSpec((tm, D_MODEL), lambda i, e, c: (i, 0)),
                  pl.BlockSpec((tm, LANES), lambda i, e, c: (i, 0)),
                  pl.BlockSpec((LANES, tm), lambda i, e, c: (0, i)),
                  pl.BlockSpec((None, D_MODEL, dfe), lambda i, e, c: (e, 0, 0)),
                  pl.BlockSpec((None, D_MODEL, dfe), lambda i, e, c: (e, 0, 0)),
                  pl.BlockSpec((None, dfe, D_MODEL), lambda i, e, c: (e, 0, 0))],
        out_specs=pl.BlockSpec((tm, D_MODEL), lambda i, e, c: (i, 0)),
        scratch_shapes=[pltpu.VMEM((tm, LANES), F32), pltpu.VMEM((LANES, tm), F32)])
    return pl.pallas_call(
        _moe_kernel,
        out_shape=jax.ShapeDtypeStruct((T, D_MODEL), F32),
        grid_spec=grid_spec,
        compiler_params=_params(("parallel", "arbitrary"), 56),
        name="moe",
    )(cnt, hn, x, comb, combt, w1, w3, w2)


def _final_norm_kernel(x_ref, g_ref, o_ref):
    o_ref[...] = _rms(x_ref[...], g_ref[...])


def _final_norm(x, g, row0, rows, tm=1024):
    first = row0 // tm
    return pl.pallas_call(
        _final_norm_kernel,
        out_shape=jax.ShapeDtypeStruct((rows, D_MODEL), F32),
        grid=(rows // tm,),
        in_specs=[pl.BlockSpec((tm, D_MODEL), lambda i: (i + first, 0)),
                  pl.BlockSpec((1, D_MODEL), lambda i: (0, 0))],
        out_specs=pl.BlockSpec((tm, D_MODEL), lambda i: (i, 0)),
        compiler_params=_params(("parallel",), 48),
        name="final_norm",
    )(x, g)


def _split_w_in(w_in_l):
    n_hy, n_sgu, n_rg = 3 * W_BR, 2 * W_BR, 2 * W_BR
    o1 = n_hy
    o2 = o1 + n_sgu
    o3 = o2 + n_rg
    o4 = o3 + 4 * W_BR
    o5 = o4 + 4 * ML_HEADS
    w = jnp.concatenate([w_in_l[:, o5:], w_in_l[:, :o3], w_in_l[:, o3 + W_BR:o3 + 2 * W_BR]],
                        axis=1).astype(BF16)
    wgt = jnp.pad(w_in_l[:, o4:o5], ((0, 0), (0, LANES - 4 * ML_HEADS))).T.astype(BF16)
    wft = jnp.concatenate([w_in_l[:, o3:o3 + W_BR], w_in_l[:, o3 + 2 * W_BR:o4]], axis=1).T.astype(BF16)
    return w, wgt, wft


def _router_operands(router_w, router_b):
    w = jnp.pad(router_w, ((0, 0), (0, LANES - N_EXPERTS)))
    w_hi = w.astype(BF16)
    w_lo = (w - w_hi.astype(F32)).astype(BF16)
    return (jnp.concatenate([w_hi, w_lo], axis=1),
            jnp.pad(router_b, (0, LANES - N_EXPERTS))[None, :])


def _rg_block_diag(wa, ba, wx, bx):
    hpg = LANES // RG_HD
    ng = RG_HEADS // hpg
    eye = jnp.eye(hpg, dtype=F32)

    def bd(w):
        w = w.reshape(ng, hpg, RG_HD, RG_HD)
        return jnp.einsum('gaij,ab->gaibj', w, eye).reshape(ng, LANES, LANES)

    w = jnp.concatenate([bd(wa[0]), bd(wx[0]), bd(wa[1]), bd(wx[1])], axis=2).astype(BF16)
    fl = lambda b: b.reshape(ng, 1, LANES)
    bias = jnp.concatenate([fl(ba[0]), fl(bx[0]), fl(ba[1]), fl(bx[1])], axis=2)
    return w, bias


def _token_mixer(x, B, L, l, norm_g, w_in, kk, hy_conv_w, hy_conv_b, sgu_ln_g, sgu_ln_b, sgu_ws, sgu_bs,
                 rg_conv_w, rg_conv_b, rg_wa, rg_ba, rg_wx, rg_bx, rg_lambda, ml_i_bias, ml_f_bias,
                 ml_norm_g, w_branch, w_out, next_g, router):
    T = B * L
    w, wgt, wft = _split_w_in(w_in[l])
    proj, mlgt, qvot = _inproj(x, norm_g[l][None, :], w, wgt, wft)
    proj3 = proj.reshape(B, L, N_PROJ)

    zc = _hy_pre(proj3, hy_conv_w[l], hy_conv_b[l][None, :])
    ya_c = _hy_conv(zc.reshape(3 * W_BR, B, L), kk[l])
    ya = _hy_post(ya_c.reshape(W_BR, T))

    yb = _sgu(proj3, sgu_ln_g[l][None, :], sgu_ln_b[l][None, :], sgu_ws[l].astype(BF16),
              sgu_bs[l][:, :, None]).reshape(T, W_BR)

    w_bd, bias_bd = _rg_block_diag(rg_wa[l], rg_ba[l], rg_wx[l], rg_bx[l])
    yc = _rglru(proj3, rg_conv_w[l], rg_conv_b[l][None, :], w_bd, bias_bd, rg_lambda[l]).reshape(T, W_BR)

    gbias = jnp.concatenate([ml_i_bias[l].reshape(-1), ml_f_bias[l].reshape(-1)])[:, None]
    yd = _mlstm(qvot, proj3, mlgt, gbias, ml_norm_g[l]).reshape(T, W_BR)

    return _merge(proj, ya, yb, yc, yd, w_branch[l].astype(BF16), w_out[l].astype(BF16), x,
                  next_g[None, :], router)


def kernel(x_prompt, x_sample, norm_mix_g, w_in, hy_conv_w, hy_conv_b, hy_ffn_w1, hy_ffn_b1, hy_ffn_w2, hy_ffn_b2, hy_ffn_w3, hy_bias, sgu_ln_g, sgu_ln_b, sgu_ws, sgu_bs, rg_conv_w, rg_conv_b, rg_wa, rg_ba, rg_wx, rg_bx, rg_lambda, ml_i_bias, ml_f_bias, ml_norm_g, w_branch, w_out, norm_ffn_g, ffn_w1, ffn_w3, ffn_w2, router_w, router_b, moe_w1, moe_w3, moe_w2, norm_final_g):
    bp, L, _ = x_prompt.shape
    bs = x_sample.shape[0]
    B = bp + bs
    depth = w_in.shape[0]
    x = jnp.concatenate([x_prompt, x_sample], axis=0).reshape(B * L, D_MODEL)
    kk = _hy_filters(hy_ffn_w1, hy_ffn_b1, hy_ffn_w2, hy_ffn_b2, hy_ffn_w3, hy_bias, L)
    for l in range(depth):
        j = l // 2
        router = None
        if l % 2 == 1:
            router = _router_operands(router_w[j], router_b[j])
        outs = _token_mixer(x, B, L, l, norm_mix_g, w_in, kk, hy_conv_w, hy_conv_b, sgu_ln_g, sgu_ln_b,
                            sgu_ws, sgu_bs, rg_conv_w, rg_conv_b, rg_wa, rg_ba, rg_wx, rg_bx, rg_lambda,
                            ml_i_bias, ml_f_bias, ml_norm_g, w_branch, w_out, norm_ffn_g[l], router)
        if router is None:
            x, hn = outs
            x = _ffn(hn, x, ffn_w1[j].astype(BF16), ffn_w3[j].astype(BF16), ffn_w2[j].astype(BF16))
        else:
            x, hn, comb, combt = outs
            x = _moe(hn, x, comb, combt, moe_w1[j].astype(BF16), moe_w3[j].astype(BF16),
                     moe_w2[j].astype(BF16))
    g = norm_final_g[None, :]
    return (_final_norm(x, g, 0, bp * L).reshape(bp, L, D_MODEL),
            _final_norm(x, g, bp * L, bs * L).reshape(bs, L, D_MODEL))
```
